```python
import jax, jax.numpy as jnp
from jax import lax
import numpy as np

D_MODEL = 1024
BATCH = 8
SEQ = 8192
DEPTH = 1

CTX_LEN = 256
GRID_W = 64
N_MOD = 9
EPS = 1e-6
D_FF = 2816
ATTN_HEADS = 8
ATTN_KV_HEADS = 2
HEAD_DIM = 64
ROPE_AXIS_DIM = HEAD_DIM // 2
ROPE_THETA = 10000.0
BLOCK_Q = 128
GLA_HEADS = 4
GLA_DK = 64
GLA_DV = 128
GLA_RANK = 16
GLA_GATE_NORM = 16.0
GLA_CHUNK = 64
ATTN_Q_W = ATTN_HEADS * HEAD_DIM
ATTN_KV_W = ATTN_KV_HEADS * HEAD_DIM
GLA_K_W = GLA_HEADS * GLA_DK
GLA_V_W = GLA_HEADS * GLA_DV
N_BRANCH = 2
IN_SIZES = (ATTN_Q_W, ATTN_KV_W, ATTN_KV_W, GLA_K_W, GLA_K_W, GLA_V_W, GLA_V_W, 2 * GLA_RANK, N_BRANCH * D_MODEL)
IN_SPLITS = tuple(int(v) for v in np.cumsum(IN_SIZES)[:-1])
IN_WIDTH = int(sum(IN_SIZES))

kernel_name = "hybrid_gqa_gla_macaron_dit_layer"


def rms_norm(x, g):
    xf = x.astype(jnp.float32)
    y = xf * lax.rsqrt(jnp.mean(xf * xf, axis=-1, keepdims=True) + EPS)
    return y.astype(x.dtype) * g


def modulate(n, shift, scale):
    return n * (1 + scale) + shift


def swiglu(n, w_up, w_down):
    a, b = jnp.split(n @ w_up, 2, axis=-1)
    return (jax.nn.silu(a) * b) @ w_down


def half_ffn(h, mod, i, g, w_up, w_down):
    shift, scale, gate = mod[..., 3 * i, :], mod[..., 3 * i + 1, :], mod[..., 3 * i + 2, :]
    return h + 0.5 * gate * swiglu(modulate(rms_norm(h, g), shift, scale), w_up, w_down)


def axial_rope(T):
    rows = T // GRID_W
    row = jnp.repeat(jnp.arange(rows, dtype=jnp.float32), GRID_W)
    col = jnp.tile(jnp.arange(GRID_W, dtype=jnp.float32), rows)
    freqs = ROPE_THETA ** (-jnp.arange(0, ROPE_AXIS_DIM, 2, dtype=jnp.float32) / ROPE_AXIS_DIM)
    ang = jnp.concatenate([row[:, None] * freqs, col[:, None] * freqs], axis=-1)
    return jnp.cos(ang)[:, None, :], jnp.sin(ang)[:, None, :]


def apply_rope(x, cos, sin):
    xf = x.astype(jnp.float32)
    x1, x2 = xf[..., :HEAD_DIM // 2], xf[..., HEAD_DIM // 2:]
    return jnp.concatenate([x1 * cos - x2 * sin, x1 * sin + x2 * cos], axis=-1).astype(x.dtype)


def attn_heads(part, n_heads, g):
    return rms_norm(part.reshape(*part.shape[:-1], n_heads, HEAD_DIM), g)


def block_attention(q, k, v):
    B, T, Hq, hd = q.shape
    Hkv = k.shape[2]
    G = Hq // Hkv
    nblk = T // BLOCK_Q
    qb = q.reshape(B, nblk, BLOCK_Q, Hkv, G, hd).transpose(1, 0, 3, 4, 2, 5)
    kt = k.transpose(0, 2, 1, 3)
    vt = v.transpose(0, 2, 1, 3)
    scale = hd ** -0.5

    def one_block(qi):
        s = jnp.einsum('bkgqd,bksd->bkgqs', qi, kt).astype(jnp.float32) * scale
        p = jax.nn.softmax(s, axis=-1).astype(vt.dtype)
        return jnp.einsum('bkgqs,bksd->bkgqd', p, vt)

    o = lax.map(one_block, qb)
    return o.transpose(1, 0, 4, 2, 3, 5).reshape(B, T, Hq * hd)


def to_heads(a, n_heads):
    B, T, _ = a.shape
    return a.reshape(B, T, n_heads, -1).transpose(0, 2, 1, 3)


def gla_chunked(q, k, v, log_a, s0):
    dtype = v.dtype
    q, k, v, log_a = (a.astype(jnp.float32) for a in (q, k, v, log_a))
    B, H, T, dk = q.shape
    dv = v.shape[-1]
    n = T // GLA_CHUNK

    def to_chunks(a):
        return jnp.moveaxis(a.reshape(B, H, n, GLA_CHUNK, a.shape[-1]), 2, 0)

    idx = jnp.arange(GLA_CHUNK)
    lower = (idx[:, None] >= idx[None, :])[:, :, None]

    def step(s, inp):
        qi, ki, vi, gi = inp
        b = jnp.cumsum(gi, axis=2)
        o_inter = jnp.einsum('bhcd,bhde->bhce', qi * jnp.exp(b), s)
        diff = b[:, :, :, None, :] - b[:, :, None, :, :]
        decay = jnp.exp(jnp.where(lower, diff, -jnp.inf))
        A = jnp.einsum('bhid,bhjd,bhijd->bhij', qi, ki, decay)
        o_intra = jnp.einsum('bhij,bhje->bhie', A, vi)
        b_last = b[:, :, -1, :]
        s_new = jnp.exp(b_last)[..., None] * s + jnp.einsum(
            'bhcd,bhce->bhde', ki * jnp.exp(b_last[:, :, None, :] - b), vi)
        return s_new, o_inter + o_intra

    s_fin, o = lax.scan(step, s0, (to_chunks(q), to_chunks(k), to_chunks(v), to_chunks(log_a)))
    o = jnp.moveaxis(o, 0, 2).reshape(B, H, T, dv)
    return o.astype(dtype), s_fin


def gla_bidirectional(parts, w_dec, b_dec, s0_f, s0_b):
    gq = to_heads(parts[3], GLA_HEADS) * (GLA_DK ** -0.5)
    gk = to_heads(parts[4], GLA_HEADS)
    gv = to_heads(parts[5], GLA_HEADS)
    low_f, low_b = jnp.split(parts[7], 2, axis=-1)
    la_f = to_heads(jax.nn.log_sigmoid((low_f @ w_dec[0] + b_dec[0]).astype(jnp.float32)) / GLA_GATE_NORM, GLA_HEADS)
    la_b = to_heads(jax.nn.log_sigmoid((low_b @ w_dec[1] + b_dec[1]).astype(jnp.float32)) / GLA_GATE_NORM, GLA_HEADS)
    o_f, s_f = gla_chunked(gq, gk, gv, la_f, s0_f)
    flip = lambda a: jnp.flip(a, axis=2)
    o_b, s_b = gla_chunked(flip(gq), flip(gk), flip(gv), flip(la_b), s0_b)
    return o_f + flip(o_b), s_f, s_b


def merge_branches(parts, attn_o, gla_o, g_gla, w_branch, b_gate, w_out):
    B, H, T, _ = gla_o.shape
    gla_o = rms_norm(gla_o.transpose(0, 2, 1, 3), g_gla).reshape(B, T, GLA_V_W) * jax.nn.silu(parts[6])
    y_attn = attn_o @ w_branch[0]
    y_gla = gla_o @ w_branch[1]
    g_attn, g_gla_b = jnp.split(jax.nn.sigmoid(parts[8] + b_gate), 2, axis=-1)
    return (g_attn * y_attn + g_gla_b * y_gla) @ w_out


def setup_inputs(seed: int = 0) -> dict:
    key = jax.random.key(seed)
    ks = jax.random.split(key, 19)

    def nrm(k, shape, scale):
        return jax.random.normal(k, shape, jnp.float32) * scale

    D = D_MODEL
    return {
        "x": nrm(ks[0], (BATCH, SEQ, D), 1.0),
        "c": nrm(ks[1], (BATCH, D), 1.0),
        "ctx": nrm(ks[2], (BATCH, CTX_LEN, D), 1.0),
        "c_ctx": nrm(ks[3], (D,), 1.0),
        "w_mod": nrm(ks[4], (DEPTH, D, N_MOD * D), 0.5 * D ** -0.5),
        "b_mod": nrm(ks[5], (DEPTH, N_MOD * D), 0.02),
        "g_norm": 1.0 + nrm(ks[6], (DEPTH, 3, D), 0.02),
        "w_ffn_up": nrm(ks[7], (DEPTH, 2, D, 2 * D_FF), D ** -0.5),
        "w_ffn_down": nrm(ks[8], (DEPTH, 2, D_FF, D), D_FF ** -0.5),
        "w_in": nrm(ks[9], (DEPTH, D, IN_WIDTH), D ** -0.5),
        "g_q": 1.0 + nrm(ks[10], (DEPTH, HEAD_DIM), 0.02),
        "g_k": 1.0 + nrm(ks[11], (DEPTH, HEAD_DIM), 0.02),
        "w_decay": nrm(ks[12], (DEPTH, 2, GLA_RANK, GLA_K_W), GLA_RANK ** -0.5),
        "b_decay": nrm(ks[13], (DEPTH, 2, GLA_K_W), 0.1),
        "g_gla": 1.0 + nrm(ks[14], (DEPTH, GLA_DV), 0.02),
        "w_branch": nrm(ks[15], (DEPTH, N_BRANCH, ATTN_Q_W, D), ATTN_Q_W ** -0.5),
        "b_gate": nrm(ks[16], (DEPTH, N_BRANCH * D), 0.1),
        "w_out": nrm(ks[17], (DEPTH, D, D), D ** -0.5),
        "g_final": 1.0 + nrm(ks[18], (D,), 0.02),
    }


def reference(x, c, ctx, c_ctx, w_mod, b_mod, g_norm, w_ffn_up, w_ffn_down, w_in, g_q, g_k,
              w_decay, b_decay, g_gla, w_branch, b_gate, w_out, g_final):
    B, T, _ = x.shape
    cos, sin = axial_rope(T)
    s_zero = jnp.zeros((B, GLA_HEADS, GLA_DK, GLA_DV), jnp.float32)
    h, hc = x, ctx
    for l in range(DEPTH):
        last = l == DEPTH - 1
        m_lat = (jax.nn.silu(c) @ w_mod[l] + b_mod[l]).reshape(B, 1, N_MOD, D_MODEL)
        m_ctx = (jax.nn.silu(c_ctx) @ w_mod[l] + b_mod[l]).reshape(N_MOD, D_MODEL)

        h = half_ffn(h, m_lat, 0, g_norm[l, 0], w_ffn_up[l, 0], w_ffn_down[l, 0])
        hc = half_ffn(hc, m_ctx, 0, g_norm[l, 0], w_ffn_up[l, 0], w_ffn_down[l, 0])

        n_lat = modulate(rms_norm(h, g_norm[l, 1]), m_lat[..., 3, :], m_lat[..., 4, :])
        n_ctx = modulate(rms_norm(hc, g_norm[l, 1]), m_ctx[3], m_ctx[4])
        p_lat = jnp.split(n_lat @ w_in[l], IN_SPLITS, axis=-1)
        p_ctx = jnp.split(n_ctx @ w_in[l], IN_SPLITS, axis=-1)

        k_ctx = attn_heads(p_ctx[1], ATTN_KV_HEADS, g_k[l])
        v_ctx = p_ctx[2].reshape(*p_ctx[2].shape[:-1], ATTN_KV_HEADS, HEAD_DIM)
        q_lat = apply_rope(attn_heads(p_lat[0], ATTN_HEADS, g_q[l]), cos, sin)
        k_lat = apply_rope(attn_heads(p_lat[1], ATTN_KV_HEADS, g_k[l]), cos, sin)
        v_lat = p_lat[2].reshape(B, T, ATTN_KV_HEADS, HEAD_DIM)
        attn_lat = block_attention(q_lat, jnp.concatenate([k_lat, k_ctx], axis=1),
                                   jnp.concatenate([v_lat, v_ctx], axis=1))

        gla_ctx, s_f, s_b = gla_bidirectional(p_ctx, w_decay[l], b_decay[l], s_zero, s_zero)
        gla_lat, _, _ = gla_bidirectional(p_lat, w_decay[l], b_decay[l], s_f, s_b)

        h = h + m_lat[..., 5, :] * merge_branches(p_lat, attn_lat, gla_lat, g_gla[l], w_branch[l], b_gate[l], w_out[l])

        if not last:
            q_ctx = attn_heads(p_ctx[0], ATTN_HEADS, g_q[l])
            attn_ctx = block_attention(q_ctx, k_ctx, v_ctx)
            hc = hc + m_ctx[5] * merge_branches(p_ctx, attn_ctx, gla_ctx, g_gla[l], w_branch[l], b_gate[l], w_out[l])
            hc = half_ffn(hc, m_ctx, 2, g_norm[l, 2], w_ffn_up[l, 1], w_ffn_down[l, 1])

        h = half_ffn(h, m_lat, 2, g_norm[l, 2], w_ffn_up[l, 1], w_ffn_down[l, 1])
    return rms_norm(h, g_final)
```

```python
import functools

import numpy as np
import jax
import jax.numpy as jnp
from jax import lax
from jax.experimental import pallas as pl
from jax.experimental.pallas import tpu as pltpu

F32 = jnp.float32
BF16 = jnp.bfloat16

EPS = 1e-6
GRID_W = 64
N_MOD = 9
ATTN_HEADS = 8
ATTN_KV_HEADS = 2
HEAD_DIM = 64
ROPE_AXIS_DIM = HEAD_DIM // 2
ROPE_THETA = 10000.0
GLA_HEADS = 4
GLA_DK = 64
GLA_DV = 128
GLA_RANK = 16
GLA_GATE_NORM = 16.0
ATTN_Q_W = ATTN_HEADS * HEAD_DIM
ATTN_KV_W = ATTN_KV_HEADS * HEAD_DIM
GLA_K_W = GLA_HEADS * GLA_DK
GLA_V_W = GLA_HEADS * GLA_DV
Q_GROUP = ATTN_HEADS // ATTN_KV_HEADS

LANES = 128
GLA_CHUNK = 128
VMEM_LIMIT = 56 * 1024 * 1024


def _dot(a, b):
    return jnp.dot(a, b, preferred_element_type=F32)


def _dot_nt(a, b):
    return lax.dot_general(a, b, (((1,), (1,)), ((), ())), preferred_element_type=F32)


def _sigmoid(x):
    return 1.0 / (1.0 + jnp.exp(-x))


def _split2(x):
    hi = x.astype(BF16)
    lo = (x - hi.astype(F32)).astype(BF16)
    return hi, lo


def _rms(x):
    return x * lax.rsqrt(jnp.mean(x * x, axis=-1, keepdims=True) + EPS)


def _const_spec(shape):
    nd = len(shape)
    return pl.BlockSpec(shape, lambda *_: (0,) * nd, pipeline_mode=pl.Buffered(1))


def _mod_kernel(c_ref, w_ref, b_ref, o_ref):
    c = c_ref[...]
    s_hi, s_lo = _split2(c * _sigmoid(c))
    w_hi, w_lo = _split2(w_ref[...])
    o_ref[...] = _dot(s_hi, w_hi) + _dot(s_hi, w_lo) + _dot(s_lo, w_hi) + b_ref[...]


def _modulation(c_rows, w_mod, b_mod):
    rows, d = c_rows.shape
    n = w_mod.shape[1]
    tn = 1024
    return pl.pallas_call(
        _mod_kernel,
        grid=(n // tn,),
        in_specs=[pl.BlockSpec((rows, d), lambda j: (0, 0)),
                  pl.BlockSpec((d, tn), lambda j: (0, j)),
                  pl.BlockSpec((1, tn), lambda j: (0, j))],
        out_specs=pl.BlockSpec((rows, tn), lambda j: (0, j)),
        out_shape=jax.ShapeDtypeStruct((rows, n), F32),
        compiler_params=pltpu.CompilerParams(dimension_semantics=("arbitrary",),
                                             vmem_limit_bytes=VMEM_LIMIT),
        name="modulation",
    )(c_rows, w_mod, b_mod.reshape(1, n))


def _ffn_kernel(*refs, n_chunks, final):
    if final:
        h_ref, mod_ref, g_ref, wa_ref, wb_ref, wd_ref, gf_ref, o_ref = refs
    else:
        h_ref, mod_ref, g_ref, wa_ref, wb_ref, wd_ref, o_ref = refs
    x = h_ref[0]
    shift, scale, gate = mod_ref[0, 0:1, :], mod_ref[0, 1:2, :], mod_ref[0, 2:3, :]
    n = ((_rms(x) * g_ref[...]) * (1.0 + scale) + shift).astype(BF16)
    fc = wa_ref.shape[1] // n_chunks
    acc = None
    for c in range(n_chunks):
        a = _dot(n, wa_ref[:, c * fc:(c + 1) * fc])
        b = _dot(n, wb_ref[:, c * fc:(c + 1) * fc])
        act = ((a * _sigmoid(a)) * b).astype(BF16)
        part = _dot(act, wd_ref[c * fc:(c + 1) * fc, :])
        acc = part if acc is None else acc + part
    out = x + 0.5 * gate * acc
    if final:
        out = _rms(out) * gf_ref[...]
    o_ref[0] = out


def _half_ffn(h, mod3, g, wa, wb, wd, g_final=None, *, tm=512, n_chunks=2):
    nb, t, d = h.shape
    f = wa.shape[1]
    tm = min(tm, t)
    final = g_final is not None
    in_specs = [pl.BlockSpec((1, tm, d), lambda b, i: (b, i, 0)),
                pl.BlockSpec((1, 3, d), lambda b, i: (b, 0, 0)),
                _const_spec((1, d)), _const_spec((d, f)), _const_spec((d, f)), _const_spec((f, d))]
    args = [h, mod3, g.reshape(1, d), wa, wb, wd]
    if final:
        in_specs.append(_const_spec((1, d)))
        args.append(g_final.reshape(1, d))
    return pl.pallas_call(
        functools.partial(_ffn_kernel, n_chunks=n_chunks, final=final),
        grid=(nb, t // tm),
        in_specs=in_specs,
        out_specs=pl.BlockSpec((1, tm, d), lambda b, i: (b, i, 0)),
        out_shape=jax.ShapeDtypeStruct((nb, t, d), F32),
        compiler_params=pltpu.CompilerParams(dimension_semantics=("arbitrary", "arbitrary"),
                                             vmem_limit_bytes=VMEM_LIMIT),
        name="half_ffn_final" if final else "half_ffn",
    )(*args)


_C_Q = 0
_C_K = _C_Q + ATTN_Q_W
_C_V = _C_K + ATTN_KV_W
_C_GQ = _C_V + ATTN_KV_W
_C_GK = _C_GQ + GLA_K_W
_C_GV = _C_GK + GLA_K_W
_C_GS = _C_GV + GLA_V_W
_C_GT = _C_GS + GLA_V_W


def _proj_kernel(h_ref, mod_ref, g_ref, w_ref, gqk_ref, seg_ref, cos_ref, sin_ref, wdh_ref, wdl_ref,
                 bdec_ref, bgate_ref,
                 q_ref, kt_ref, v_ref, gq_ref, gk_ref, gv_ref, gvt_ref, laf_ref, lab_ref, gs_ref, gt_ref):
    d = h_ref.shape[2]
    c_low = _C_GT + 2 * d
    x = h_ref[0]
    shift, scale = mod_ref[0, 0:1, :], mod_ref[0, 1:2, :]
    n = ((_rms(x) * g_ref[...]) * (1.0 + scale) + shift).astype(BF16)
    tm = x.shape[0]

    qk = _dot(n, w_ref[:, _C_Q:_C_V])
    seg = seg_ref[...]
    ms = []
    for j in range((ATTN_Q_W + ATTN_KV_W) // LANES):
        sq = qk[:, j * LANES:(j + 1) * LANES]
        hi, lo = _split2(sq * sq)
        ms.append(_dot(hi, seg) + _dot(lo, seg))
    ms = jnp.concatenate(ms, axis=-1)
    qk = qk * lax.rsqrt(ms + EPS) * gqk_ref[...]
    lane = lax.broadcasted_iota(jnp.int32, (tm, LANES), 1)
    first = (lane % HEAD_DIM) < (HEAD_DIM // 2)
    cos, sin = cos_ref[...], sin_ref[...]
    rot = []
    for j in range((ATTN_Q_W + ATTN_KV_W) // LANES):
        xs = qk[:, j * LANES:(j + 1) * LANES]
        other = jnp.where(first, pltpu.roll(xs, LANES - HEAD_DIM // 2, 1), pltpu.roll(xs, HEAD_DIM // 2, 1))
        rot.append(xs * cos + other * sin)
    q_ref[0] = (jnp.concatenate(rot[:ATTN_Q_W // LANES], axis=-1) * (HEAD_DIM ** -0.5)).astype(BF16)
    kt_ref[0] = rot[ATTN_Q_W // LANES].T.astype(BF16)

    v = _dot(n, w_ref[:, _C_V:_C_GQ])
    ones_col = jnp.where(lane == HEAD_DIM, 1.0, 0.0)
    v_ref[0, 0] = jnp.where(lane < HEAD_DIM, v, ones_col).astype(BF16)
    v_ref[0, 1] = jnp.where(lane < HEAD_DIM, pltpu.roll(v, HEAD_DIM, 1), ones_col).astype(BF16)

    gqk = _dot(n, w_ref[:, _C_GQ:_C_GV])
    gq_ref[0] = (gqk[:, :GLA_K_W] * (GLA_DK ** -0.5)).astype(BF16)
    gk_ref[0] = gqk[:, GLA_K_W:].astype(BF16)
    gv = _dot(n, w_ref[:, _C_GV:_C_GS])
    gv_ref[0] = gv.astype(BF16)
    gvt_ref[0] = gv.T.astype(BF16)

    low = _dot(n, w_ref[:, c_low:c_low + LANES])
    l_hi, l_lo = _split2(low)
    z = _dot(l_hi, wdh_ref[...]) + _dot(l_lo, wdh_ref[...]) + _dot(l_hi, wdl_ref[...]) + bdec_ref[...]
    la = (jnp.minimum(z, 0.0) - jnp.log(1.0 + jnp.exp(-jnp.abs(z)))) * (1.0 / GLA_GATE_NORM)
    laf_ref[0] = la[:, :GLA_K_W]
    lab_ref[0] = la[:, GLA_K_W:]

    gs = _dot(n, w_ref[:, _C_GS:_C_GT])
    gs_ref[0] = (gs * _sigmoid(gs)).astype(BF16)
    gt = _dot(n, w_ref[:, _C_GT:c_low]) + bgate_ref[...]
    gt_ref[0] = _sigmoid(gt).astype(BF16)


def _in_proj(h, mod2, g, w_r, gqk, seg, cos_t, sin_t, wd_hi, wd_lo, b_dec, b_gate, *, tm=512):
    nb, t, d = h.shape
    tm = min(tm, t)
    wp = w_r.shape[1]
    tok = lambda w: pl.BlockSpec((1, tm, w), lambda b, i: (b, i, 0))
    out_shape = [
        jax.ShapeDtypeStruct((nb, t, ATTN_Q_W), BF16),
        jax.ShapeDtypeStruct((nb, ATTN_KV_W, t), BF16),
        jax.ShapeDtypeStruct((nb, ATTN_KV_HEADS, t, LANES), BF16),
        jax.ShapeDtypeStruct((nb, t, GLA_K_W), BF16),
        jax.ShapeDtypeStruct((nb, t, GLA_K_W), BF16),
        jax.ShapeDtypeStruct((nb, t, GLA_V_W), BF16),
        jax.ShapeDtypeStruct((nb, GLA_V_W, t), BF16),
        jax.ShapeDtypeStruct((nb, t, GLA_K_W), F32),
        jax.ShapeDtypeStruct((nb, t, GLA_K_W), F32),
        jax.ShapeDtypeStruct((nb, t, GLA_V_W), BF16),
        jax.ShapeDtypeStruct((nb, t, 2 * d), BF16),
    ]
    out_specs = [
        tok(ATTN_Q_W),
        pl.BlockSpec((1, ATTN_KV_W, tm), lambda b, i: (b, 0, i)),
        pl.BlockSpec((1, ATTN_KV_HEADS, tm, LANES), lambda b, i: (b, 0, i, 0)),
        tok(GLA_K_W), tok(GLA_K_W), tok(GLA_V_W),
        pl.BlockSpec((1, GLA_V_W, tm), lambda b, i: (b, 0, i)),
        tok(GLA_K_W), tok(GLA_K_W), tok(GLA_V_W), tok(2 * d),
    ]
    in_specs = [
        pl.BlockSpec((1, tm, d), lambda b, i: (b, i, 0)),
        pl.BlockSpec((1, 2, d), lambda b, i: (b, 0, 0)),
        _const_spec((1, d)), _const_spec((d, wp)), _const_spec((1, ATTN_Q_W + ATTN_KV_W)),
        _const_spec((LANES, LANES)),
        pl.BlockSpec((tm, LANES), lambda b, i: (i, 0)),
        pl.BlockSpec((tm, LANES), lambda b, i: (i, 0)),
        _const_spec((LANES, 2 * GLA_K_W)), _const_spec((LANES, 2 * GLA_K_W)),
        _const_spec((1, 2 * GLA_K_W)), _const_spec((1, 2 * d)),
    ]
    return pl.pallas_call(
        _proj_kernel,
        grid=(nb, t // tm),
        in_specs=in_specs,
        out_specs=out_specs,
        out_shape=out_shape,
        compiler_params=pltpu.CompilerParams(dimension_semantics=("arbitrary", "arbitrary"),
                                             vmem_limit_bytes=VMEM_LIMIT),
        name="in_proj",
    )(h, mod2, g.reshape(1, d), w_r, gqk, seg, cos_t, sin_t, wd_hi, wd_lo, b_dec, b_gate)


def _attn_kernel(q_ref, kt_ref, ktc_ref, v_ref, vc_ref, o_ref, q4_ref, m_ref, acc_ref, *, kc):
    tq = q_ref.shape[1]
    t = kt_ref.shape[2]
    for h in range(Q_GROUP):
        q4_ref[h * tq:(h + 1) * tq, :] = q_ref[0, :, h * HEAD_DIM:(h + 1) * HEAD_DIM]
    m_ref[...] = jnp.full(m_ref.shape, -jnp.inf, F32)
    acc_ref[...] = jnp.zeros(acc_ref.shape, F32)

    def step(kt, v):
        s = _dot(q4_ref[...], kt)
        m_old = m_ref[...]
        m_new = jnp.maximum(m_old, jnp.max(s, axis=-1, keepdims=True))
        p = jnp.exp(s - m_new).astype(BF16)
        acc_ref[...] = jnp.exp(m_old - m_new) * acc_ref[...] + _dot(p, v)
        m_ref[...] = m_new

    step(ktc_ref[0], vc_ref[0, 0])

    def body(c, carry):
        off = pl.multiple_of(c * kc, kc)
        step(kt_ref[0, :, pl.ds(off, kc)], v_ref[0, 0, pl.ds(off, kc), :])
        return carry

    lax.fori_loop(0, t // kc, body, 0)
    acc = acc_ref[...]
    o = acc[:, :HEAD_DIM] / acc[:, HEAD_DIM:HEAD_DIM + 1]
    for h in range(Q_GROUP):
        o_ref[0, :, h * HEAD_DIM:(h + 1) * HEAD_DIM] = o[h * tq:(h + 1) * tq].astype(o_ref.dtype)


def _attention(q, kt, kt_ctx, v, v_ctx, *, tq=256, kc=512):
    nb, t, _ = q.shape
    tc = kt_ctx.shape[2] // nb
    tq = min(tq, t)
    kc = min(kc, t)
    gw = Q_GROUP * HEAD_DIM
    return pl.pallas_call(
        functools.partial(_attn_kernel, kc=kc),
        grid=(nb, ATTN_KV_HEADS, t // tq),
        in_specs=[
            pl.BlockSpec((1, tq, gw), lambda b, g, i: (b, i, g)),
            pl.BlockSpec((1, HEAD_DIM, t), lambda b, g, i: (b, g, 0)),
            pl.BlockSpec((1, HEAD_DIM, tc), lambda b, g, i: (0, g, b)),
            pl.BlockSpec((1, 1, t, LANES), lambda b, g, i: (b, g, 0, 0)),
            pl.BlockSpec((1, 1, tc, LANES), lambda b, g, i: (0, g, b, 0)),
        ],
        out_specs=pl.BlockSpec((1, tq, gw), lambda b, g, i: (b, i, g)),
        out_shape=jax.ShapeDtypeStruct((nb, t, ATTN_Q_W), BF16),
        scratch_shapes=[pltpu.VMEM((Q_GROUP * tq, HEAD_DIM), BF16),
                        pltpu.VMEM((Q_GROUP * tq, 1), F32),
                        pltpu.VMEM((Q_GROUP * tq, LANES), F32)],
        compiler_params=pltpu.CompilerParams(dimension_semantics=("arbitrary", "arbitrary", "arbitrary"),
                                             vmem_limit_bytes=VMEM_LIMIT),
        name="flash_attention",
    )(q, kt, kt_ctx, v, v_ctx)


def _gla_constants():
    c = GLA_CHUNK
    i = np.arange(c)[:, None]
    t = np.arange(c)[None, :]
    fwd = [t <= i, t > i]
    bwd = [t >= i, t < i]
    s = c // 2
    while s >= 1:
        mid = (i // (2 * s)) * (2 * s) + s
        second = (i % (2 * s)) >= s
        fwd.append(np.where(second, (t >= mid) & (t <= i), (t > i) & (t < mid)))
        bwd.append(np.where(second, (t >= mid) & (t < i), (t >= i) & (t < mid)))
        s //= 2
    to = lambda blocks: jnp.asarray(np.concatenate(blocks, axis=0).astype(np.float32), dtype=BF16)
    return to(fwd), to(bwd)


_GLA_LEVELS = int(np.log2(GLA_CHUNK))


def _cum(mat, la):
    hi, lo = _split2(la)
    r = _dot(mat, jnp.concatenate([hi, lo], axis=-1))
    return r[:, :LANES] + r[:, LANES:]


def _gla_kernel(q_ref, k_ref, v_ref, vt_ref, laf_ref, lab_ref, kc_ref, vtc_ref, lafc_ref, labc_ref,
                mf_ref, mb_ref, o_ref, st_ref, dec_ref, *, cpt):
    c = GLA_CHUNK
    n_lat = k_ref.shape[1] // c
    n_ctx = kc_ref.shape[1] // c
    n_all = n_ctx + n_lat
    tile = pl.program_id(2)
    lane = lax.broadcasted_iota(jnp.int32, (c, LANES), 1)
    head_masks = [lane < GLA_DK, lane >= GLA_DK]

    def increments(k_r, vt_r, laf_r, lab_r, n, slot):
        off = pl.multiple_of(n * c, c)
        k = k_r[0, pl.ds(off, c), :].astype(F32)
        rf = _cum(mf_ref[0:2 * c, :], laf_r[0, pl.ds(off, c), :])
        rb = _cum(mb_ref[0:2 * c, :], lab_r[0, pl.ds(off, c), :])
        kf = k * jnp.exp(rf[c:2 * c])
        kb = k * jnp.exp(rb[c:2 * c])
        inc = None
        for h in range(2):
            kk = jnp.concatenate([jnp.where(head_masks[h], kf, 0.0), jnp.where(head_masks[h], kb, 0.0)],
                                 axis=-1).astype(BF16)
            part = _dot(vt_r[0, h * GLA_DV:(h + 1) * GLA_DV, pl.ds(off, c)], kk)
            inc = part if inc is None else inc + part
        st_ref[slot] = inc
        dec_ref[slot, 0:1, :] = jnp.exp(rf[c - 1:c, :])
        dec_ref[slot, 1:2, :] = jnp.exp(rb[0:1, :])

    @pl.when(tile == 0)
    def _():
        def ctx_body(n, carry):
            increments(kc_ref, vtc_ref, lafc_ref, labc_ref, n, n)
            return carry
        lax.fori_loop(0, n_ctx, ctx_body, 0)

        def lat_body(n, carry):
            increments(k_ref, vt_ref, laf_ref, lab_ref, n, n_ctx + n)
            return carry
        lax.fori_loop(0, n_lat, lat_body, 0)

        def fwd_body(s, st):
            inc = st_ref[s, :, 0:LANES]
            st_ref[s, :, 0:LANES] = st
            return st * dec_ref[s, 0:1, :] + inc
        lax.fori_loop(0, n_all, fwd_body, jnp.zeros((GLA_DV, LANES), F32))

        def bwd_body(j, st, base, count):
            s = base + count - 1 - j
            inc = st_ref[s, :, LANES:2 * LANES]
            st_ref[s, :, LANES:2 * LANES] = st
            return st * dec_ref[s, 1:2, :] + inc
        st = lax.fori_loop(0, n_ctx, functools.partial(bwd_body, base=0, count=n_ctx),
                           jnp.zeros((GLA_DV, LANES), F32))
        lax.fori_loop(0, n_lat, functools.partial(bwd_body, base=n_ctx, count=n_lat), st)

    row = lax.broadcasted_iota(jnp.int32, (c, c), 0)
    col = lax.broadcasted_iota(jnp.int32, (c, c), 1)
    xor = row ^ col
    row_l = lax.broadcasted_iota(jnp.int32, (c, LANES), 0)

    def out_body(jj, carry):
        n = tile * cpt + jj
        off = pl.multiple_of(n * c, c)
        loc = pl.multiple_of(jj * c, c)
        q = q_ref[0, pl.ds(loc, c), :].astype(F32)
        k = k_ref[0, pl.ds(off, c), :].astype(F32)
        rf = _cum(mf_ref[...], laf_ref[0, pl.ds(off, c), :])
        rb = _cum(mb_ref[...], lab_ref[0, pl.ds(off, c), :])
        states = st_ref[n_ctx + n].astype(BF16)
        q_inter = jnp.concatenate([q * jnp.exp(rf[0:c]), q * jnp.exp(rb[0:c])], axis=-1)
        lane2 = lax.broadcasted_iota(jnp.int32, (c, 2 * LANES), 1) % LANES
        a = [jnp.zeros((c, c), F32), jnp.zeros((c, c), F32)]
        for lvl in range(_GLA_LEVELS):
            sh = _GLA_LEVELS - 1 - lvl
            second = ((row_l >> sh) & 1) == 1
            ef = jnp.exp(rf[(2 + lvl) * c:(3 + lvl) * c])
            eb = jnp.exp(rb[(2 + lvl) * c:(3 + lvl) * c])
            ql = q * jnp.where(second, ef, eb)
            kl = (k * jnp.where(second, eb, ef)).astype(BF16)
            keep = (xor >> sh) == 1
            for h in range(2):
                al = _dot_nt(jnp.where(head_masks[h], ql, 0.0).astype(BF16), kl)
                a[h] = a[h] + jnp.where(keep, al, 0.0)
        kb16 = k.astype(BF16)
        for h in range(2):
            diag = _dot_nt(jnp.where(head_masks[h], q, 0.0).astype(BF16), kb16)
            ah = (a[h] + jnp.where(xor == 0, 2.0 * diag, 0.0)).astype(BF16)
            qi = jnp.where((lane2 < GLA_DK) if h == 0 else (lane2 >= GLA_DK), q_inter, 0.0).astype(BF16)
            o = _dot(ah, v_ref[0, pl.ds(loc, c), h * GLA_DV:(h + 1) * GLA_DV]) + _dot_nt(qi, states)
            o_ref[0, pl.ds(loc, c), h * GLA_DV:(h + 1) * GLA_DV] = o
        return carry

    lax.fori_loop(0, cpt, out_body, 0)


def _gla(gq, gk, gv, gvt, laf, lab, gk_c, gvt_c, laf_c, lab_c, mf, mb, *, tile=1024):
    nb, t, _ = gq.shape
    tc = gk_c.shape[1] // nb
    tile = min(tile, t)
    c = GLA_CHUNK
    n_all = (t + tc) // c
    pair_k = 2 * GLA_DK
    pair_v = 2 * GLA_DV
    return pl.pallas_call(
        functools.partial(_gla_kernel, cpt=tile // c),
        grid=(nb, GLA_HEADS // 2, t // tile),
        in_specs=[
            pl.BlockSpec((1, tile, pair_k), lambda b, p, i: (b, i, p)),
            pl.BlockSpec((1, t, pair_k), lambda b, p, i: (b, 0, p)),
            pl.BlockSpec((1, tile, pair_v), lambda b, p, i: (b, i, p)),
            pl.BlockSpec((1, pair_v, t), lambda b, p, i: (b, p, 0)),
            pl.BlockSpec((1, t, pair_k), lambda b, p, i: (b, 0, p)),
            pl.BlockSpec((1, t, pair_k), lambda b, p, i: (b, 0, p)),
            pl.BlockSpec((1, tc, pair_k), lambda b, p, i: (0, b, p)),
            pl.BlockSpec((1, pair_v, tc), lambda b, p, i: (0, p, b)),
            pl.BlockSpec((1, tc, pair_k), lambda b, p, i: (0, b, p)),
            pl.BlockSpec((1, tc, pair_k), lambda b, p, i: (0, b, p)),
            _const_spec(tuple(mf.shape)), _const_spec(tuple(mb.shape)),
        ],
        out_specs=pl.BlockSpec((1, tile, pair_v), lambda b, p, i: (b, i, p)),
        out_shape=jax.ShapeDtypeStruct((nb, t, GLA_V_W), F32),
        scratch_shapes=[pltpu.VMEM((n_all, GLA_DV, 2 * LANES), F32),
                        pltpu.VMEM((n_all, 8, LANES), F32)],
        compiler_params=pltpu.CompilerParams(dimension_semantics=("arbitrary", "arbitrary", "arbitrary"),
                                             vmem_limit_bytes=VMEM_LIMIT),
        name="gla",
    )(gq, gk, gv, gvt, laf, lab, gk_c, gvt_c, laf_c, lab_c, mf, mb)


def _merge_kernel(h_ref, ao_ref, go_ref, gs_ref, gt_ref, m_ref, gg_ref, wb0_ref, wb1_ref, wo_ref, o_ref):
    d = h_ref.shape[2]
    go = go_ref[0]
    normed = [_rms(go[:, h * GLA_DV:(h + 1) * GLA_DV]) * gg_ref[...] for h in range(GLA_HEADS)]
    gn = (jnp.concatenate(normed, axis=-1) * gs_ref[0].astype(F32)).astype(BF16)
    y_attn = _dot(ao_ref[0], wb0_ref[...])
    y_gla = _dot(gn, wb1_ref[...])
    gt = gt_ref[0].astype(F32)
    z = (gt[:, :d] * y_attn + gt[:, d:] * y_gla).astype(BF16)
    o_ref[0] = h_ref[0] + m_ref[0] * _dot(z, wo_ref[...])


def _merge(h, attn_o, gla_o, gs, gt, m_gate, g_gla, wb0, wb1, wo, *, tm=512):
    nb, t, d = h.shape
    tm = min(tm, t)
    tok = lambda w: pl.BlockSpec((1, tm, w), lambda b, i: (b, i, 0))
    return pl.pallas_call(
        _merge_kernel,
        grid=(nb, t // tm),
        in_specs=[tok(d), tok(ATTN_Q_W), tok(GLA_V_W), tok(GLA_V_W), tok(2 * d),
                  pl.BlockSpec((1, 1, d), lambda b, i: (b, 0, 0)),
                  _const_spec((1, GLA_DV)), _const_spec((ATTN_Q_W, d)), _const_spec((GLA_V_W, d)),
                  _const_spec((d, d))],
        out_specs=tok(d),
        out_shape=jax.ShapeDtypeStruct((nb, t, d), F32),
        compiler_params=pltpu.CompilerParams(dimension_semantics=("arbitrary", "arbitrary"),
                                             vmem_limit_bytes=VMEM_LIMIT),
        name="merge",
    )(h, attn_o, gla_o, gs, gt, m_gate, g_gla.reshape(1, GLA_DV), wb0, wb1, wo)


def _rope_tables(t):
    rows = t // GRID_W
    row = jnp.repeat(jnp.arange(rows, dtype=F32), GRID_W)
    col = jnp.tile(jnp.arange(GRID_W, dtype=F32), rows)
    freqs = ROPE_THETA ** (-jnp.arange(0, ROPE_AXIS_DIM, 2, dtype=F32) / ROPE_AXIS_DIM)
    ang = jnp.concatenate([row[:, None] * freqs, col[:, None] * freqs], axis=-1)
    cos, sin = jnp.cos(ang), jnp.sin(ang)
    reps = LANES // HEAD_DIM
    return (jnp.tile(jnp.concatenate([cos, cos], axis=-1), (1, reps)),
            jnp.tile(jnp.concatenate([-sin, sin], axis=-1), (1, reps)))


def kernel(x, c, ctx, c_ctx, w_mod, b_mod, g_norm, w_ffn_up, w_ffn_down, w_in, g_q, g_k,
           w_decay, b_decay, g_gla, w_branch, b_gate, w_out, g_final):
    nb, t, d = x.shape
    tc = ctx.shape[1]
    f = w_ffn_down.shape[2]
    assert w_mod.shape[0] == 1, "single layer"
    assert t % GLA_CHUNK == 0 and tc % GLA_CHUNK == 0 and t % GRID_W == 0

    rows = -(-(nb + 1) // 8) * 8
    c_rows = jnp.zeros((rows, d), F32).at[:nb].set(c).at[nb].set(c_ctx)
    m = _modulation(c_rows, w_mod[0], b_mod[0]).reshape(rows, N_MOD, d)
    m_lat, m_ctx = m[:nb], m[nb:nb + 1]

    wa = [w_ffn_up[0, i, :, :f].astype(BF16) for i in range(2)]
    wb = [w_ffn_up[0, i, :, f:].astype(BF16) for i in range(2)]
    wd = [w_ffn_down[0, i].astype(BF16) for i in range(2)]
    c_low = _C_GT
    w_low = w_in[0][:, c_low:c_low + 2 * GLA_RANK]
    w_r = jnp.concatenate([w_in[0][:, :c_low], w_in[0][:, c_low + 2 * GLA_RANK:], w_low,
                           jnp.zeros((d, LANES - 2 * GLA_RANK), F32)], axis=-1).astype(BF16)
    w_dec = jnp.zeros((LANES, 2 * GLA_K_W), F32)
    w_dec = w_dec.at[:GLA_RANK, :GLA_K_W].set(w_decay[0, 0]).at[GLA_RANK:2 * GLA_RANK, GLA_K_W:].set(w_decay[0, 1])
    wd_hi = w_dec.astype(BF16)
    wd_lo = (w_dec - wd_hi.astype(F32)).astype(BF16)
    b_dec = b_decay[0].reshape(1, 2 * GLA_K_W)
    gqk = jnp.concatenate([jnp.tile(g_q[0], ATTN_HEADS), jnp.tile(g_k[0], ATTN_KV_HEADS)]).reshape(1, -1)
    lane = np.arange(LANES)
    seg = jnp.asarray((lane[:, None] // HEAD_DIM == lane[None, :] // HEAD_DIM) / HEAD_DIM, dtype=BF16)
    cos_t, sin_t = _rope_tables(t)
    ones_t, zeros_t = jnp.ones((nb * tc, LANES), F32), jnp.zeros((nb * tc, LANES), F32)
    mf, mb = _gla_constants()

    proj = functools.partial(_in_proj, g=g_norm[0, 1], w_r=w_r, gqk=gqk, seg=seg, wd_hi=wd_hi, wd_lo=wd_lo,
                             b_dec=b_dec, b_gate=b_gate[0].reshape(1, 2 * d))

    hc = _half_ffn(ctx.reshape(1, nb * tc, d), m_ctx[:, 0:3], g_norm[0, 0], wa[0], wb[0], wd[0])
    pc = proj(hc, m_ctx[:, 3:5], cos_t=ones_t, sin_t=zeros_t)
    _, kt_c, v_c, _, gk_c, _, gvt_c, laf_c, lab_c, _, _ = pc

    h1 = _half_ffn(x, m_lat[:, 0:3], g_norm[0, 0], wa[0], wb[0], wd[0])
    q, kt, v, gq, gk, gv, gvt, laf, lab, gs, gt = proj(h1, m_lat[:, 3:5], cos_t=cos_t, sin_t=sin_t)
    attn_o = _attention(q, kt, kt_c, v, v_c)
    gla_o = _gla(gq, gk, gv, gvt, laf, lab, gk_c, gvt_c, laf_c, lab_c, mf, mb)
    h2 = _merge(h1, attn_o, gla_o, gs, gt, m_lat[:, 5:6], g_gla[0],
                w_branch[0, 0].astype(BF16), w_branch[0, 1].astype(BF16), w_out[0].astype(BF16))
    return _half_ffn(h2, m_lat[:, 6:9], g_norm[0, 2], wa[1], wb[1], wd[1], g_final=g_final)
```

```python
import functools

import numpy as np
import jax
import jax.numpy as jnp
from jax import lax
from jax.experimental import pallas as pl
from jax.experimental.pallas import tpu as pltpu

F32 = jnp.float32
BF16 = jnp.bfloat16

EPS = 1e-6
GRID_W = 64
N_MOD = 9
ATTN_HEADS = 8
ATTN_KV_HEADS = 2
HEAD_DIM = 64
ROPE_AXIS_DIM = HEAD_DIM // 2
ROPE_THETA = 10000.0
GLA_HEADS = 4
GLA_DK = 64
GLA_DV = 128
GLA_RANK = 16
GLA_GATE_NORM = 16.0
ATTN_Q_W = ATTN_HEADS * HEAD_DIM
ATTN_KV_W = ATTN_KV_HEADS * HEAD_DIM
GLA_K_W = GLA_HEADS * GLA_DK
GLA_V_W = GLA_HEADS * GLA_DV
Q_GROUP = ATTN_HEADS // ATTN_KV_HEADS

LANES = 128
V_ROWS = 80
GLA_CHUNK = 128
VMEM_LIMIT = 56 * 1024 * 1024


def _dot(a, b):
    return jnp.dot(a, b, preferred_element_type=F32)


def _dot_nt(a, b):
    return lax.dot_general(a, b, (((1,), (1,)), ((), ())), preferred_element_type=F32)


def _sigmoid(x):
    return 1.0 / (1.0 + jnp.exp(-x))


def _split2(x):
    hi = x.astype(BF16)
    lo = (x - hi.astype(F32)).astype(BF16)
    return hi, lo


def _rms(x):
    return x * lax.rsqrt(jnp.mean(x * x, axis=-1, keepdims=True) + EPS)


def _const_spec(shape):
    nd = len(shape)
    return pl.BlockSpec(shape, lambda *_: (0,) * nd, pipeline_mode=pl.Buffered(1))


def _mod_kernel(c_ref, w_ref, b_ref, o_ref):
    c = c_ref[...]
    s_hi, s_lo = _split2(c * _sigmoid(c))
    w_hi, w_lo = _split2(w_ref[...])
    o_ref[...] = _dot(s_hi, w_hi) + _dot(s_hi, w_lo) + _dot(s_lo, w_hi) + b_ref[...]


def _modulation(c_rows, w_mod, b_mod):
    rows, d = c_rows.shape
    n = w_mod.shape[1]
    tn = 1024
    return pl.pallas_call(
        _mod_kernel,
        grid=(n // tn,),
        in_specs=[pl.BlockSpec((rows, d), lambda j: (0, 0)),
                  pl.BlockSpec((d, tn), lambda j: (0, j)),
                  pl.BlockSpec((1, tn), lambda j: (0, j))],
        out_specs=pl.BlockSpec((rows, tn), lambda j: (0, j)),
        out_shape=jax.ShapeDtypeStruct((rows, n), F32),
        compiler_params=pltpu.CompilerParams(dimension_semantics=("arbitrary",),
                                             vmem_limit_bytes=VMEM_LIMIT),
        name="modulation",
    )(c_rows, w_mod, b_mod.reshape(1, n))


def _ffn_kernel(*refs, n_chunks, final):
    if final:
        h_ref, mod_ref, g_ref, wa_ref, wb_ref, wd_ref, gf_ref, o_ref = refs
    else:
        h_ref, mod_ref, g_ref, wa_ref, wb_ref, wd_ref, o_ref = refs
    x = h_ref[0]
    shift, scale, gate = mod_ref[0, 0:1, :], mod_ref[0, 1:2, :], mod_ref[0, 2:3, :]
    n = ((_rms(x) * g_ref[...]) * (1.0 + scale) + shift).astype(BF16)
    fc = wa_ref.shape[1] // n_chunks
    acc = None
    for c in range(n_chunks):
        a = _dot(n, wa_ref[:, c * fc:(c + 1) * fc])
        b = _dot(n, wb_ref[:, c * fc:(c + 1) * fc])
        act = ((a * _sigmoid(a)) * b).astype(BF16)
        part = _dot(act, wd_ref[c * fc:(c + 1) * fc, :])
        acc = part if acc is None else acc + part
    out = x + 0.5 * gate * acc
    if final:
        out = _rms(out) * gf_ref[...]
    o_ref[0] = out


def _half_ffn(h, mod3, g, wa, wb, wd, g_final=None, *, tm=512, n_chunks=2):
    nb, t, d = h.shape
    f = wa.shape[1]
    tm = min(tm, t)
    final = g_final is not None
    in_specs = [pl.BlockSpec((1, tm, d), lambda b, i: (b, i, 0)),
                pl.BlockSpec((1, 3, d), lambda b, i: (b, 0, 0)),
                _const_spec((1, d)), _const_spec((d, f)), _const_spec((d, f)), _const_spec((f, d))]
    args = [h, mod3, g.reshape(1, d), wa, wb, wd]
    if final:
        in_specs.append(_const_spec((1, d)))
        args.append(g_final.reshape(1, d))
    return pl.pallas_call(
        functools.partial(_ffn_kernel, n_chunks=n_chunks, final=final),
        grid=(nb, t // tm),
        in_specs=in_specs,
        out_specs=pl.BlockSpec((1, tm, d), lambda b, i: (b, i, 0)),
        out_shape=jax.ShapeDtypeStruct((nb, t, d), F32),
        compiler_params=pltpu.CompilerParams(dimension_semantics=("arbitrary", "arbitrary"),
                                             vmem_limit_bytes=VMEM_LIMIT),
        name="half_ffn_final" if final else "half_ffn",
    )(*args)


_C_Q = 0
_C_K = _C_Q + ATTN_Q_W
_C_V = _C_K + ATTN_KV_W
_C_GQ = _C_V + ATTN_KV_W
_C_GK = _C_GQ + GLA_K_W
_C_GV = _C_GK + GLA_K_W
_C_GS = _C_GV + GLA_V_W
_C_GT = _C_GS + GLA_V_W


def _proj_kernel(h_ref, mod_ref, g_ref, w_ref, gqk_ref, seg_ref, cos_ref, sin_ref, wdh_ref, wdl_ref,
                 bdec_ref, bgate_ref,
                 qt_ref, k_ref, vt_ref, gq_ref, gk_ref, gv_ref, gvt_ref, laf_ref, lab_ref, gs_ref, gt_ref):
    d = h_ref.shape[2]
    c_low = _C_GT + 2 * d
    x = h_ref[0]
    shift, scale = mod_ref[0, 0:1, :], mod_ref[0, 1:2, :]
    n = ((_rms(x) * g_ref[...]) * (1.0 + scale) + shift).astype(BF16)
    tm = x.shape[0]

    qk = _dot(n, w_ref[:, _C_Q:_C_V])
    seg = seg_ref[...]
    ms = []
    for j in range((ATTN_Q_W + ATTN_KV_W) // LANES):
        sq = qk[:, j * LANES:(j + 1) * LANES]
        hi, lo = _split2(sq * sq)
        ms.append(_dot(hi, seg) + _dot(lo, seg))
    ms = jnp.concatenate(ms, axis=-1)
    qk = qk * lax.rsqrt(ms + EPS) * gqk_ref[...]
    lane = lax.broadcasted_iota(jnp.int32, (tm, LANES), 1)
    first = (lane % HEAD_DIM) < (HEAD_DIM // 2)
    cos, sin = cos_ref[...], sin_ref[...]
    rot = []
    for j in range((ATTN_Q_W + ATTN_KV_W) // LANES):
        xs = qk[:, j * LANES:(j + 1) * LANES]
        other = jnp.where(first, pltpu.roll(xs, LANES - HEAD_DIM // 2, 1), pltpu.roll(xs, HEAD_DIM // 2, 1))
        rot.append(xs * cos + other * sin)
    q_scale = HEAD_DIM ** -0.5 * float(np.log2(np.e))
    for j in range(ATTN_Q_W // LANES):
        qt_ref[0, j * LANES:(j + 1) * LANES, :] = (rot[j] * q_scale).T.astype(BF16)
    k_rot = rot[ATTN_Q_W // LANES]
    for g in range(ATTN_KV_HEADS):
        k_ref[0, g] = k_rot[:, g * HEAD_DIM:(g + 1) * HEAD_DIM].astype(BF16)

    vt = _dot(n, w_ref[:, _C_V:_C_GQ]).T
    tail = jnp.where(lax.broadcasted_iota(jnp.int32, (V_ROWS - HEAD_DIM, tm), 0) == 0, 1.0, 0.0)
    for g in range(ATTN_KV_HEADS):
        vt_ref[0, g] = jnp.concatenate([vt[g * HEAD_DIM:(g + 1) * HEAD_DIM], tail], axis=0).astype(BF16)

    gqk = _dot(n, w_ref[:, _C_GQ:_C_GV])
    gq_ref[0] = (gqk[:, :GLA_K_W] * (GLA_DK ** -0.5)).astype(BF16)
    gk_ref[0] = gqk[:, GLA_K_W:].astype(BF16)
    gv = _dot(n, w_ref[:, _C_GV:_C_GS])
    gv_ref[0] = gv.astype(BF16)
    gvt_ref[0] = gv.T.astype(BF16)

    low = _dot(n, w_ref[:, c_low:c_low + LANES])
    l_hi, l_lo = _split2(low)
    z = _dot(l_hi, wdh_ref[...]) + _dot(l_lo, wdh_ref[...]) + _dot(l_hi, wdl_ref[...]) + bdec_ref[...]
    la = (jnp.minimum(z, 0.0) - jnp.log(1.0 + jnp.exp(-jnp.abs(z)))) * (1.0 / GLA_GATE_NORM)
    laf_ref[0] = la[:, :GLA_K_W]
    lab_ref[0] = la[:, GLA_K_W:]

    gs = _dot(n, w_ref[:, _C_GS:_C_GT])
    gs_ref[0] = (gs * _sigmoid(gs)).astype(BF16)
    gt = _dot(n, w_ref[:, _C_GT:c_low]) + bgate_ref[...]
    gt_ref[0] = _sigmoid(gt).astype(BF16)


def _in_proj(h, mod2, g, w_r, gqk, seg, cos_t, sin_t, wd_hi, wd_lo, b_dec, b_gate, *, tm=512):
    nb, t, d = h.shape
    tm = min(tm, t)
    wp = w_r.shape[1]
    tok = lambda w: pl.BlockSpec((1, tm, w), lambda b, i: (b, i, 0))
    out_shape = [
        jax.ShapeDtypeStruct((nb, ATTN_Q_W, t), BF16),
        jax.ShapeDtypeStruct((nb, ATTN_KV_HEADS, t, HEAD_DIM), BF16),
        jax.ShapeDtypeStruct((nb, ATTN_KV_HEADS, V_ROWS, t), BF16),
        jax.ShapeDtypeStruct((nb, t, GLA_K_W), BF16),
        jax.ShapeDtypeStruct((nb, t, GLA_K_W), BF16),
        jax.ShapeDtypeStruct((nb, t, GLA_V_W), BF16),
        jax.ShapeDtypeStruct((nb, GLA_V_W, t), BF16),
        jax.ShapeDtypeStruct((nb, t, GLA_K_W), F32),
        jax.ShapeDtypeStruct((nb, t, GLA_K_W), F32),
        jax.ShapeDtypeStruct((nb, t, GLA_V_W), BF16),
        jax.ShapeDtypeStruct((nb, t, 2 * d), BF16),
    ]
    out_specs = [
        pl.BlockSpec((1, ATTN_Q_W, tm), lambda b, i: (b, 0, i)),
        pl.BlockSpec((1, ATTN_KV_HEADS, tm, HEAD_DIM), lambda b, i: (b, 0, i, 0)),
        pl.BlockSpec((1, ATTN_KV_HEADS, V_ROWS, tm), lambda b, i: (b, 0, 0, i)),
        tok(GLA_K_W), tok(GLA_K_W), tok(GLA_V_W),
        pl.BlockSpec((1, GLA_V_W, tm), lambda b, i: (b, 0, i)),
        tok(GLA_K_W), tok(GLA_K_W), tok(GLA_V_W), tok(2 * d),
    ]
    in_specs = [
        pl.BlockSpec((1, tm, d), lambda b, i: (b, i, 0)),
        pl.BlockSpec((1, 2, d), lambda b, i: (b, 0, 0)),
        _const_spec((1, d)), _const_spec((d, wp)), _const_spec((1, ATTN_Q_W + ATTN_KV_W)),
        _const_spec((LANES, LANES)),
        pl.BlockSpec((tm, LANES), lambda b, i: (i, 0)),
        pl.BlockSpec((tm, LANES), lambda b, i: (i, 0)),
        _const_spec((LANES, 2 * GLA_K_W)), _const_spec((LANES, 2 * GLA_K_W)),
        _const_spec((1, 2 * GLA_K_W)), _const_spec((1, 2 * d)),
    ]
    return pl.pallas_call(
        _proj_kernel,
        grid=(nb, t // tm),
        in_specs=in_specs,
        out_specs=out_specs,
        out_shape=out_shape,
        compiler_params=pltpu.CompilerParams(dimension_semantics=("arbitrary", "arbitrary"),
                                             vmem_limit_bytes=VMEM_LIMIT),
        name="in_proj",
    )(h, mod2, g.reshape(1, d), w_r, gqk, seg, cos_t, sin_t, wd_hi, wd_lo, b_dec, b_gate)


ATTN_GROUPS = 2
ATTN_ROW_BLOCK = 256


def _attn_kernel(qt_ref, k_ref, vt_ref, o_ref, qs_ref, *group_refs, kc):
    tq = qt_ref.shape[2]
    n = k_ref.shape[2] // kc
    gw = Q_GROUP * tq // ATTN_GROUPS
    rb = min(ATTN_ROW_BLOCK, kc)
    per = len(group_refs) // ATTN_GROUPS
    m_refs, cmax_refs, acc_refs, s_refs = (
        [group_refs[g * per + j] for g in range(ATTN_GROUPS)] for j in range(per))
    for h in range(Q_GROUP):
        qs_ref[:, h * tq:(h + 1) * tq] = qt_ref[0, h * HEAD_DIM:(h + 1) * HEAD_DIM, :]
    for g in range(ATTN_GROUPS):
        m_refs[g][...] = jnp.full(m_refs[g].shape, -jnp.inf, F32)
        acc_refs[g][...] = jnp.zeros(acc_refs[g].shape, F32)

    def stage(score, apply):
        if apply is not None:
            ca, ga = apply
            m_old = m_refs[ga][...]
            m_new = jnp.maximum(m_old, cmax_refs[ga][...])
            m_refs[ga][...] = m_new
        cmax = pv_sum = None
        for j in range(kc // rb):
            blk = slice(j * rb, (j + 1) * rb)
            if score is not None:
                cs, gs = score
                rows = pl.ds(pl.multiple_of(cs * kc + j * rb, rb), rb)
                s = _dot(k_ref[0, 0, rows, :], qs_ref[:, gs * gw:(gs + 1) * gw])
                s_refs[gs][blk, :] = s
                bmax = jnp.max(s, axis=0, keepdims=True)
                cmax = bmax if cmax is None else jnp.maximum(cmax, bmax)
            if apply is not None:
                rows = pl.ds(pl.multiple_of(ca * kc + j * rb, rb), rb)
                p = jnp.exp2(s_refs[ga][blk, :] - m_new).astype(BF16)
                part = _dot(vt_ref[0, 0, :, rows], p)
                pv_sum = part if pv_sum is None else pv_sum + part
        if score is not None:
            cmax_refs[gs][...] = cmax
        if apply is not None:
            acc_refs[ga][...] = jnp.exp2(m_old - m_new) * acc_refs[ga][...] + pv_sum

    stage((0, 0), None)
    stage((0, 1), (0, 0))

    def body(c, carry):
        stage((c + 1, 0), (c, 1))
        stage((c + 1, 1), (c + 1, 0))
        return carry

    lax.fori_loop(0, n - 1, body, 0)
    stage(None, (n - 1, 1))

    for g in range(ATTN_GROUPS):
        acc = acc_refs[g][...]
        ot = acc[:HEAD_DIM] / acc[HEAD_DIM:HEAD_DIM + 1]
        pair = jnp.concatenate([ot[:, :tq], ot[:, tq:]], axis=0)
        o_ref[0, :, g * LANES:(g + 1) * LANES] = pair.T.astype(o_ref.dtype)


ATTN_KC = 768


def _attention(qt, k, vt, *, tq=256):
    nb, _, t = qt.shape
    s_len = k.shape[2]
    tq = min(tq, t)
    kc = max(c for c in range(LANES, ATTN_KC + 1, LANES) if s_len % c == 0)
    gw = Q_GROUP * tq // ATTN_GROUPS
    return pl.pallas_call(
        functools.partial(_attn_kernel, kc=kc),
        grid=(nb, ATTN_KV_HEADS, t // tq),
        in_specs=[
            pl.BlockSpec((1, Q_GROUP * HEAD_DIM, tq), lambda b, g, i: (b, g, i)),
            pl.BlockSpec((1, 1, s_len, HEAD_DIM), lambda b, g, i: (b, g, 0, 0)),
            pl.BlockSpec((1, 1, V_ROWS, s_len), lambda b, g, i: (b, g, 0, 0)),
        ],
        out_specs=pl.BlockSpec((1, tq, Q_GROUP * HEAD_DIM), lambda b, g, i: (b, i, g)),
        out_shape=jax.ShapeDtypeStruct((nb, t, ATTN_Q_W), BF16),
        scratch_shapes=[pltpu.VMEM((HEAD_DIM, Q_GROUP * tq), BF16)] + ATTN_GROUPS * [
            pltpu.VMEM((1, gw), F32),
            pltpu.VMEM((1, gw), F32),
            pltpu.VMEM((V_ROWS, gw), F32),
            pltpu.VMEM((kc, gw), F32)],
        compiler_params=pltpu.CompilerParams(dimension_semantics=("arbitrary", "arbitrary", "arbitrary"),
                                             vmem_limit_bytes=VMEM_LIMIT),
        name="flash_attention",
    )(qt, k, vt)


def _gla_constants():
    c = GLA_CHUNK
    i = np.arange(c)[:, None]
    t = np.arange(c)[None, :]
    fwd = [t <= i, t > i]
    bwd = [t >= i, t < i]
    s = c // 2
    while s >= 1:
        mid = (i // (2 * s)) * (2 * s) + s
        second = (i % (2 * s)) >= s
        fwd.append(np.where(second, (t >= mid) & (t <= i), (t > i) & (t < mid)))
        bwd.append(np.where(second, (t >= mid) & (t < i), (t >= i) & (t < mid)))
        s //= 2
    to = lambda blocks: jnp.asarray(np.concatenate(blocks, axis=0).astype(np.float32), dtype=BF16)
    return to(fwd), to(bwd)


_GLA_LEVELS = int(np.log2(GLA_CHUNK))


def _cum(mat, la):
    hi, lo = _split2(la)
    r = _dot(mat, jnp.concatenate([hi, lo], axis=-1))
    return r[:, :LANES] + r[:, LANES:]


def _gla_kernel(q_ref, k_ref, v_ref, vt_ref, laf_ref, lab_ref, kc_ref, vtc_ref, lafc_ref, labc_ref,
                mf_ref, mb_ref, o_ref, st_ref, dec_ref, *, cpt):
    c = GLA_CHUNK
    n_lat = k_ref.shape[1] // c
    n_ctx = kc_ref.shape[1] // c
    n_all = n_ctx + n_lat
    tile = pl.program_id(2)
    lane = lax.broadcasted_iota(jnp.int32, (c, LANES), 1)
    head_masks = [lane < GLA_DK, lane >= GLA_DK]

    def increments(k_r, vt_r, laf_r, lab_r, n, slot):
        off = pl.multiple_of(n * c, c)
        k = k_r[0, pl.ds(off, c), :].astype(F32)
        rf = _cum(mf_ref[0:2 * c, :], laf_r[0, pl.ds(off, c), :])
        rb = _cum(mb_ref[0:2 * c, :], lab_r[0, pl.ds(off, c), :])
        kf = k * jnp.exp(rf[c:2 * c])
        kb = k * jnp.exp(rb[c:2 * c])
        inc = None
        for h in range(2):
            kk = jnp.concatenate([jnp.where(head_masks[h], kf, 0.0), jnp.where(head_masks[h], kb, 0.0)],
                                 axis=-1).astype(BF16)
            part = _dot(vt_r[0, h * GLA_DV:(h + 1) * GLA_DV, pl.ds(off, c)], kk)
            inc = part if inc is None else inc + part
        st_ref[slot] = inc
        dec_ref[slot, 0:1, :] = jnp.exp(rf[c - 1:c, :])
        dec_ref[slot, 1:2, :] = jnp.exp(rb[0:1, :])

    @pl.when(tile == 0)
    def _():
        def ctx_body(n, carry):
            increments(kc_ref, vtc_ref, lafc_ref, labc_ref, n, n)
            return carry
        lax.fori_loop(0, n_ctx, ctx_body, 0)

        def lat_body(n, carry):
            increments(k_ref, vt_ref, laf_ref, lab_ref, n, n_ctx + n)
            return carry
        lax.fori_loop(0, n_lat, lat_body, 0)

        def fwd_body(s, st):
            inc = st_ref[s, :, 0:LANES]
            st_ref[s, :, 0:LANES] = st
            return st * dec_ref[s, 0:1, :] + inc
        lax.fori_loop(0, n_all, fwd_body, jnp.zeros((GLA_DV, LANES), F32))

        def bwd_body(j, st, base, count):
            s = base + count - 1 - j
            inc = st_ref[s, :, LANES:2 * LANES]
            st_ref[s, :, LANES:2 * LANES] = st
            return st * dec_ref[s, 1:2, :] + inc
        st = lax.fori_loop(0, n_ctx, functools.partial(bwd_body, base=0, count=n_ctx),
                           jnp.zeros((GLA_DV, LANES), F32))
        lax.fori_loop(0, n_lat, functools.partial(bwd_body, base=n_ctx, count=n_lat), st)

    row = lax.broadcasted_iota(jnp.int32, (c, c), 0)
    col = lax.broadcasted_iota(jnp.int32, (c, c), 1)
    xor = row ^ col
    row_l = lax.broadcasted_iota(jnp.int32, (c, LANES), 0)

    def out_body(jj, carry):
        n = tile * cpt + jj
        off = pl.multiple_of(n * c, c)
        loc = pl.multiple_of(jj * c, c)
        q = q_ref[0, pl.ds(loc, c), :].astype(F32)
        k = k_ref[0, pl.ds(off, c), :].astype(F32)
        rf = _cum(mf_ref[...], laf_ref[0, pl.ds(off, c), :])
        rb = _cum(mb_ref[...], lab_ref[0, pl.ds(off, c), :])
        states = st_ref[n_ctx + n].astype(BF16)
        q_inter = jnp.concatenate([q * jnp.exp(rf[0:c]), q * jnp.exp(rb[0:c])], axis=-1)
        lane2 = lax.broadcasted_iota(jnp.int32, (c, 2 * LANES), 1) % LANES
        a = [jnp.zeros((c, c), F32), jnp.zeros((c, c), F32)]
        for lvl in range(_GLA_LEVELS):
            sh = _GLA_LEVELS - 1 - lvl
            second = ((row_l >> sh) & 1) == 1
            ef = jnp.exp(rf[(2 + lvl) * c:(3 + lvl) * c])
            eb = jnp.exp(rb[(2 + lvl) * c:(3 + lvl) * c])
            ql = q * jnp.where(second, ef, eb)
            kl = (k * jnp.where(second, eb, ef)).astype(BF16)
            keep = (xor >> sh) == 1
            for h in range(2):
                al = _dot_nt(jnp.where(head_masks[h], ql, 0.0).astype(BF16), kl)
                a[h] = a[h] + jnp.where(keep, al, 0.0)
        kb16 = k.astype(BF16)
        for h in range(2):
            diag = _dot_nt(jnp.where(head_masks[h], q, 0.0).astype(BF16), kb16)
            ah = (a[h] + jnp.where(xor == 0, 2.0 * diag, 0.0)).astype(BF16)
            qi = jnp.where((lane2 < GLA_DK) if h == 0 else (lane2 >= GLA_DK), q_inter, 0.0).astype(BF16)
            o = _dot(ah, v_ref[0, pl.ds(loc, c), h * GLA_DV:(h + 1) * GLA_DV]) + _dot_nt(qi, states)
            o_ref[0, pl.ds(loc, c), h * GLA_DV:(h + 1) * GLA_DV] = o
        return carry

    lax.fori_loop(0, cpt, out_body, 0)


def _gla(gq, gk, gv, gvt, laf, lab, gk_c, gvt_c, laf_c, lab_c, mf, mb, *, tile=1024):
    nb, t, _ = gq.shape
    tc = gk_c.shape[1] // nb
    tile = min(tile, t)
    c = GLA_CHUNK
    n_all = (t + tc) // c
    pair_k = 2 * GLA_DK
    pair_v = 2 * GLA_DV
    return pl.pallas_call(
        functools.partial(_gla_kernel, cpt=tile // c),
        grid=(nb, GLA_HEADS // 2, t // tile),
        in_specs=[
            pl.BlockSpec((1, tile, pair_k), lambda b, p, i: (b, i, p)),
            pl.BlockSpec((1, t, pair_k), lambda b, p, i: (b, 0, p)),
            pl.BlockSpec((1, tile, pair_v), lambda b, p, i: (b, i, p)),
            pl.BlockSpec((1, pair_v, t), lambda b, p, i: (b, p, 0)),
            pl.BlockSpec((1, t, pair_k), lambda b, p, i: (b, 0, p)),
            pl.BlockSpec((1, t, pair_k), lambda b, p, i: (b, 0, p)),
            pl.BlockSpec((1, tc, pair_k), lambda b, p, i: (0, b, p)),
            pl.BlockSpec((1, pair_v, tc), lambda b, p, i: (0, p, b)),
            pl.BlockSpec((1, tc, pair_k), lambda b, p, i: (0, b, p)),
            pl.BlockSpec((1, tc, pair_k), lambda b, p, i: (0, b, p)),
            _const_spec(tuple(mf.shape)), _const_spec(tuple(mb.shape)),
        ],
        out_specs=pl.BlockSpec((1, tile, pair_v), lambda b, p, i: (b, i, p)),
        out_shape=jax.ShapeDtypeStruct((nb, t, GLA_V_W), F32),
        scratch_shapes=[pltpu.VMEM((n_all, GLA_DV, 2 * LANES), F32),
                        pltpu.VMEM((n_all, 8, LANES), F32)],
        compiler_params=pltpu.CompilerParams(dimension_semantics=("arbitrary", "arbitrary", "arbitrary"),
                                             vmem_limit_bytes=VMEM_LIMIT),
        name="gla",
    )(gq, gk, gv, gvt, laf, lab, gk_c, gvt_c, laf_c, lab_c, mf, mb)


def _merge_kernel(h_ref, ao_ref, go_ref, gs_ref, gt_ref, m_ref, gg_ref, wb0_ref, wb1_ref, wo_ref, o_ref):
    d = h_ref.shape[2]
    go = go_ref[0]
    normed = [_rms(go[:, h * GLA_DV:(h + 1) * GLA_DV]) * gg_ref[...] for h in range(GLA_HEADS)]
    gn = (jnp.concatenate(normed, axis=-1) * gs_ref[0].astype(F32)).astype(BF16)
    y_attn = _dot(ao_ref[0], wb0_ref[...])
    y_gla = _dot(gn, wb1_ref[...])
    gt = gt_ref[0].astype(F32)
    z = (gt[:, :d] * y_attn + gt[:, d:] * y_gla).astype(BF16)
    o_ref[0] = h_ref[0] + m_ref[0] * _dot(z, wo_ref[...])


def _merge(h, attn_o, gla_o, gs, gt, m_gate, g_gla, wb0, wb1, wo, *, tm=512):
    nb, t, d = h.shape
    tm = min(tm, t)
    tok = lambda w: pl.BlockSpec((1, tm, w), lambda b, i: (b, i, 0))
    return pl.pallas_call(
        _merge_kernel,
        grid=(nb, t // tm),
        in_specs=[tok(d), tok(ATTN_Q_W), tok(GLA_V_W), tok(GLA_V_W), tok(2 * d),
                  pl.BlockSpec((1, 1, d), lambda b, i: (b, 0, 0)),
                  _const_spec((1, GLA_DV)), _const_spec((ATTN_Q_W, d)), _const_spec((GLA_V_W, d)),
                  _const_spec((d, d))],
        out_specs=tok(d),
        out_shape=jax.ShapeDtypeStruct((nb, t, d), F32),
        compiler_params=pltpu.CompilerParams(dimension_semantics=("arbitrary", "arbitrary"),
                                             vmem_limit_bytes=VMEM_LIMIT),
        name="merge",
    )(h, attn_o, gla_o, gs, gt, m_gate, g_gla.reshape(1, GLA_DV), wb0, wb1, wo)


def _rope_tables(t):
    rows = t // GRID_W
    row = jnp.repeat(jnp.arange(rows, dtype=F32), GRID_W)
    col = jnp.tile(jnp.arange(GRID_W, dtype=F32), rows)
    freqs = ROPE_THETA ** (-jnp.arange(0, ROPE_AXIS_DIM, 2, dtype=F32) / ROPE_AXIS_DIM)
    ang = jnp.concatenate([row[:, None] * freqs, col[:, None] * freqs], axis=-1)
    cos, sin = jnp.cos(ang), jnp.sin(ang)
    reps = LANES // HEAD_DIM
    return (jnp.tile(jnp.concatenate([cos, cos], axis=-1), (1, reps)),
            jnp.tile(jnp.concatenate([-sin, sin], axis=-1), (1, reps)))


def kernel(x, c, ctx, c_ctx, w_mod, b_mod, g_norm, w_ffn_up, w_ffn_down, w_in, g_q, g_k,
           w_decay, b_decay, g_gla, w_branch, b_gate, w_out, g_final):
    nb, t, d = x.shape
    tc = ctx.shape[1]
    f = w_ffn_down.shape[2]
    assert w_mod.shape[0] == 1, "single layer"
    assert t % GLA_CHUNK == 0 and tc % GLA_CHUNK == 0 and t % GRID_W == 0

    rows = -(-(nb + 1) // 8) * 8
    c_rows = jnp.zeros((rows, d), F32).at[:nb].set(c).at[nb].set(c_ctx)
    m = _modulation(c_rows, w_mod[0], b_mod[0]).reshape(rows, N_MOD, d)
    m_lat, m_ctx = m[:nb], m[nb:nb + 1]

    wa = [w_ffn_up[0, i, :, :f].astype(BF16) for i in range(2)]
    wb = [w_ffn_up[0, i, :, f:].astype(BF16) for i in range(2)]
    wd = [w_ffn_down[0, i].astype(BF16) for i in range(2)]
    c_low = _C_GT
    w_low = w_in[0][:, c_low:c_low + 2 * GLA_RANK]
    w_r = jnp.concatenate([w_in[0][:, :c_low], w_in[0][:, c_low + 2 * GLA_RANK:], w_low,
                           jnp.zeros((d, LANES - 2 * GLA_RANK), F32)], axis=-1).astype(BF16)
    w_dec = jnp.zeros((LANES, 2 * GLA_K_W), F32)
    w_dec = w_dec.at[:GLA_RANK, :GLA_K_W].set(w_decay[0, 0]).at[GLA_RANK:2 * GLA_RANK, GLA_K_W:].set(w_decay[0, 1])
    wd_hi = w_dec.astype(BF16)
    wd_lo = (w_dec - wd_hi.astype(F32)).astype(BF16)
    b_dec = b_decay[0].reshape(1, 2 * GLA_K_W)
    gqk = jnp.concatenate([jnp.tile(g_q[0], ATTN_HEADS), jnp.tile(g_k[0], ATTN_KV_HEADS)]).reshape(1, -1)
    lane = np.arange(LANES)
    seg = jnp.asarray((lane[:, None] // HEAD_DIM == lane[None, :] // HEAD_DIM) / HEAD_DIM, dtype=BF16)
    cos_t, sin_t = _rope_tables(t)
    ones_t, zeros_t = jnp.ones((nb * tc, LANES), F32), jnp.zeros((nb * tc, LANES), F32)
    mf, mb = _gla_constants()

    proj = functools.partial(_in_proj, g=g_norm[0, 1], w_r=w_r, gqk=gqk, seg=seg, wd_hi=wd_hi, wd_lo=wd_lo,
                             b_dec=b_dec, b_gate=b_gate[0].reshape(1, 2 * d))

    hc = _half_ffn(ctx.reshape(1, nb * tc, d), m_ctx[:, 0:3], g_norm[0, 0], wa[0], wb[0], wd[0])
    pc = proj(hc, m_ctx[:, 3:5], cos_t=ones_t, sin_t=zeros_t)
    _, k_c, vt_c, _, gk_c, _, gvt_c, laf_c, lab_c, _, _ = pc

    h1 = _half_ffn(x, m_lat[:, 0:3], g_norm[0, 0], wa[0], wb[0], wd[0])
    qt, k, vt, gq, gk, gv, gvt, laf, lab, gs, gt = proj(h1, m_lat[:, 3:5], cos_t=cos_t, sin_t=sin_t)
    k_all = jnp.concatenate(
        [k, k_c.reshape(ATTN_KV_HEADS, nb, tc, HEAD_DIM).transpose(1, 0, 2, 3)], axis=2)
    vt_all = jnp.concatenate(
        [vt, vt_c.reshape(ATTN_KV_HEADS, V_ROWS, nb, tc).transpose(2, 0, 1, 3)], axis=3)
    attn_o = _attention(qt, k_all, vt_all)
    gla_o = _gla(gq, gk, gv, gvt, laf, lab, gk_c, gvt_c, laf_c, lab_c, mf, mb)
    h2 = _merge(h1, attn_o, gla_o, gs, gt, m_lat[:, 5:6], g_gla[0],
                w_branch[0, 0].astype(BF16), w_branch[0, 1].astype(BF16), w_out[0].astype(BF16))
    return _half_ffn(h2, m_lat[:, 6:9], g_norm[0, 2], wa[1], wb[1], wd[1], g_final=g_final)
```

```python
import functools

import numpy as np
import jax
import jax.numpy as jnp
from jax import lax
from jax.experimental import pallas as pl
from jax.experimental.pallas import tpu as pltpu

F32 = jnp.float32
BF16 = jnp.bfloat16

EPS = 1e-6
GRID_W = 64
N_MOD = 9
ATTN_HEADS = 8
ATTN_KV_HEADS = 2
HEAD_DIM = 64
ROPE_AXIS_DIM = HEAD_DIM // 2
ROPE_THETA = 10000.0
GLA_HEADS = 4
GLA_DK = 64
GLA_DV = 128
GLA_RANK = 16
GLA_GATE_NORM = 16.0
ATTN_Q_W = ATTN_HEADS * HEAD_DIM
ATTN_KV_W = ATTN_KV_HEADS * HEAD_DIM
GLA_K_W = GLA_HEADS * GLA_DK
GLA_V_W = GLA_HEADS * GLA_DV
Q_GROUP = ATTN_HEADS // ATTN_KV_HEADS

LANES = 128
V_ROWS = 80
GLA_CHUNK = 128
VMEM_LIMIT = 56 * 1024 * 1024


def _dot(a, b):
    return jnp.dot(a, b, preferred_element_type=F32)


def _dot_nt(a, b):
    return lax.dot_general(a, b, (((1,), (1,)), ((), ())), preferred_element_type=F32)


def _sigmoid(x):
    return 1.0 / (1.0 + jnp.exp(-x))


def _split2(x):
    hi = x.astype(BF16)
    lo = (x - hi.astype(F32)).astype(BF16)
    return hi, lo


def _rms(x):
    return x * lax.rsqrt(jnp.mean(x * x, axis=-1, keepdims=True) + EPS)


def _const_spec(shape):
    nd = len(shape)
    return pl.BlockSpec(shape, lambda *_: (0,) * nd, pipeline_mode=pl.Buffered(1))


def _mod_kernel(c_ref, w_ref, b_ref, o_ref):
    c = c_ref[...]
    s_hi, s_lo = _split2(c * _sigmoid(c))
    w_hi, w_lo = _split2(w_ref[...])
    o_ref[...] = _dot(s_hi, w_hi) + _dot(s_hi, w_lo) + _dot(s_lo, w_hi) + b_ref[...]


def _modulation(c_rows, w_mod, b_mod):
    rows, d = c_rows.shape
    n = w_mod.shape[1]
    tn = 1024
    return pl.pallas_call(
        _mod_kernel,
        grid=(n // tn,),
        in_specs=[pl.BlockSpec((rows, d), lambda j: (0, 0)),
                  pl.BlockSpec((d, tn), lambda j: (0, j)),
                  pl.BlockSpec((1, tn), lambda j: (0, j))],
        out_specs=pl.BlockSpec((rows, tn), lambda j: (0, j)),
        out_shape=jax.ShapeDtypeStruct((rows, n), F32),
        compiler_params=pltpu.CompilerParams(dimension_semantics=("arbitrary",),
                                             vmem_limit_bytes=VMEM_LIMIT),
        name="modulation",
    )(c_rows, w_mod, b_mod.reshape(1, n))


def _ffn_math(x, mod_ref, g_ref, wa_ref, wb_ref, wd_ref, n_chunks):
    shift, scale, gate = mod_ref[0, 0:1, :], mod_ref[0, 1:2, :], mod_ref[0, 2:3, :]
    n = ((_rms(x) * g_ref[...]) * (1.0 + scale) + shift).astype(BF16)
    fc = wa_ref.shape[1] // n_chunks
    acc = None
    for c in range(n_chunks):
        a = _dot(n, wa_ref[:, c * fc:(c + 1) * fc])
        b = _dot(n, wb_ref[:, c * fc:(c + 1) * fc])
        act = ((a * _sigmoid(a)) * b).astype(BF16)
        part = _dot(act, wd_ref[c * fc:(c + 1) * fc, :])
        acc = part if acc is None else acc + part
    return x + 0.5 * gate * acc


def _ffn_kernel(h_ref, mod_ref, g_ref, wa_ref, wb_ref, wd_ref, o_ref, *, n_chunks):
    o_ref[0] = _ffn_math(h_ref[0], mod_ref, g_ref, wa_ref, wb_ref, wd_ref, n_chunks)


def _merge_ffn_kernel(h_ref, ao_ref, go_ref, gs_ref, gt_ref, mg_ref, gg_ref, wb0_ref, wb1_ref, wo_ref,
                      mod_ref, g_ref, wa_ref, wb_ref, wd_ref, gf_ref, o_ref, *, n_chunks):
    d = h_ref.shape[2]
    go = go_ref[0]
    normed = [_rms(go[:, h * GLA_DV:(h + 1) * GLA_DV]) * gg_ref[...] for h in range(GLA_HEADS)]
    gn = (jnp.concatenate(normed, axis=-1) * gs_ref[0].astype(F32)).astype(BF16)
    y_attn = _dot(ao_ref[0], wb0_ref[...])
    y_gla = _dot(gn, wb1_ref[...])
    gt = gt_ref[0].astype(F32)
    z = (gt[:, :d] * y_attn + gt[:, d:] * y_gla).astype(BF16)
    h2 = h_ref[0] + mg_ref[0] * _dot(z, wo_ref[...])
    out = _ffn_math(h2, mod_ref, g_ref, wa_ref, wb_ref, wd_ref, n_chunks)
    o_ref[0] = _rms(out) * gf_ref[...]


def _ffn_weight_specs(d, f):
    return [_const_spec((1, d)), _const_spec((d, f)), _const_spec((d, f)), _const_spec((f, d))]


def _half_ffn(h, mod3, g, wa, wb, wd, *, tm=512, n_chunks=2):
    nb, t, d = h.shape
    f = wa.shape[1]
    tm = min(tm, t)
    return pl.pallas_call(
        functools.partial(_ffn_kernel, n_chunks=n_chunks),
        grid=(nb, t // tm),
        in_specs=[pl.BlockSpec((1, tm, d), lambda b, i: (b, i, 0)),
                  pl.BlockSpec((1, 3, d), lambda b, i: (b, 0, 0))] + _ffn_weight_specs(d, f),
        out_specs=pl.BlockSpec((1, tm, d), lambda b, i: (b, i, 0)),
        out_shape=jax.ShapeDtypeStruct((nb, t, d), F32),
        compiler_params=pltpu.CompilerParams(dimension_semantics=("arbitrary", "arbitrary"),
                                             vmem_limit_bytes=VMEM_LIMIT),
        name="half_ffn",
    )(h, mod3, g.reshape(1, d), wa, wb, wd)


def _merge_ffn(h, attn_o, gla_o, gs, gt, m_gate, g_gla, wb0, wb1, wo, mod3, g, wa, wb, wd, g_final,
               *, tm=512, n_chunks=2):
    nb, t, d = h.shape
    f = wa.shape[1]
    tm = min(tm, t)
    tok = lambda w: pl.BlockSpec((1, tm, w), lambda b, i: (b, i, 0))
    return pl.pallas_call(
        functools.partial(_merge_ffn_kernel, n_chunks=n_chunks),
        grid=(nb, t // tm),
        in_specs=[tok(d), tok(ATTN_Q_W), tok(GLA_V_W), tok(GLA_V_W), tok(2 * d),
                  pl.BlockSpec((1, 1, d), lambda b, i: (b, 0, 0)),
                  _const_spec((1, GLA_DV)), _const_spec((ATTN_Q_W, d)), _const_spec((GLA_V_W, d)),
                  _const_spec((d, d)),
                  pl.BlockSpec((1, 3, d), lambda b, i: (b, 0, 0))] + _ffn_weight_specs(d, f)
                 + [_const_spec((1, d))],
        out_specs=tok(d),
        out_shape=jax.ShapeDtypeStruct((nb, t, d), F32),
        compiler_params=pltpu.CompilerParams(dimension_semantics=("arbitrary", "arbitrary"),
                                             vmem_limit_bytes=VMEM_LIMIT),
        name="merge_ffn_final",
    )(h, attn_o, gla_o, gs, gt, m_gate, g_gla.reshape(1, GLA_DV), wb0, wb1, wo,
      mod3, g.reshape(1, d), wa, wb, wd, g_final.reshape(1, d))


_C_Q = 0
_C_K = _C_Q + ATTN_Q_W
_C_V = _C_K + ATTN_KV_W
_C_GQ = _C_V + ATTN_KV_W
_C_GK = _C_GQ + GLA_K_W
_C_GV = _C_GK + GLA_K_W
_C_GS = _C_GV + GLA_V_W
_C_GT = _C_GS + GLA_V_W


def _proj_kernel(h_ref, mod_ref, g_ref, w_ref, gqk_ref, seg_ref, cos_ref, sin_ref, wdh_ref, wdl_ref,
                 bdec_ref, bgate_ref,
                 qt_ref, k_ref, vt_ref, gq_ref, gk_ref, gv_ref, gvt_ref, laf_ref, lab_ref, gs_ref, gt_ref):
    d = h_ref.shape[2]
    c_low = _C_GT + 2 * d
    x = h_ref[0]
    shift, scale = mod_ref[0, 0:1, :], mod_ref[0, 1:2, :]
    n = ((_rms(x) * g_ref[...]) * (1.0 + scale) + shift).astype(BF16)
    tm = x.shape[0]

    qk = _dot(n, w_ref[:, _C_Q:_C_V])
    seg = seg_ref[...]
    ms = []
    for j in range((ATTN_Q_W + ATTN_KV_W) // LANES):
        sq = qk[:, j * LANES:(j + 1) * LANES]
        hi, lo = _split2(sq * sq)
        ms.append(_dot(hi, seg) + _dot(lo, seg))
    ms = jnp.concatenate(ms, axis=-1)
    qk = qk * lax.rsqrt(ms + EPS) * gqk_ref[...]
    lane = lax.broadcasted_iota(jnp.int32, (tm, LANES), 1)
    first = (lane % HEAD_DIM) < (HEAD_DIM // 2)
    cos, sin = cos_ref[...], sin_ref[...]
    rot = []
    for j in range((ATTN_Q_W + ATTN_KV_W) // LANES):
        xs = qk[:, j * LANES:(j + 1) * LANES]
        other = jnp.where(first, pltpu.roll(xs, LANES - HEAD_DIM // 2, 1), pltpu.roll(xs, HEAD_DIM // 2, 1))
        rot.append(xs * cos + other * sin)
    q_scale = HEAD_DIM ** -0.5 * float(np.log2(np.e))
    for j in range(ATTN_Q_W // LANES):
        qt_ref[0, j * LANES:(j + 1) * LANES, :] = (rot[j] * q_scale).T.astype(BF16)
    k_rot = rot[ATTN_Q_W // LANES]
    for g in range(ATTN_KV_HEADS):
        k_ref[0, g] = k_rot[:, g * HEAD_DIM:(g + 1) * HEAD_DIM].astype(BF16)

    vt = _dot(n, w_ref[:, _C_V:_C_GQ]).T
    tail = jnp.where(lax.broadcasted_iota(jnp.int32, (V_ROWS - HEAD_DIM, tm), 0) == 0, 1.0, 0.0)
    for g in range(ATTN_KV_HEADS):
        vt_ref[0, g] = jnp.concatenate([vt[g * HEAD_DIM:(g + 1) * HEAD_DIM], tail], axis=0).astype(BF16)

    gqk = _dot(n, w_ref[:, _C_GQ:_C_GV])
    gq_ref[0] = (gqk[:, :GLA_K_W] * (GLA_DK ** -0.5)).astype(BF16)
    gk_ref[0] = gqk[:, GLA_K_W:].astype(BF16)
    gv = _dot(n, w_ref[:, _C_GV:_C_GS])
    gv_ref[0] = gv.astype(BF16)
    gvt_ref[0] = gv.T.astype(BF16)

    low = _dot(n, w_ref[:, c_low:c_low + LANES])
    l_hi, l_lo = _split2(low)
    z = _dot(l_hi, wdh_ref[...]) + _dot(l_lo, wdh_ref[...]) + _dot(l_hi, wdl_ref[...]) + bdec_ref[...]
    la = (jnp.minimum(z, 0.0) - jnp.log(1.0 + jnp.exp(-jnp.abs(z)))) * (1.0 / GLA_GATE_NORM)
    laf_ref[0] = la[:, :GLA_K_W]
    lab_ref[0] = la[:, GLA_K_W:]

    gs = _dot(n, w_ref[:, _C_GS:_C_GT])
    gs_ref[0] = (gs * _sigmoid(gs)).astype(BF16)
    gt = _dot(n, w_ref[:, _C_GT:c_low]) + bgate_ref[...]
    gt_ref[0] = _sigmoid(gt).astype(BF16)


def _in_proj(h, mod2, g, w_r, gqk, seg, cos_t, sin_t, wd_hi, wd_lo, b_dec, b_gate, *, tm=512):
    nb, t, d = h.shape
    tm = min(tm, t)
    wp = w_r.shape[1]
    tok = lambda w: pl.BlockSpec((1, tm, w), lambda b, i: (b, i, 0))
    out_shape = [
        jax.ShapeDtypeStruct((nb, ATTN_Q_W, t), BF16),
        jax.ShapeDtypeStruct((nb, ATTN_KV_HEADS, t, HEAD_DIM), BF16),
        jax.ShapeDtypeStruct((nb, ATTN_KV_HEADS, V_ROWS, t), BF16),
        jax.ShapeDtypeStruct((nb, t, GLA_K_W), BF16),
        jax.ShapeDtypeStruct((nb, t, GLA_K_W), BF16),
        jax.ShapeDtypeStruct((nb, t, GLA_V_W), BF16),
        jax.ShapeDtypeStruct((nb, GLA_V_W, t), BF16),
        jax.ShapeDtypeStruct((nb, t, GLA_K_W), F32),
        jax.ShapeDtypeStruct((nb, t, GLA_K_W), F32),
        jax.ShapeDtypeStruct((nb, t, GLA_V_W), BF16),
        jax.ShapeDtypeStruct((nb, t, 2 * d), BF16),
    ]
    out_specs = [
        pl.BlockSpec((1, ATTN_Q_W, tm), lambda b, i: (b, 0, i)),
        pl.BlockSpec((1, ATTN_KV_HEADS, tm, HEAD_DIM), lambda b, i: (b, 0, i, 0)),
        pl.BlockSpec((1, ATTN_KV_HEADS, V_ROWS, tm), lambda b, i: (b, 0, 0, i)),
        tok(GLA_K_W), tok(GLA_K_W), tok(GLA_V_W),
        pl.BlockSpec((1, GLA_V_W, tm), lambda b, i: (b, 0, i)),
        tok(GLA_K_W), tok(GLA_K_W), tok(GLA_V_W), tok(2 * d),
    ]
    in_specs = [
        pl.BlockSpec((1, tm, d), lambda b, i: (b, i, 0)),
        pl.BlockSpec((1, 2, d), lambda b, i: (b, 0, 0)),
        _const_spec((1, d)), _const_spec((d, wp)), _const_spec((1, ATTN_Q_W + ATTN_KV_W)),
        _const_spec((LANES, LANES)),
        pl.BlockSpec((tm, LANES), lambda b, i: (i, 0)),
        pl.BlockSpec((tm, LANES), lambda b, i: (i, 0)),
        _const_spec((LANES, 2 * GLA_K_W)), _const_spec((LANES, 2 * GLA_K_W)),
        _const_spec((1, 2 * GLA_K_W)), _const_spec((1, 2 * d)),
    ]
    return pl.pallas_call(
        _proj_kernel,
        grid=(nb, t // tm),
        in_specs=in_specs,
        out_specs=out_specs,
        out_shape=out_shape,
        compiler_params=pltpu.CompilerParams(dimension_semantics=("arbitrary", "arbitrary"),
                                             vmem_limit_bytes=VMEM_LIMIT),
        name="in_proj",
    )(h, mod2, g.reshape(1, d), w_r, gqk, seg, cos_t, sin_t, wd_hi, wd_lo, b_dec, b_gate)


ATTN_GROUPS = 2
ATTN_ROW_BLOCK = 256


def _attn_kernel(qt_ref, k_ref, vt_ref, o_ref, qs_ref, *group_refs, kc):
    tq = qt_ref.shape[2]
    n = k_ref.shape[2] // kc
    gw = Q_GROUP * tq // ATTN_GROUPS
    rb = min(ATTN_ROW_BLOCK, kc)
    per = len(group_refs) // ATTN_GROUPS
    m_refs, cmax_refs, acc_refs, s_refs = (
        [group_refs[g * per + j] for g in range(ATTN_GROUPS)] for j in range(per))
    for h in range(Q_GROUP):
        qs_ref[:, h * tq:(h + 1) * tq] = qt_ref[0, h * HEAD_DIM:(h + 1) * HEAD_DIM, :]
    for g in range(ATTN_GROUPS):
        m_refs[g][...] = jnp.full(m_refs[g].shape, -jnp.inf, F32)
        acc_refs[g][...] = jnp.zeros(acc_refs[g].shape, F32)

    def stage(score, apply):
        if apply is not None:
            ca, ga = apply
            m_old = m_refs[ga][...]
            m_new = jnp.maximum(m_old, cmax_refs[ga][...])
            m_refs[ga][...] = m_new
        cmax = pv_sum = None
        for j in range(kc // rb):
            blk = slice(j * rb, (j + 1) * rb)
            if score is not None:
                cs, gs = score
                rows = pl.ds(pl.multiple_of(cs * kc + j * rb, rb), rb)
                s = _dot(k_ref[0, 0, rows, :], qs_ref[:, gs * gw:(gs + 1) * gw])
                s_refs[gs][blk, :] = s
                bmax = jnp.max(s, axis=0, keepdims=True)
                cmax = bmax if cmax is None else jnp.maximum(cmax, bmax)
            if apply is not None:
                rows = pl.ds(pl.multiple_of(ca * kc + j * rb, rb), rb)
                p = jnp.exp2(s_refs[ga][blk, :] - m_new).astype(BF16)
                part = _dot(vt_ref[0, 0, :, rows], p)
                pv_sum = part if pv_sum is None else pv_sum + part
        if score is not None:
            cmax_refs[gs][...] = cmax
        if apply is not None:
            acc_refs[ga][...] = jnp.exp2(m_old - m_new) * acc_refs[ga][...] + pv_sum

    stage((0, 0), None)
    stage((0, 1), (0, 0))

    def body(c, carry):
        stage((c + 1, 0), (c, 1))
        stage((c + 1, 1), (c + 1, 0))
        return carry

    lax.fori_loop(0, n - 1, body, 0)
    stage(None, (n - 1, 1))

    for g in range(ATTN_GROUPS):
        acc = acc_refs[g][...]
        ot = acc[:HEAD_DIM] / acc[HEAD_DIM:HEAD_DIM + 1]
        pair = jnp.concatenate([ot[:, :tq], ot[:, tq:]], axis=0)
        o_ref[0, :, g * LANES:(g + 1) * LANES] = pair.T.astype(o_ref.dtype)


ATTN_KC = 768


def _attention(qt, k, vt, *, tq=512):
    nb, _, t = qt.shape
    s_len = k.shape[2]
    tq = min(tq, t)
    kc = max(c for c in range(LANES, ATTN_KC + 1, LANES) if s_len % c == 0)
    gw = Q_GROUP * tq // ATTN_GROUPS
    return pl.pallas_call(
        functools.partial(_attn_kernel, kc=kc),
        grid=(nb, ATTN_KV_HEADS, t // tq),
        in_specs=[
            pl.BlockSpec((1, Q_GROUP * HEAD_DIM, tq), lambda b, g, i: (b, g, i)),
            pl.BlockSpec((1, 1, s_len, HEAD_DIM), lambda b, g, i: (b, g, 0, 0)),
            pl.BlockSpec((1, 1, V_ROWS, s_len), lambda b, g, i: (b, g, 0, 0)),
        ],
        out_specs=pl.BlockSpec((1, tq, Q_GROUP * HEAD_DIM), lambda b, g, i: (b, i, g)),
        out_shape=jax.ShapeDtypeStruct((nb, t, ATTN_Q_W), BF16),
        scratch_shapes=[pltpu.VMEM((HEAD_DIM, Q_GROUP * tq), BF16)] + ATTN_GROUPS * [
            pltpu.VMEM((1, gw), F32),
            pltpu.VMEM((1, gw), F32),
            pltpu.VMEM((V_ROWS, gw), F32),
            pltpu.VMEM((kc, gw), F32)],
        compiler_params=pltpu.CompilerParams(dimension_semantics=("arbitrary", "arbitrary", "arbitrary"),
                                             vmem_limit_bytes=VMEM_LIMIT),
        name="flash_attention",
    )(qt, k, vt)


def _gla_constants():
    c = GLA_CHUNK
    i = np.arange(c)[:, None]
    t = np.arange(c)[None, :]
    fwd = [t > i, t <= i]
    bwd = [t < i, t >= i]
    s = c // 2
    while s >= 1:
        mid = (i // (2 * s)) * (2 * s) + s
        second = (i % (2 * s)) >= s
        fwd.append(np.where(second, (t >= mid) & (t <= i), (t > i) & (t < mid)))
        bwd.append(np.where(second, (t >= mid) & (t < i), (t >= i) & (t < mid)))
        s //= 2
    to = lambda blocks: jnp.asarray(np.concatenate(blocks, axis=0).astype(np.float32), dtype=BF16)
    return to(fwd), to(bwd)


_GLA_LEVELS = int(np.log2(GLA_CHUNK))


def _cum(mat, la):
    hi, lo = _split2(la)
    r = _dot(mat, jnp.concatenate([hi, lo], axis=-1))
    return r[:, :LANES] + r[:, LANES:]


def _gla_kernel(q_ref, k_ref, v_ref, vt_ref, laf_ref, lab_ref, kc_ref, vtc_ref, lafc_ref, labc_ref,
                mf_ref, mb_ref, o_ref, st_ref, dec_ref, *, cpt):
    c = GLA_CHUNK
    n_lat = k_ref.shape[1] // c
    n_ctx = kc_ref.shape[1] // c
    n_all = n_ctx + n_lat
    tile = pl.program_id(2)
    lane = lax.broadcasted_iota(jnp.int32, (c, LANES), 1)
    head_masks = [lane < GLA_DK, lane >= GLA_DK]
    lane2 = lax.broadcasted_iota(jnp.int32, (c, 2 * LANES), 1) % LANES
    pair_masks = [lane2 < GLA_DK, lane2 >= GLA_DK]

    def increments(k_r, vt_r, laf_r, lab_r, n, slot):
        off = pl.multiple_of(n * c, c)
        k = k_r[0, pl.ds(off, c), :].astype(F32)
        rf = _cum(mf_ref[0:2 * c, :], laf_r[0, pl.ds(off, c), :])
        rb = _cum(mb_ref[0:2 * c, :], lab_r[0, pl.ds(off, c), :])
        kfb = jnp.concatenate([k * jnp.exp(rf[0:c]),
                               k * jnp.exp(rb[0:c])], axis=-1)
        kk = jnp.concatenate([jnp.where(pair_masks[0], kfb, 0.0), jnp.where(pair_masks[1], kfb, 0.0)],
                             axis=0).astype(BF16)
        vt2 = jnp.concatenate([vt_r[0, 0:GLA_DV, pl.ds(off, c)], vt_r[0, GLA_DV:2 * GLA_DV, pl.ds(off, c)]],
                              axis=-1)
        st_ref[slot] = _dot(vt2, kk)
        dec_ref[slot, 0:1, :] = jnp.exp(rf[2 * c - 1:2 * c, :])
        dec_ref[slot, 1:2, :] = jnp.exp(rb[c:c + 1, :])

    @pl.when(tile == 0)
    def _():
        def ctx_body(n, carry):
            increments(kc_ref, vtc_ref, lafc_ref, labc_ref, n, n)
            return carry
        lax.fori_loop(0, n_ctx, ctx_body, 0)

        def lat_body(n, carry):
            increments(k_ref, vt_ref, laf_ref, lab_ref, n, n_ctx + n)
            return carry
        lax.fori_loop(0, n_lat, lat_body, 0, unroll=2)

        def fwd_body(s, st):
            inc = st_ref[s, :, 0:LANES]
            st_ref[s, :, 0:LANES] = st
            return st * dec_ref[s, 0:1, :] + inc
        lax.fori_loop(0, n_all, fwd_body, jnp.zeros((GLA_DV, LANES), F32))

        def bwd_body(j, st, base, count):
            s = base + count - 1 - j
            inc = st_ref[s, :, LANES:2 * LANES]
            st_ref[s, :, LANES:2 * LANES] = st
            return st * dec_ref[s, 1:2, :] + inc
        st = lax.fori_loop(0, n_ctx, functools.partial(bwd_body, base=0, count=n_ctx),
                           jnp.zeros((GLA_DV, LANES), F32))
        lax.fori_loop(0, n_lat, functools.partial(bwd_body, base=n_ctx, count=n_lat), st)

    xor2 = lax.broadcasted_iota(jnp.int32, (c, 2 * c), 0) ^ (lax.broadcasted_iota(jnp.int32, (c, 2 * c), 1) & (c - 1))
    row_l = lax.broadcasted_iota(jnp.int32, (c, LANES), 0)
    lane2v = lax.broadcasted_iota(jnp.int32, (c, 2 * GLA_DV), 1)
    lane2s = lax.broadcasted_iota(jnp.int32, (GLA_DV, 2 * LANES), 1) % LANES

    def out_body(jj, carry):
        n = tile * cpt + jj
        off = pl.multiple_of(n * c, c)
        loc = pl.multiple_of(jj * c, c)
        q = q_ref[0, pl.ds(loc, c), :].astype(F32)
        k = k_ref[0, pl.ds(off, c), :].astype(F32)
        rf = _cum(mf_ref[c:, :], laf_ref[0, pl.ds(off, c), :])
        rb = _cum(mb_ref[c:, :], lab_ref[0, pl.ds(off, c), :])

        def stack_heads(x):
            return jnp.concatenate([jnp.where(head_masks[0], x, 0.0), jnp.where(head_masks[1], x, 0.0)], axis=0)

        a = 2.0 * _dot_nt(q.astype(BF16), stack_heads(k).astype(BF16))
        for lvl in range(_GLA_LEVELS):
            sh = _GLA_LEVELS - 1 - lvl
            second = ((row_l >> sh) & 1) == 1
            ef = jnp.exp(rf[(1 + lvl) * c:(2 + lvl) * c])
            eb = jnp.exp(rb[(1 + lvl) * c:(2 + lvl) * c])
            ql = (q * jnp.where(second, ef, eb)).astype(BF16)
            kl = stack_heads(k * jnp.where(second, eb, ef)).astype(BF16)
            a = jnp.where((xor2 >> sh) == 1, _dot_nt(ql, kl), a)
        a = a.astype(BF16)
        v = v_ref[0, pl.ds(loc, c), :]
        v_bd = jnp.concatenate([jnp.where(lane2v < GLA_DV, v, jnp.zeros_like(v)),
                                jnp.where(lane2v >= GLA_DV, v, jnp.zeros_like(v))], axis=0)
        q_inter = jnp.concatenate([q * jnp.exp(rf[0:c]), q * jnp.exp(rb[0:c])], axis=-1).astype(BF16)
        states = st_ref[n_ctx + n]
        st2 = jnp.concatenate([jnp.where(lane2s < GLA_DK, states, 0.0), jnp.where(lane2s >= GLA_DK, states, 0.0)],
                              axis=0).astype(BF16)
        o_ref[0, pl.ds(loc, c), :] = _dot(a, v_bd) + _dot_nt(q_inter, st2)
        return carry

    lax.fori_loop(0, cpt, out_body, 0)


def _gla(gq, gk, gv, gvt, laf, lab, gk_c, gvt_c, laf_c, lab_c, mf, mb, *, tile=1024):
    nb, t, _ = gq.shape
    tc = gk_c.shape[1] // nb
    tile = min(tile, t)
    c = GLA_CHUNK
    n_all = (t + tc) // c
    pair_k = 2 * GLA_DK
    pair_v = 2 * GLA_DV
    return pl.pallas_call(
        functools.partial(_gla_kernel, cpt=tile // c),
        grid=(nb, GLA_HEADS // 2, t // tile),
        in_specs=[
            pl.BlockSpec((1, tile, pair_k), lambda b, p, i: (b, i, p)),
            pl.BlockSpec((1, t, pair_k), lambda b, p, i: (b, 0, p)),
            pl.BlockSpec((1, tile, pair_v), lambda b, p, i: (b, i, p)),
            pl.BlockSpec((1, pair_v, t), lambda b, p, i: (b, p, 0)),
            pl.BlockSpec((1, t, pair_k), lambda b, p, i: (b, 0, p)),
            pl.BlockSpec((1, t, pair_k), lambda b, p, i: (b, 0, p)),
            pl.BlockSpec((1, tc, pair_k), lambda b, p, i: (0, b, p)),
            pl.BlockSpec((1, pair_v, tc), lambda b, p, i: (0, p, b)),
            pl.BlockSpec((1, tc, pair_k), lambda b, p, i: (0, b, p)),
            pl.BlockSpec((1, tc, pair_k), lambda b, p, i: (0, b, p)),
            _const_spec(tuple(mf.shape)), _const_spec(tuple(mb.shape)),
        ],
        out_specs=pl.BlockSpec((1, tile, pair_v), lambda b, p, i: (b, i, p)),
        out_shape=jax.ShapeDtypeStruct((nb, t, GLA_V_W), F32),
        scratch_shapes=[pltpu.VMEM((n_all, GLA_DV, 2 * LANES), F32),
                        pltpu.VMEM((n_all, 8, LANES), F32)],
        compiler_params=pltpu.CompilerParams(dimension_semantics=("arbitrary", "arbitrary", "arbitrary"),
                                             vmem_limit_bytes=VMEM_LIMIT),
        name="gla",
    )(gq, gk, gv, gvt, laf, lab, gk_c, gvt_c, laf_c, lab_c, mf, mb)


def _rope_tables(t):
    rows = t // GRID_W
    row = jnp.repeat(jnp.arange(rows, dtype=F32), GRID_W)
    col = jnp.tile(jnp.arange(GRID_W, dtype=F32), rows)
    freqs = ROPE_THETA ** (-jnp.arange(0, ROPE_AXIS_DIM, 2, dtype=F32) / ROPE_AXIS_DIM)
    ang = jnp.concatenate([row[:, None] * freqs, col[:, None] * freqs], axis=-1)
    cos, sin = jnp.cos(ang), jnp.sin(ang)
    reps = LANES // HEAD_DIM
    return (jnp.tile(jnp.concatenate([cos, cos], axis=-1), (1, reps)),
            jnp.tile(jnp.concatenate([-sin, sin], axis=-1), (1, reps)))


def kernel(x, c, ctx, c_ctx, w_mod, b_mod, g_norm, w_ffn_up, w_ffn_down, w_in, g_q, g_k,
           w_decay, b_decay, g_gla, w_branch, b_gate, w_out, g_final):
    nb, t, d = x.shape
    tc = ctx.shape[1]
    f = w_ffn_down.shape[2]
    assert w_mod.shape[0] == 1, "single layer"
    assert t % GLA_CHUNK == 0 and tc % GLA_CHUNK == 0 and t % GRID_W == 0

    rows = -(-(nb + 1) // 8) * 8
    c_rows = jnp.zeros((rows, d), F32).at[:nb].set(c).at[nb].set(c_ctx)
    m = _modulation(c_rows, w_mod[0], b_mod[0]).reshape(rows, N_MOD, d)
    m_lat, m_ctx = m[:nb], m[nb:nb + 1]

    wa = [w_ffn_up[0, i, :, :f].astype(BF16) for i in range(2)]
    wb = [w_ffn_up[0, i, :, f:].astype(BF16) for i in range(2)]
    wd = [w_ffn_down[0, i].astype(BF16) for i in range(2)]
    c_low = _C_GT
    w_low = w_in[0][:, c_low:c_low + 2 * GLA_RANK]
    w_r = jnp.concatenate([w_in[0][:, :c_low], w_in[0][:, c_low + 2 * GLA_RANK:], w_low,
                           jnp.zeros((d, LANES - 2 * GLA_RANK), F32)], axis=-1).astype(BF16)
    w_dec = jnp.zeros((LANES, 2 * GLA_K_W), F32)
    w_dec = w_dec.at[:GLA_RANK, :GLA_K_W].set(w_decay[0, 0]).at[GLA_RANK:2 * GLA_RANK, GLA_K_W:].set(w_decay[0, 1])
    wd_hi = w_dec.astype(BF16)
    wd_lo = (w_dec - wd_hi.astype(F32)).astype(BF16)
    b_dec = b_decay[0].reshape(1, 2 * GLA_K_W)
    gqk = jnp.concatenate([jnp.tile(g_q[0], ATTN_HEADS), jnp.tile(g_k[0], ATTN_KV_HEADS)]).reshape(1, -1)
    lane = np.arange(LANES)
    seg = jnp.asarray((lane[:, None] // HEAD_DIM == lane[None, :] // HEAD_DIM) / HEAD_DIM, dtype=BF16)
    cos_t, sin_t = _rope_tables(t)
    ones_t, zeros_t = jnp.ones((nb * tc, LANES), F32), jnp.zeros((nb * tc, LANES), F32)
    mf, mb = _gla_constants()

    proj = functools.partial(_in_proj, g=g_norm[0, 1], w_r=w_r, gqk=gqk, seg=seg, wd_hi=wd_hi, wd_lo=wd_lo,
                             b_dec=b_dec, b_gate=b_gate[0].reshape(1, 2 * d))

    hc = _half_ffn(ctx.reshape(1, nb * tc, d), m_ctx[:, 0:3], g_norm[0, 0], wa[0], wb[0], wd[0])
    pc = proj(hc, m_ctx[:, 3:5], cos_t=ones_t, sin_t=zeros_t)
    _, k_c, vt_c, _, gk_c, _, gvt_c, laf_c, lab_c, _, _ = pc

    h1 = _half_ffn(x, m_lat[:, 0:3], g_norm[0, 0], wa[0], wb[0], wd[0])
    qt, k, vt, gq, gk, gv, gvt, laf, lab, gs, gt = proj(h1, m_lat[:, 3:5], cos_t=cos_t, sin_t=sin_t)
    k_all = jnp.concatenate(
        [k, k_c.reshape(ATTN_KV_HEADS, nb, tc, HEAD_DIM).transpose(1, 0, 2, 3)], axis=2)
    vt_all = jnp.concatenate(
        [vt, vt_c.reshape(ATTN_KV_HEADS, V_ROWS, nb, tc).transpose(2, 0, 1, 3)], axis=3)
    attn_o = _attention(qt, k_all, vt_all)
    gla_o = _gla(gq, gk, gv, gvt, laf, lab, gk_c, gvt_c, laf_c, lab_c, mf, mb)
    return _merge_ffn(h1, attn_o, gla_o, gs, gt, m_lat[:, 5:6], g_gla[0],
                      w_branch[0, 0].astype(BF16), w_branch[0, 1].astype(BF16), w_out[0].astype(BF16),
                      m_lat[:, 6:9], g_norm[0, 2], wa[1], wb[1], wd[1], g_final)
```

```python
import functools

import numpy as np
import jax
import jax.numpy as jnp
from jax import lax
from jax.experimental import pallas as pl
from jax.experimental.pallas import tpu as pltpu

F32 = jnp.float32
BF16 = jnp.bfloat16

EPS = 1e-6
GRID_W = 64
N_MOD = 9
ATTN_HEADS = 8
ATTN_KV_HEADS = 2
HEAD_DIM = 64
ROPE_AXIS_DIM = HEAD_DIM // 2
ROPE_THETA = 10000.0
GLA_HEADS = 4
GLA_DK = 64
GLA_DV = 128
GLA_RANK = 16
GLA_GATE_NORM = 16.0
ATTN_Q_W = ATTN_HEADS * HEAD_DIM
ATTN_KV_W = ATTN_KV_HEADS * HEAD_DIM
GLA_K_W = GLA_HEADS * GLA_DK
GLA_V_W = GLA_HEADS * GLA_DV
Q_GROUP = ATTN_HEADS // ATTN_KV_HEADS

LANES = 128
V_ROWS = 80
GLA_CHUNK = 128
VMEM_LIMIT = 56 * 1024 * 1024


def _dot(a, b):
    return jnp.dot(a, b, preferred_element_type=F32)


def _dot_nt(a, b):
    return lax.dot_general(a, b, (((1,), (1,)), ((), ())), preferred_element_type=F32)


def _sigmoid(x):
    return 1.0 / (1.0 + jnp.exp(-x))


def _split2(x):
    hi = x.astype(BF16)
    lo = (x - hi.astype(F32)).astype(BF16)
    return hi, lo


def _rms(x):
    return x * lax.rsqrt(jnp.mean(x * x, axis=-1, keepdims=True) + EPS)


def _const_spec(shape):
    nd = len(shape)
    return pl.BlockSpec(shape, lambda *_: (0,) * nd, pipeline_mode=pl.Buffered(1))


def _mod_kernel(c_ref, w_ref, b_ref, o_ref):
    c = c_ref[...]
    s_hi, s_lo = _split2(c * _sigmoid(c))
    w_hi, w_lo = _split2(w_ref[...])
    o_ref[...] = _dot(s_hi, w_hi) + _dot(s_hi, w_lo) + _dot(s_lo, w_hi) + b_ref[...]


def _modulation(c_rows, w_mod, b_mod):
    rows, d = c_rows.shape
    n = w_mod.shape[1]
    tn = 1024
    return pl.pallas_call(
        _mod_kernel,
        grid=(n // tn,),
        in_specs=[pl.BlockSpec((rows, d), lambda j: (0, 0)),
                  pl.BlockSpec((d, tn), lambda j: (0, j)),
                  pl.BlockSpec((1, tn), lambda j: (0, j))],
        out_specs=pl.BlockSpec((rows, tn), lambda j: (0, j)),
        out_shape=jax.ShapeDtypeStruct((rows, n), F32),
        compiler_params=pltpu.CompilerParams(dimension_semantics=("arbitrary",),
                                             vmem_limit_bytes=VMEM_LIMIT),
        name="modulation",
    )(c_rows, w_mod, b_mod.reshape(1, n))


def _ffn_math(x, mod_ref, g_ref, wa_ref, wb_ref, wd_ref, n_chunks):
    shift, scale, gate = mod_ref[0, 0:1, :], mod_ref[0, 1:2, :], mod_ref[0, 2:3, :]
    n = ((_rms(x) * g_ref[...]) * (1.0 + scale) + shift).astype(BF16)
    fc = wa_ref.shape[1] // n_chunks
    acc = None
    for c in range(n_chunks):
        a = _dot(n, wa_ref[:, c * fc:(c + 1) * fc])
        b = _dot(n, wb_ref[:, c * fc:(c + 1) * fc])
        act = ((a * _sigmoid(a)) * b).astype(BF16)
        part = _dot(act, wd_ref[c * fc:(c + 1) * fc, :])
        acc = part if acc is None else acc + part
    return x + 0.5 * gate * acc


def _ffn_kernel(h_ref, mod_ref, g_ref, wa_ref, wb_ref, wd_ref, o_ref, *, n_chunks):
    o_ref[0] = _ffn_math(h_ref[0], mod_ref, g_ref, wa_ref, wb_ref, wd_ref, n_chunks)


def _merge_ffn_kernel(h_ref, ao_ref, go_ref, gs_ref, gt_ref, mg_ref, gg_ref, wb0_ref, wb1_ref, wo_ref,
                      mod_ref, g_ref, wa_ref, wb_ref, wd_ref, gf_ref, o_ref, *, n_chunks):
    d = h_ref.shape[2]
    go = go_ref[0]
    normed = [_rms(go[:, h * GLA_DV:(h + 1) * GLA_DV]) * gg_ref[...] for h in range(GLA_HEADS)]
    gn = (jnp.concatenate(normed, axis=-1) * gs_ref[0].astype(F32)).astype(BF16)
    y_attn = _dot(ao_ref[0], wb0_ref[...])
    y_gla = _dot(gn, wb1_ref[...])
    gt = gt_ref[0].astype(F32)
    z = (gt[:, :d] * y_attn + gt[:, d:] * y_gla).astype(BF16)
    h2 = h_ref[0] + mg_ref[0] * _dot(z, wo_ref[...])
    out = _ffn_math(h2, mod_ref, g_ref, wa_ref, wb_ref, wd_ref, n_chunks)
    o_ref[0] = _rms(out) * gf_ref[...]


def _ffn_weight_specs(d, f):
    return [_const_spec((1, d)), _const_spec((d, f)), _const_spec((d, f)), _const_spec((f, d))]


def _half_ffn(h, mod3, g, wa, wb, wd, *, tm=512, n_chunks=1):
    nb, t, d = h.shape
    f = wa.shape[1]
    tm = min(tm, t)
    return pl.pallas_call(
        functools.partial(_ffn_kernel, n_chunks=n_chunks),
        grid=(nb, t // tm),
        in_specs=[pl.BlockSpec((1, tm, d), lambda b, i: (b, i, 0)),
                  pl.BlockSpec((1, 3, d), lambda b, i: (b, 0, 0))] + _ffn_weight_specs(d, f),
        out_specs=pl.BlockSpec((1, tm, d), lambda b, i: (b, i, 0)),
        out_shape=jax.ShapeDtypeStruct((nb, t, d), F32),
        compiler_params=pltpu.CompilerParams(dimension_semantics=("arbitrary", "arbitrary"),
                                             vmem_limit_bytes=VMEM_LIMIT),
        name="half_ffn",
    )(h, mod3, g.reshape(1, d), wa, wb, wd)


def _merge_ffn(h, attn_o, gla_o, gs, gt, m_gate, g_gla, wb0, wb1, wo, mod3, g, wa, wb, wd, g_final,
               *, tm=512, n_chunks=1):
    nb, t, d = h.shape
    f = wa.shape[1]
    tm = min(tm, t)
    tok = lambda w: pl.BlockSpec((1, tm, w), lambda b, i: (b, i, 0))
    return pl.pallas_call(
        functools.partial(_merge_ffn_kernel, n_chunks=n_chunks),
        grid=(nb, t // tm),
        in_specs=[tok(d), tok(ATTN_Q_W), tok(GLA_V_W), tok(GLA_V_W), tok(2 * d),
                  pl.BlockSpec((1, 1, d), lambda b, i: (b, 0, 0)),
                  _const_spec((1, GLA_DV)), _const_spec((ATTN_Q_W, d)), _const_spec((GLA_V_W, d)),
                  _const_spec((d, d)),
                  pl.BlockSpec((1, 3, d), lambda b, i: (b, 0, 0))] + _ffn_weight_specs(d, f)
                 + [_const_spec((1, d))],
        out_specs=tok(d),
        out_shape=jax.ShapeDtypeStruct((nb, t, d), F32),
        compiler_params=pltpu.CompilerParams(dimension_semantics=("arbitrary", "arbitrary"),
                                             vmem_limit_bytes=VMEM_LIMIT),
        name="merge_ffn_final",
    )(h, attn_o, gla_o, gs, gt, m_gate, g_gla.reshape(1, GLA_DV), wb0, wb1, wo,
      mod3, g.reshape(1, d), wa, wb, wd, g_final.reshape(1, d))


_C_Q = 0
_C_K = _C_Q + ATTN_Q_W
_C_V = _C_K + ATTN_KV_W
_C_GQ = _C_V + ATTN_KV_W
_C_GK = _C_GQ + GLA_K_W
_C_GV = _C_GK + GLA_K_W
_C_GS = _C_GV + GLA_V_W
_C_GT = _C_GS + GLA_V_W


def _proj_kernel(h_ref, mod_ref, g_ref, w_ref, gqk_ref, seg_ref, cos_ref, sin_ref, wdec_ref,
                 bdec_ref, bgate_ref,
                 qt_ref, k_ref, vt_ref, gq_ref, gk_ref, gv_ref, gvt_ref, laf_ref, lab_ref, gs_ref, gt_ref):
    d = h_ref.shape[2]
    c_low = _C_GT + 2 * d
    x = h_ref[0]
    shift, scale = mod_ref[0, 0:1, :], mod_ref[0, 1:2, :]
    n = ((_rms(x) * g_ref[...]) * (1.0 + scale) + shift).astype(BF16)
    tm = x.shape[0]

    qk = _dot(n, w_ref[:, _C_Q:_C_V])
    low = _dot(n, w_ref[:, c_low:c_low + LANES])
    gv = _dot(n, w_ref[:, _C_GV:_C_GS])

    seg = seg_ref[...]
    ms = []
    for j in range((ATTN_Q_W + ATTN_KV_W) // LANES):
        sq = qk[:, j * LANES:(j + 1) * LANES]
        hi, lo = _split2(sq * sq)
        ms.append(_dot(jnp.concatenate([hi, lo], axis=-1), seg))
    ms = jnp.concatenate(ms, axis=-1)
    gqk = _dot(n, w_ref[:, _C_GQ:_C_GV])

    l_hi = low.astype(BF16).astype(F32)
    l_lo = low - l_hi
    packed = l_hi + pltpu.roll(l_lo, 2 * GLA_RANK, 1) + pltpu.roll(l_hi, 4 * GLA_RANK, 1)
    z = _dot(packed.astype(BF16), wdec_ref[...]) + bdec_ref[...]
    vt = _dot(n, w_ref[:, _C_V:_C_GQ]).T
    gs = _dot(n, w_ref[:, _C_GS:_C_GT])
    gt = _dot(n, w_ref[:, _C_GT:c_low]) + bgate_ref[...]

    qk = qk * lax.rsqrt(ms + EPS) * gqk_ref[...]
    lane = lax.broadcasted_iota(jnp.int32, (tm, LANES), 1)
    first = (lane % HEAD_DIM) < (HEAD_DIM // 2)
    cos, sin = cos_ref[...], sin_ref[...]
    rot = []
    for j in range((ATTN_Q_W + ATTN_KV_W) // LANES):
        xs = qk[:, j * LANES:(j + 1) * LANES]
        other = jnp.where(first, pltpu.roll(xs, LANES - HEAD_DIM // 2, 1), pltpu.roll(xs, HEAD_DIM // 2, 1))
        rot.append(xs * cos + other * sin)
    q_scale = HEAD_DIM ** -0.5 * float(np.log2(np.e))
    for j in range(ATTN_Q_W // LANES):
        qt_ref[0, j * LANES:(j + 1) * LANES, :] = (rot[j] * q_scale).T.astype(BF16)
    k_rot = rot[ATTN_Q_W // LANES]
    for g in range(ATTN_KV_HEADS):
        k_ref[0, g] = k_rot[:, g * HEAD_DIM:(g + 1) * HEAD_DIM].astype(BF16)

    tail = jnp.where(lax.broadcasted_iota(jnp.int32, (V_ROWS - HEAD_DIM, tm), 0) == 0, 1.0, 0.0)
    for g in range(ATTN_KV_HEADS):
        vt_ref[0, g] = jnp.concatenate([vt[g * HEAD_DIM:(g + 1) * HEAD_DIM], tail], axis=0).astype(BF16)

    gq_ref[0] = (gqk[:, :GLA_K_W] * (GLA_DK ** -0.5)).astype(BF16)
    gk_ref[0] = gqk[:, GLA_K_W:].astype(BF16)
    gv_ref[0] = gv.astype(BF16)
    gvt_ref[0] = gv.T.astype(BF16)

    la = (jnp.minimum(z, 0.0) - jnp.log(1.0 + jnp.exp(-jnp.abs(z)))) * (1.0 / GLA_GATE_NORM)
    laf_ref[0] = la[:, :GLA_K_W]
    lab_ref[0] = la[:, GLA_K_W:]

    gs_ref[0] = (gs * _sigmoid(gs)).astype(BF16)
    gt_ref[0] = _sigmoid(gt).astype(BF16)


def _in_proj(h, mod2, g, w_r, gqk, seg, cos_t, sin_t, w_dec, b_dec, b_gate, *, tm=512):
    nb, t, d = h.shape
    tm = min(tm, t)
    wp = w_r.shape[1]
    tok = lambda w: pl.BlockSpec((1, tm, w), lambda b, i: (b, i, 0))
    out_shape = [
        jax.ShapeDtypeStruct((nb, ATTN_Q_W, t), BF16),
        jax.ShapeDtypeStruct((nb, ATTN_KV_HEADS, t, HEAD_DIM), BF16),
        jax.ShapeDtypeStruct((nb, ATTN_KV_HEADS, V_ROWS, t), BF16),
        jax.ShapeDtypeStruct((nb, t, GLA_K_W), BF16),
        jax.ShapeDtypeStruct((nb, t, GLA_K_W), BF16),
        jax.ShapeDtypeStruct((nb, t, GLA_V_W), BF16),
        jax.ShapeDtypeStruct((nb, GLA_V_W, t), BF16),
        jax.ShapeDtypeStruct((nb, t, GLA_K_W), F32),
        jax.ShapeDtypeStruct((nb, t, GLA_K_W), F32),
        jax.ShapeDtypeStruct((nb, t, GLA_V_W), BF16),
        jax.ShapeDtypeStruct((nb, t, 2 * d), BF16),
    ]
    out_specs = [
        pl.BlockSpec((1, ATTN_Q_W, tm), lambda b, i: (b, 0, i)),
        pl.BlockSpec((1, ATTN_KV_HEADS, tm, HEAD_DIM), lambda b, i: (b, 0, i, 0)),
        pl.BlockSpec((1, ATTN_KV_HEADS, V_ROWS, tm), lambda b, i: (b, 0, 0, i)),
        tok(GLA_K_W), tok(GLA_K_W), tok(GLA_V_W),
        pl.BlockSpec((1, GLA_V_W, tm), lambda b, i: (b, 0, i)),
        tok(GLA_K_W), tok(GLA_K_W), tok(GLA_V_W), tok(2 * d),
    ]
    in_specs = [
        pl.BlockSpec((1, tm, d), lambda b, i: (b, i, 0)),
        pl.BlockSpec((1, 2, d), lambda b, i: (b, 0, 0)),
        _const_spec((1, d)), _const_spec((d, wp)), _const_spec((1, ATTN_Q_W + ATTN_KV_W)),
        _const_spec((2 * LANES, LANES)),
        pl.BlockSpec((tm, LANES), lambda b, i: (i, 0)),
        pl.BlockSpec((tm, LANES), lambda b, i: (i, 0)),
        _const_spec((LANES, 2 * GLA_K_W)),
        _const_spec((1, 2 * GLA_K_W)), _const_spec((1, 2 * d)),
    ]
    return pl.pallas_call(
        _proj_kernel,
        grid=(nb, t // tm),
        in_specs=in_specs,
        out_specs=out_specs,
        out_shape=out_shape,
        compiler_params=pltpu.CompilerParams(dimension_semantics=("arbitrary", "arbitrary"),
                                             vmem_limit_bytes=VMEM_LIMIT),
        name="in_proj",
    )(h, mod2, g.reshape(1, d), w_r, gqk, seg, cos_t, sin_t, w_dec, b_dec, b_gate)


ATTN_GROUPS = 2
ATTN_ROW_BLOCK = 256


def _attn_kernel(qt_ref, k_ref, vt_ref, o_ref, qs_ref, *group_refs, kc):
    tq = qt_ref.shape[2]
    n = k_ref.shape[2] // kc
    gw = Q_GROUP * tq // ATTN_GROUPS
    rb = min(ATTN_ROW_BLOCK, kc)
    per = len(group_refs) // ATTN_GROUPS
    m_refs, cmax_refs, acc_refs, s_refs = (
        [group_refs[g * per + j] for g in range(ATTN_GROUPS)] for j in range(per))
    for h in range(Q_GROUP):
        qs_ref[:, h * tq:(h + 1) * tq] = qt_ref[0, h * HEAD_DIM:(h + 1) * HEAD_DIM, :]
    for g in range(ATTN_GROUPS):
        m_refs[g][...] = jnp.full(m_refs[g].shape, -jnp.inf, F32)
        acc_refs[g][...] = jnp.zeros(acc_refs[g].shape, F32)

    def stage(score, apply):
        if apply is not None:
            ca, ga = apply
            m_old = m_refs[ga][...]
            m_new = jnp.maximum(m_old, cmax_refs[ga][...])
            m_refs[ga][...] = m_new
        cmax = pv_sum = None
        for j in range(kc // rb):
            blk = slice(j * rb, (j + 1) * rb)
            if apply is not None:
                p = jnp.exp2(s_refs[ga][blk, :] - m_new).astype(BF16)
            if score is not None:
                cs, gs = score
                rows = pl.ds(pl.multiple_of(cs * kc + j * rb, rb), rb)
                s = _dot(k_ref[0, 0, rows, :], qs_ref[:, gs * gw:(gs + 1) * gw])
                s_refs[gs][blk, :] = s
                bmax = jnp.max(s, axis=0, keepdims=True)
                cmax = bmax if cmax is None else jnp.maximum(cmax, bmax)
            if apply is not None:
                rows = pl.ds(pl.multiple_of(ca * kc + j * rb, rb), rb)
                part = _dot(vt_ref[0, 0, :, rows], p)
                pv_sum = part if pv_sum is None else pv_sum + part
        if score is not None:
            cmax_refs[gs][...] = cmax
        if apply is not None:
            acc_refs[ga][...] = jnp.exp2(m_old - m_new) * acc_refs[ga][...] + pv_sum

    last = ATTN_GROUPS - 1
    stage((0, 0), None)
    for g in range(1, ATTN_GROUPS):
        stage((0, g), (0, g - 1))

    def body(c, carry):
        stage((c + 1, 0), (c, last))
        for g in range(1, ATTN_GROUPS):
            stage((c + 1, g), (c + 1, g - 1))
        return carry

    lax.fori_loop(0, n - 1, body, 0)
    stage(None, (n - 1, last))

    acc = jnp.concatenate([r[...] for r in acc_refs], axis=1)
    ot = acc[:HEAD_DIM] / acc[HEAD_DIM:HEAD_DIM + 1]
    for j in range(Q_GROUP // 2):
        pair = jnp.concatenate([ot[:, (2 * j) * tq:(2 * j + 1) * tq], ot[:, (2 * j + 1) * tq:(2 * j + 2) * tq]],
                               axis=0)
        o_ref[0, :, j * LANES:(j + 1) * LANES] = pair.T.astype(o_ref.dtype)


ATTN_KC = 768


def _attention(qt, k, vt, *, tq=512):
    nb, _, t = qt.shape
    s_len = k.shape[2]
    tq = min(tq, t)
    kc = max(c for c in range(LANES, ATTN_KC + 1, LANES) if s_len % c == 0)
    gw = Q_GROUP * tq // ATTN_GROUPS
    return pl.pallas_call(
        functools.partial(_attn_kernel, kc=kc),
        grid=(nb, ATTN_KV_HEADS, t // tq),
        in_specs=[
            pl.BlockSpec((1, Q_GROUP * HEAD_DIM, tq), lambda b, g, i: (b, g, i)),
            pl.BlockSpec((1, 1, s_len, HEAD_DIM), lambda b, g, i: (b, g, 0, 0)),
            pl.BlockSpec((1, 1, V_ROWS, s_len), lambda b, g, i: (b, g, 0, 0)),
        ],
        out_specs=pl.BlockSpec((1, tq, Q_GROUP * HEAD_DIM), lambda b, g, i: (b, i, g)),
        out_shape=jax.ShapeDtypeStruct((nb, t, ATTN_Q_W), BF16),
        scratch_shapes=[pltpu.VMEM((HEAD_DIM, Q_GROUP * tq), BF16)] + ATTN_GROUPS * [
            pltpu.VMEM((1, gw), F32),
            pltpu.VMEM((1, gw), F32),
            pltpu.VMEM((V_ROWS, gw), F32),
            pltpu.VMEM((kc, gw), F32)],
        compiler_params=pltpu.CompilerParams(dimension_semantics=("arbitrary", "arbitrary", "arbitrary"),
                                             vmem_limit_bytes=VMEM_LIMIT),
        name="flash_attention",
    )(qt, k, vt)


def _gla_constants():
    c = GLA_CHUNK
    i = np.arange(c)[:, None]
    t = np.arange(c)[None, :]
    fwd = [t > i, t <= i]
    bwd = [t < i, t >= i]
    s = c // 2
    while s >= 1:
        mid = (i // (2 * s)) * (2 * s) + s
        second = (i % (2 * s)) >= s
        fwd.append(np.where(second, (t >= mid) & (t <= i), (t > i) & (t < mid)))
        bwd.append(np.where(second, (t >= mid) & (t < i), (t >= i) & (t < mid)))
        s //= 2
    to = lambda blocks: jnp.asarray(np.concatenate(blocks, axis=0).astype(np.float32), dtype=BF16)
    return to(fwd), to(bwd)


_GLA_LEVELS = int(np.log2(GLA_CHUNK))


def _cum(mat, la):
    hi, lo = _split2(la)
    r = _dot(mat, jnp.concatenate([hi, lo], axis=-1))
    return r[:, :LANES] + r[:, LANES:]


def _gla_kernel(q_ref, k_ref, v_ref, vt_ref, laf_ref, lab_ref, kc_ref, vtc_ref, lafc_ref, labc_ref,
                mf_ref, mb_ref, o_ref, st_ref, dec_ref, *, cpt):
    c = GLA_CHUNK
    n_lat = k_ref.shape[1] // c
    n_ctx = kc_ref.shape[1] // c
    n_all = n_ctx + n_lat
    tile = pl.program_id(2)
    lane = lax.broadcasted_iota(jnp.int32, (c, LANES), 1)
    head_masks = [lane < GLA_DK, lane >= GLA_DK]
    lane2 = lax.broadcasted_iota(jnp.int32, (c, 2 * LANES), 1) % LANES
    pair_masks = [lane2 < GLA_DK, lane2 >= GLA_DK]

    def increments(k_r, vt_r, laf_r, lab_r, n, slot):
        off = pl.multiple_of(n * c, c)
        k = k_r[0, pl.ds(off, c), :].astype(F32)
        rf = _cum(mf_ref[0:2 * c, :], laf_r[0, pl.ds(off, c), :])
        rb = _cum(mb_ref[0:2 * c, :], lab_r[0, pl.ds(off, c), :])
        kfb = jnp.concatenate([k * jnp.exp(rf[0:c]),
                               k * jnp.exp(rb[0:c])], axis=-1)
        kk = jnp.concatenate([jnp.where(pair_masks[0], kfb, 0.0), jnp.where(pair_masks[1], kfb, 0.0)],
                             axis=0).astype(BF16)
        vt2 = jnp.concatenate([vt_r[0, 0:GLA_DV, pl.ds(off, c)], vt_r[0, GLA_DV:2 * GLA_DV, pl.ds(off, c)]],
                              axis=-1)
        st_ref[slot] = _dot(vt2, kk)
        dec_ref[slot, 0:1, :] = jnp.exp(rf[2 * c - 1:2 * c, :])
        dec_ref[slot, 1:2, :] = jnp.exp(rb[c:c + 1, :])

    @pl.when(tile == 0)
    def _():
        def ctx_body(n, carry):
            increments(kc_ref, vtc_ref, lafc_ref, labc_ref, n, n)
            return carry
        lax.fori_loop(0, n_ctx, ctx_body, 0)

        def lat_body(n, carry):
            increments(k_ref, vt_ref, laf_ref, lab_ref, n, n_ctx + n)
            return carry
        lax.fori_loop(0, n_lat, lat_body, 0, unroll=2)

        def fwd_body(s, st):
            inc = st_ref[s, :, 0:LANES]
            st_ref[s, :, 0:LANES] = st
            return st * dec_ref[s, 0:1, :] + inc
        lax.fori_loop(0, n_all, fwd_body, jnp.zeros((GLA_DV, LANES), F32))

        def bwd_body(j, st, base, count):
            s = base + count - 1 - j
            inc = st_ref[s, :, LANES:2 * LANES]
            st_ref[s, :, LANES:2 * LANES] = st
            return st * dec_ref[s, 1:2, :] + inc
        st = lax.fori_loop(0, n_ctx, functools.partial(bwd_body, base=0, count=n_ctx),
                           jnp.zeros((GLA_DV, LANES), F32))
        lax.fori_loop(0, n_lat, functools.partial(bwd_body, base=n_ctx, count=n_lat), st)

    xor2 = lax.broadcasted_iota(jnp.int32, (c, 2 * c), 0) ^ (lax.broadcasted_iota(jnp.int32, (c, 2 * c), 1) & (c - 1))
    row_l = lax.broadcasted_iota(jnp.int32, (c, LANES), 0)
    lane2v = lax.broadcasted_iota(jnp.int32, (c, 2 * GLA_DV), 1)
    lane2s = lax.broadcasted_iota(jnp.int32, (GLA_DV, 2 * LANES), 1) % LANES

    def out_body(jj, carry):
        n = tile * cpt + jj
        off = pl.multiple_of(n * c, c)
        loc = pl.multiple_of(jj * c, c)
        q = q_ref[0, pl.ds(loc, c), :].astype(F32)
        k = k_ref[0, pl.ds(off, c), :].astype(F32)
        rf = _cum(mf_ref[c:, :], laf_ref[0, pl.ds(off, c), :])
        rb = _cum(mb_ref[c:, :], lab_ref[0, pl.ds(off, c), :])

        def stack_heads(x):
            return jnp.concatenate([jnp.where(head_masks[0], x, 0.0), jnp.where(head_masks[1], x, 0.0)], axis=0)

        a = 2.0 * _dot_nt(q.astype(BF16), stack_heads(k).astype(BF16))
        for lvl in range(_GLA_LEVELS):
            sh = _GLA_LEVELS - 1 - lvl
            second = ((row_l >> sh) & 1) == 1
            ef = jnp.exp(rf[(1 + lvl) * c:(2 + lvl) * c])
            eb = jnp.exp(rb[(1 + lvl) * c:(2 + lvl) * c])
            ql = (q * jnp.where(second, ef, eb)).astype(BF16)
            kl = stack_heads(k * jnp.where(second, eb, ef)).astype(BF16)
            a = jnp.where((xor2 >> sh) == 1, _dot_nt(ql, kl), a)
        a = a.astype(BF16)
        v = v_ref[0, pl.ds(loc, c), :]
        v_bd = jnp.concatenate([jnp.where(lane2v < GLA_DV, v, jnp.zeros_like(v)),
                                jnp.where(lane2v >= GLA_DV, v, jnp.zeros_like(v))], axis=0)
        q_inter = jnp.concatenate([q * jnp.exp(rf[0:c]), q * jnp.exp(rb[0:c])], axis=-1).astype(BF16)
        states = st_ref[n_ctx + n]
        st2 = jnp.concatenate([jnp.where(lane2s < GLA_DK, states, 0.0), jnp.where(lane2s >= GLA_DK, states, 0.0)],
                              axis=0).astype(BF16)
        o_ref[0, pl.ds(loc, c), :] = _dot(a, v_bd) + _dot_nt(q_inter, st2)
        return carry

    lax.fori_loop(0, cpt, out_body, 0, unroll=2)


def _gla(gq, gk, gv, gvt, laf, lab, gk_c, gvt_c, laf_c, lab_c, mf, mb, *, tile=1024):
    nb, t, _ = gq.shape
    tc = gk_c.shape[1] // nb
    tile = min(tile, t)
    c = GLA_CHUNK
    n_all = (t + tc) // c
    pair_k = 2 * GLA_DK
    pair_v = 2 * GLA_DV
    return pl.pallas_call(
        functools.partial(_gla_kernel, cpt=tile // c),
        grid=(nb, GLA_HEADS // 2, t // tile),
        in_specs=[
            pl.BlockSpec((1, tile, pair_k), lambda b, p, i: (b, i, p)),
            pl.BlockSpec((1, t, pair_k), lambda b, p, i: (b, 0, p)),
            pl.BlockSpec((1, tile, pair_v), lambda b, p, i: (b, i, p)),
            pl.BlockSpec((1, pair_v, t), lambda b, p, i: (b, p, 0)),
            pl.BlockSpec((1, t, pair_k), lambda b, p, i: (b, 0, p)),
            pl.BlockSpec((1, t, pair_k), lambda b, p, i: (b, 0, p)),
            pl.BlockSpec((1, tc, pair_k), lambda b, p, i: (0, b, p)),
            pl.BlockSpec((1, pair_v, tc), lambda b, p, i: (0, p, b)),
            pl.BlockSpec((1, tc, pair_k), lambda b, p, i: (0, b, p)),
            pl.BlockSpec((1, tc, pair_k), lambda b, p, i: (0, b, p)),
            _const_spec(tuple(mf.shape)), _const_spec(tuple(mb.shape)),
        ],
        out_specs=pl.BlockSpec((1, tile, pair_v), lambda b, p, i: (b, i, p)),
        out_shape=jax.ShapeDtypeStruct((nb, t, GLA_V_W), F32),
        scratch_shapes=[pltpu.VMEM((n_all, GLA_DV, 2 * LANES), F32),
                        pltpu.VMEM((n_all, 8, LANES), F32)],
        compiler_params=pltpu.CompilerParams(dimension_semantics=("arbitrary", "arbitrary", "arbitrary"),
                                             vmem_limit_bytes=VMEM_LIMIT),
        name="gla",
    )(gq, gk, gv, gvt, laf, lab, gk_c, gvt_c, laf_c, lab_c, mf, mb)


def _rope_tables(t):
    rows = t // GRID_W
    row = jnp.repeat(jnp.arange(rows, dtype=F32), GRID_W)
    col = jnp.tile(jnp.arange(GRID_W, dtype=F32), rows)
    freqs = ROPE_THETA ** (-jnp.arange(0, ROPE_AXIS_DIM, 2, dtype=F32) / ROPE_AXIS_DIM)
    ang = jnp.concatenate([row[:, None] * freqs, col[:, None] * freqs], axis=-1)
    cos, sin = jnp.cos(ang), jnp.sin(ang)
    reps = LANES // HEAD_DIM
    return (jnp.tile(jnp.concatenate([cos, cos], axis=-1), (1, reps)),
            jnp.tile(jnp.concatenate([-sin, sin], axis=-1), (1, reps)))


def kernel(x, c, ctx, c_ctx, w_mod, b_mod, g_norm, w_ffn_up, w_ffn_down, w_in, g_q, g_k,
           w_decay, b_decay, g_gla, w_branch, b_gate, w_out, g_final):
    nb, t, d = x.shape
    tc = ctx.shape[1]
    f = w_ffn_down.shape[2]
    assert w_mod.shape[0] == 1, "single layer"
    assert t % GLA_CHUNK == 0 and tc % GLA_CHUNK == 0 and t % GRID_W == 0

    rows = -(-(nb + 1) // 8) * 8
    c_rows = jnp.zeros((rows, d), F32).at[:nb].set(c).at[nb].set(c_ctx)
    m = _modulation(c_rows, w_mod[0], b_mod[0]).reshape(rows, N_MOD, d)
    m_lat, m_ctx = m[:nb], m[nb:nb + 1]

    wa = [w_ffn_up[0, i, :, :f].astype(BF16) for i in range(2)]
    wb = [w_ffn_up[0, i, :, f:].astype(BF16) for i in range(2)]
    wd = [w_ffn_down[0, i].astype(BF16) for i in range(2)]
    c_low = _C_GT
    w_low = w_in[0][:, c_low:c_low + 2 * GLA_RANK]
    w_r = jnp.concatenate([w_in[0][:, :c_low], w_in[0][:, c_low + 2 * GLA_RANK:], w_low,
                           jnp.zeros((d, LANES - 2 * GLA_RANK), F32)], axis=-1).astype(BF16)
    w_bd = jnp.zeros((2 * GLA_RANK, 2 * GLA_K_W), F32)
    w_bd = w_bd.at[:GLA_RANK, :GLA_K_W].set(w_decay[0, 0]).at[GLA_RANK:, GLA_K_W:].set(w_decay[0, 1])
    w_bd_hi = w_bd.astype(BF16)
    w_bd_lo = (w_bd - w_bd_hi.astype(F32)).astype(BF16)
    w_dec = jnp.concatenate([w_bd_hi, w_bd_hi, w_bd_lo, jnp.zeros_like(w_bd_hi)], axis=0)
    b_dec = b_decay[0].reshape(1, 2 * GLA_K_W)
    gqk = jnp.concatenate([jnp.tile(g_q[0], ATTN_HEADS), jnp.tile(g_k[0], ATTN_KV_HEADS)]).reshape(1, -1)
    lane = np.arange(LANES)
    seg = jnp.asarray(np.tile((lane[:, None] // HEAD_DIM == lane[None, :] // HEAD_DIM) / HEAD_DIM, (2, 1)), dtype=BF16)
    cos_t, sin_t = _rope_tables(t)
    ones_t, zeros_t = jnp.ones((nb * tc, LANES), F32), jnp.zeros((nb * tc, LANES), F32)
    mf, mb = _gla_constants()

    proj = functools.partial(_in_proj, g=g_norm[0, 1], w_r=w_r, gqk=gqk, seg=seg, w_dec=w_dec,
                             b_dec=b_dec, b_gate=b_gate[0].reshape(1, 2 * d))

    hc = _half_ffn(ctx.reshape(1, nb * tc, d), m_ctx[:, 0:3], g_norm[0, 0], wa[0], wb[0], wd[0])
    pc = proj(hc, m_ctx[:, 3:5], cos_t=ones_t, sin_t=zeros_t)
    _, k_c, vt_c, _, gk_c, _, gvt_c, laf_c, lab_c, _, _ = pc

    h1 = _half_ffn(x, m_lat[:, 0:3], g_norm[0, 0], wa[0], wb[0], wd[0])
    qt, k, vt, gq, gk, gv, gvt, laf, lab, gs, gt = proj(h1, m_lat[:, 3:5], cos_t=cos_t, sin_t=sin_t)
    k_all = jnp.concatenate(
        [k, k_c.reshape(ATTN_KV_HEADS, nb, tc, HEAD_DIM).transpose(1, 0, 2, 3)], axis=2)
    vt_all = jnp.concatenate(
        [vt, vt_c.reshape(ATTN_KV_HEADS, V_ROWS, nb, tc).transpose(2, 0, 1, 3)], axis=3)
    attn_o = _attention(qt, k_all, vt_all)
    gla_o = _gla(gq, gk, gv, gvt, laf, lab, gk_c, gvt_c, laf_c, lab_c, mf, mb)
    return _merge_ffn(h1, attn_o, gla_o, gs, gt, m_lat[:, 5:6], g_gla[0],
                      w_branch[0, 0].astype(BF16), w_branch[0, 1].astype(BF16), w_out[0].astype(BF16),
                      m_lat[:, 6:9], g_norm[0, 2], wa[1], wb[1], wd[1], g_final)
```

```python
import functools

import numpy as np
import jax
import jax.numpy as jnp
from jax import lax
from jax.experimental import pallas as pl
from jax.experimental.pallas import tpu as pltpu

F32 = jnp.float32
BF16 = jnp.bfloat16

EPS = 1e-6
GRID_W = 64
N_MOD = 9
ATTN_HEADS = 8
ATTN_KV_HEADS = 2
HEAD_DIM = 64
ROPE_AXIS_DIM = HEAD_DIM // 2
ROPE_THETA = 10000.0
GLA_HEADS = 4
GLA_DK = 64
GLA_DV = 128
GLA_RANK = 16
GLA_GATE_NORM = 16.0
ATTN_Q_W = ATTN_HEADS * HEAD_DIM
ATTN_KV_W = ATTN_KV_HEADS * HEAD_DIM
GLA_K_W = GLA_HEADS * GLA_DK
GLA_V_W = GLA_HEADS * GLA_DV
Q_GROUP = ATTN_HEADS // ATTN_KV_HEADS

LANES = 128
V_ROWS = 80
GLA_CHUNK = 128
VMEM_LIMIT = 56 * 1024 * 1024


def _dot(a, b):
    return jnp.dot(a, b, preferred_element_type=F32)


def _dot_nt(a, b):
    return lax.dot_general(a, b, (((1,), (1,)), ((), ())), preferred_element_type=F32)


def _sigmoid(x):
    return 1.0 / (1.0 + jnp.exp(-x))


def _split2(x):
    hi = x.astype(BF16)
    lo = (x - hi.astype(F32)).astype(BF16)
    return hi, lo


def _rms(x):
    return x * lax.rsqrt(jnp.mean(x * x, axis=-1, keepdims=True) + EPS)


def _const_spec(shape):
    nd = len(shape)
    return pl.BlockSpec(shape, lambda *_: (0,) * nd, pipeline_mode=pl.Buffered(1))


def _mod_kernel(c_ref, w_ref, b_ref, o_ref):
    c = c_ref[...]
    s_hi, s_lo = _split2(c * _sigmoid(c))
    w_hi, w_lo = _split2(w_ref[...])
    o_ref[...] = _dot(s_hi, w_hi) + _dot(s_hi, w_lo) + _dot(s_lo, w_hi) + b_ref[...]


def _modulation(c_rows, w_mod, b_mod):
    rows, d = c_rows.shape
    n = w_mod.shape[1]
    tn = 1024
    return pl.pallas_call(
        _mod_kernel,
        grid=(n // tn,),
        in_specs=[pl.BlockSpec((rows, d), lambda j: (0, 0)),
                  pl.BlockSpec((d, tn), lambda j: (0, j)),
                  pl.BlockSpec((1, tn), lambda j: (0, j))],
        out_specs=pl.BlockSpec((rows, tn), lambda j: (0, j)),
        out_shape=jax.ShapeDtypeStruct((rows, n), F32),
        compiler_params=pltpu.CompilerParams(dimension_semantics=("arbitrary",),
                                             vmem_limit_bytes=VMEM_LIMIT),
        name="modulation",
    )(c_rows, w_mod, b_mod.reshape(1, n))


def _ffn_math(x, mod_ref, g_ref, wa_ref, wb_ref, wd_ref, n_chunks):
    shift, scale, gate = mod_ref[0, 0:1, :], mod_ref[0, 1:2, :], mod_ref[0, 2:3, :]
    n = ((_rms(x) * g_ref[...]) * (1.0 + scale) + shift).astype(BF16)
    fc = wa_ref.shape[1] // n_chunks
    acc = None
    for c in range(n_chunks):
        a = _dot(n, wa_ref[:, c * fc:(c + 1) * fc])
        b = _dot(n, wb_ref[:, c * fc:(c + 1) * fc])
        act = ((a * _sigmoid(a)) * b).astype(BF16)
        part = _dot(act, wd_ref[c * fc:(c + 1) * fc, :])
        acc = part if acc is None else acc + part
    return x + 0.5 * gate * acc


def _ffn_kernel(h_ref, mod_ref, g_ref, wa_ref, wb_ref, wd_ref, o_ref, *, n_chunks):
    o_ref[0] = _ffn_math(h_ref[0], mod_ref, g_ref, wa_ref, wb_ref, wd_ref, n_chunks)


def _merge_ffn_kernel(h_ref, ao_ref, go_ref, gs_ref, gt_ref, mg_ref, gg_ref, wb0_ref, wb1_ref, wo_ref,
                      mod_ref, g_ref, wa_ref, wb_ref, wd_ref, gf_ref, o_ref, *, n_chunks):
    d = h_ref.shape[2]
    go = go_ref[0]
    normed = [_rms(go[:, h * GLA_DV:(h + 1) * GLA_DV]) * gg_ref[...] for h in range(GLA_HEADS)]
    gn = (jnp.concatenate(normed, axis=-1) * gs_ref[0].astype(F32)).astype(BF16)
    y_attn = _dot(ao_ref[0], wb0_ref[...])
    y_gla = _dot(gn, wb1_ref[...])
    gt = gt_ref[0].astype(F32)
    z = (gt[:, :d] * y_attn + gt[:, d:] * y_gla).astype(BF16)
    h2 = h_ref[0] + mg_ref[0] * _dot(z, wo_ref[...])
    out = _ffn_math(h2, mod_ref, g_ref, wa_ref, wb_ref, wd_ref, n_chunks)
    o_ref[0] = _rms(out) * gf_ref[...]


def _ffn_weight_specs(d, f):
    return [_const_spec((1, d)), _const_spec((d, f)), _const_spec((d, f)), _const_spec((f, d))]


def _half_ffn(h, mod3, g, wa, wb, wd, *, tm=512, n_chunks=1):
    nb, t, d = h.shape
    f = wa.shape[1]
    tm = min(tm, t)
    return pl.pallas_call(
        functools.partial(_ffn_kernel, n_chunks=n_chunks),
        grid=(nb, t // tm),
        in_specs=[pl.BlockSpec((1, tm, d), lambda b, i: (b, i, 0)),
                  pl.BlockSpec((1, 3, d), lambda b, i: (b, 0, 0))] + _ffn_weight_specs(d, f),
        out_specs=pl.BlockSpec((1, tm, d), lambda b, i: (b, i, 0)),
        out_shape=jax.ShapeDtypeStruct((nb, t, d), F32),
        compiler_params=pltpu.CompilerParams(dimension_semantics=("arbitrary", "arbitrary"),
                                             vmem_limit_bytes=VMEM_LIMIT),
        name="half_ffn",
    )(h, mod3, g.reshape(1, d), wa, wb, wd)


def _merge_ffn(h, attn_o, gla_o, gs, gt, m_gate, g_gla, wb0, wb1, wo, mod3, g, wa, wb, wd, g_final,
               *, tm=512, n_chunks=1):
    nb, t, d = h.shape
    f = wa.shape[1]
    tm = min(tm, t)
    tok = lambda w: pl.BlockSpec((1, tm, w), lambda b, i: (b, i, 0))
    return pl.pallas_call(
        functools.partial(_merge_ffn_kernel, n_chunks=n_chunks),
        grid=(nb, t // tm),
        in_specs=[tok(d), tok(ATTN_Q_W), tok(GLA_V_W), tok(GLA_V_W), tok(2 * d),
                  pl.BlockSpec((1, 1, d), lambda b, i: (b, 0, 0)),
                  _const_spec((1, GLA_DV)), _const_spec((ATTN_Q_W, d)), _const_spec((GLA_V_W, d)),
                  _const_spec((d, d)),
                  pl.BlockSpec((1, 3, d), lambda b, i: (b, 0, 0))] + _ffn_weight_specs(d, f)
                 + [_const_spec((1, d))],
        out_specs=tok(d),
        out_shape=jax.ShapeDtypeStruct((nb, t, d), F32),
        compiler_params=pltpu.CompilerParams(dimension_semantics=("arbitrary", "arbitrary"),
                                             vmem_limit_bytes=VMEM_LIMIT),
        name="merge_ffn_final",
    )(h, attn_o, gla_o, gs, gt, m_gate, g_gla.reshape(1, GLA_DV), wb0, wb1, wo,
      mod3, g.reshape(1, d), wa, wb, wd, g_final.reshape(1, d))


_C_Q = 0
_C_K = _C_Q + ATTN_Q_W
_C_V = _C_K + ATTN_KV_W
_C_GQ = _C_V + ATTN_KV_W
_C_GK = _C_GQ + GLA_K_W
_C_GV = _C_GK + GLA_K_W
_C_GS = _C_GV + GLA_V_W
_C_GT = _C_GS + GLA_V_W


def _proj_kernel(h_ref, mod_ref, g_ref, w_ref, gqk_ref, seg_ref, cos_ref, sin_ref, wdec_ref,
                 bdec_ref, bgate_ref,
                 qt_ref, k_ref, vt_ref, gq_ref, gk_ref, gv_ref, gvt_ref, laf_ref, lab_ref, gs_ref, gt_ref):
    d = h_ref.shape[2]
    c_low = _C_GT + 2 * d
    x = h_ref[0]
    shift, scale = mod_ref[0, 0:1, :], mod_ref[0, 1:2, :]
    n = ((_rms(x) * g_ref[...]) * (1.0 + scale) + shift).astype(BF16)
    tm = x.shape[0]

    qk = _dot(n, w_ref[:, _C_Q:_C_V])
    low = _dot(n, w_ref[:, c_low:c_low + LANES])
    gv = _dot(n, w_ref[:, _C_GV:_C_GS])

    seg = seg_ref[...]
    ms = []
    for j in range((ATTN_Q_W + ATTN_KV_W) // LANES):
        sq = qk[:, j * LANES:(j + 1) * LANES]
        hi, lo = _split2(sq * sq)
        ms.append(_dot(jnp.concatenate([hi, lo], axis=-1), seg))
    ms = jnp.concatenate(ms, axis=-1)
    gqk = _dot(n, w_ref[:, _C_GQ:_C_GV])

    l_hi = low.astype(BF16).astype(F32)
    l_lo = low - l_hi
    packed = l_hi + pltpu.roll(l_lo, 2 * GLA_RANK, 1) + pltpu.roll(l_hi, 4 * GLA_RANK, 1)
    z = _dot(packed.astype(BF16), wdec_ref[...]) + bdec_ref[...]
    vt = _dot(n, w_ref[:, _C_V:_C_GQ]).T
    gs = _dot(n, w_ref[:, _C_GS:_C_GT])
    gt = _dot(n, w_ref[:, _C_GT:c_low]) + bgate_ref[...]

    qk = qk * lax.rsqrt(ms + EPS) * gqk_ref[...]
    lane = lax.broadcasted_iota(jnp.int32, (tm, LANES), 1)
    first = (lane % HEAD_DIM) < (HEAD_DIM // 2)
    cos, sin = cos_ref[...], sin_ref[...]
    rot = []
    for j in range((ATTN_Q_W + ATTN_KV_W) // LANES):
        xs = qk[:, j * LANES:(j + 1) * LANES]
        other = jnp.where(first, pltpu.roll(xs, LANES - HEAD_DIM // 2, 1), pltpu.roll(xs, HEAD_DIM // 2, 1))
        rot.append(xs * cos + other * sin)
    q_scale = HEAD_DIM ** -0.5 * float(np.log2(np.e))
    for j in range(ATTN_Q_W // LANES):
        qt_ref[0, j * LANES:(j + 1) * LANES, :] = (rot[j] * q_scale).T.astype(BF16)
    k_rot = rot[ATTN_Q_W // LANES]
    for g in range(ATTN_KV_HEADS):
        k_ref[0, g] = k_rot[:, g * HEAD_DIM:(g + 1) * HEAD_DIM].astype(BF16)

    tail = jnp.where(lax.broadcasted_iota(jnp.int32, (V_ROWS - HEAD_DIM, tm), 0) == 0, 1.0, 0.0)
    for g in range(ATTN_KV_HEADS):
        vt_ref[0, g] = jnp.concatenate([vt[g * HEAD_DIM:(g + 1) * HEAD_DIM], tail], axis=0).astype(BF16)

    gq_ref[0] = (gqk[:, :GLA_K_W] * (GLA_DK ** -0.5)).astype(BF16)
    gk_ref[0] = gqk[:, GLA_K_W:].astype(BF16)
    gv_ref[0] = gv.astype(BF16)
    gvt_ref[0] = gv.T.astype(BF16)

    la = (jnp.minimum(z, 0.0) - jnp.log(1.0 + jnp.exp(-jnp.abs(z)))) * (1.0 / GLA_GATE_NORM)
    laf_ref[0] = la[:, :GLA_K_W]
    lab_ref[0] = la[:, GLA_K_W:]

    gs_ref[0] = (gs * _sigmoid(gs)).astype(BF16)
    gt_ref[0] = _sigmoid(gt).astype(BF16)


def _in_proj(h, mod2, g, w_r, gqk, seg, cos_t, sin_t, w_dec, b_dec, b_gate, *, tm=512):
    nb, t, d = h.shape
    tm = min(tm, t)
    wp = w_r.shape[1]
    tok = lambda w: pl.BlockSpec((1, tm, w), lambda b, i: (b, i, 0))
    out_shape = [
        jax.ShapeDtypeStruct((nb, ATTN_Q_W, t), BF16),
        jax.ShapeDtypeStruct((nb, ATTN_KV_HEADS, t, HEAD_DIM), BF16),
        jax.ShapeDtypeStruct((nb, ATTN_KV_HEADS, V_ROWS, t), BF16),
        jax.ShapeDtypeStruct((nb, t, GLA_K_W), BF16),
        jax.ShapeDtypeStruct((nb, t, GLA_K_W), BF16),
        jax.ShapeDtypeStruct((nb, t, GLA_V_W), BF16),
        jax.ShapeDtypeStruct((nb, GLA_V_W, t), BF16),
        jax.ShapeDtypeStruct((nb, t, GLA_K_W), F32),
        jax.ShapeDtypeStruct((nb, t, GLA_K_W), F32),
        jax.ShapeDtypeStruct((nb, t, GLA_V_W), BF16),
        jax.ShapeDtypeStruct((nb, t, 2 * d), BF16),
    ]
    out_specs = [
        pl.BlockSpec((1, ATTN_Q_W, tm), lambda b, i: (b, 0, i)),
        pl.BlockSpec((1, ATTN_KV_HEADS, tm, HEAD_DIM), lambda b, i: (b, 0, i, 0)),
        pl.BlockSpec((1, ATTN_KV_HEADS, V_ROWS, tm), lambda b, i: (b, 0, 0, i)),
        tok(GLA_K_W), tok(GLA_K_W), tok(GLA_V_W),
        pl.BlockSpec((1, GLA_V_W, tm), lambda b, i: (b, 0, i)),
        tok(GLA_K_W), tok(GLA_K_W), tok(GLA_V_W), tok(2 * d),
    ]
    in_specs = [
        pl.BlockSpec((1, tm, d), lambda b, i: (b, i, 0)),
        pl.BlockSpec((1, 2, d), lambda b, i: (b, 0, 0)),
        _const_spec((1, d)), _const_spec((d, wp)), _const_spec((1, ATTN_Q_W + ATTN_KV_W)),
        _const_spec((2 * LANES, LANES)),
        pl.BlockSpec((tm, LANES), lambda b, i: (i, 0)),
        pl.BlockSpec((tm, LANES), lambda b, i: (i, 0)),
        _const_spec((LANES, 2 * GLA_K_W)),
        _const_spec((1, 2 * GLA_K_W)), _const_spec((1, 2 * d)),
    ]
    return pl.pallas_call(
        _proj_kernel,
        grid=(nb, t // tm),
        in_specs=in_specs,
        out_specs=out_specs,
        out_shape=out_shape,
        compiler_params=pltpu.CompilerParams(dimension_semantics=("arbitrary", "arbitrary"),
                                             vmem_limit_bytes=VMEM_LIMIT),
        name="in_proj",
    )(h, mod2, g.reshape(1, d), w_r, gqk, seg, cos_t, sin_t, w_dec, b_dec, b_gate)


ATTN_GROUPS = 2
ATTN_ROW_BLOCK = 256


def _attn_kernel(qt_ref, k_ref, vt_ref, o_ref, qs_ref, *group_refs, kc):
    tq = qt_ref.shape[2]
    n = k_ref.shape[2] // kc
    gw = Q_GROUP * tq // ATTN_GROUPS
    rb = min(ATTN_ROW_BLOCK, kc)
    per = len(group_refs) // ATTN_GROUPS
    m_refs, cmax_refs, acc_refs, s_refs = (
        [group_refs[g * per + j] for g in range(ATTN_GROUPS)] for j in range(per))
    for h in range(Q_GROUP):
        qs_ref[:, h * tq:(h + 1) * tq] = qt_ref[0, h * HEAD_DIM:(h + 1) * HEAD_DIM, :]
    for g in range(ATTN_GROUPS):
        m_refs[g][...] = jnp.full(m_refs[g].shape, -jnp.inf, F32)
        acc_refs[g][...] = jnp.zeros(acc_refs[g].shape, F32)

    def stage(score, apply):
        if apply is not None:
            ca, ga = apply
            m_old = m_refs[ga][...]
            m_new = jnp.maximum(m_old, cmax_refs[ga][...])
            m_refs[ga][...] = m_new
        cmax = pv_sum = None
        for j in range(kc // rb):
            blk = slice(j * rb, (j + 1) * rb)
            if apply is not None:
                p = jnp.exp2(s_refs[ga][blk, :] - m_new).astype(BF16)
            if score is not None:
                cs, gs = score
                rows = pl.ds(pl.multiple_of(cs * kc + j * rb, rb), rb)
                s = _dot(k_ref[0, 0, rows, :], qs_ref[:, gs * gw:(gs + 1) * gw])
                s_refs[gs][blk, :] = s
                bmax = jnp.max(s, axis=0, keepdims=True)
                cmax = bmax if cmax is None else jnp.maximum(cmax, bmax)
            if apply is not None:
                rows = pl.ds(pl.multiple_of(ca * kc + j * rb, rb), rb)
                part = _dot(vt_ref[0, 0, :, rows], p)
                pv_sum = part if pv_sum is None else pv_sum + part
        if score is not None:
            cmax_refs[gs][...] = cmax
        if apply is not None:
            acc_refs[ga][...] = jnp.exp2(m_old - m_new) * acc_refs[ga][...] + pv_sum

    last = ATTN_GROUPS - 1
    stage((0, 0), None)
    for g in range(1, ATTN_GROUPS):
        stage((0, g), (0, g - 1))

    def body(c, carry):
        stage((c + 1, 0), (c, last))
        for g in range(1, ATTN_GROUPS):
            stage((c + 1, g), (c + 1, g - 1))
        return carry

    lax.fori_loop(0, n - 1, body, 0)
    stage(None, (n - 1, last))

    acc = jnp.concatenate([r[...] for r in acc_refs], axis=1)
    ot = acc[:HEAD_DIM] / acc[HEAD_DIM:HEAD_DIM + 1]
    for j in range(Q_GROUP // 2):
        pair = jnp.concatenate([ot[:, (2 * j) * tq:(2 * j + 1) * tq], ot[:, (2 * j + 1) * tq:(2 * j + 2) * tq]],
                               axis=0)
        o_ref[0, :, j * LANES:(j + 1) * LANES] = pair.T.astype(o_ref.dtype)


ATTN_KC = 768


def _attention(qt, k, vt, *, tq=1024):
    nb, _, t = qt.shape
    s_len = k.shape[2]
    tq = min(tq, t)
    kc = max(c for c in range(LANES, ATTN_KC + 1, LANES) if s_len % c == 0)
    gw = Q_GROUP * tq // ATTN_GROUPS
    return pl.pallas_call(
        functools.partial(_attn_kernel, kc=kc),
        grid=(nb, ATTN_KV_HEADS, t // tq),
        in_specs=[
            pl.BlockSpec((1, Q_GROUP * HEAD_DIM, tq), lambda b, g, i: (b, g, i)),
            pl.BlockSpec((1, 1, s_len, HEAD_DIM), lambda b, g, i: (b, g, 0, 0)),
            pl.BlockSpec((1, 1, V_ROWS, s_len), lambda b, g, i: (b, g, 0, 0)),
        ],
        out_specs=pl.BlockSpec((1, tq, Q_GROUP * HEAD_DIM), lambda b, g, i: (b, i, g)),
        out_shape=jax.ShapeDtypeStruct((nb, t, ATTN_Q_W), BF16),
        scratch_shapes=[pltpu.VMEM((HEAD_DIM, Q_GROUP * tq), BF16)] + ATTN_GROUPS * [
            pltpu.VMEM((1, gw), F32),
            pltpu.VMEM((1, gw), F32),
            pltpu.VMEM((V_ROWS, gw), F32),
            pltpu.VMEM((kc, gw), F32)],
        compiler_params=pltpu.CompilerParams(dimension_semantics=("arbitrary", "arbitrary", "arbitrary"),
                                             vmem_limit_bytes=VMEM_LIMIT),
        name="flash_attention",
    )(qt, k, vt)


def _gla_constants():
    c = GLA_CHUNK
    i = np.arange(c)[:, None]
    t = np.arange(c)[None, :]
    fwd = [t > i, t <= i]
    bwd = [t < i, t >= i]
    s = c // 2
    while s >= 1:
        mid = (i // (2 * s)) * (2 * s) + s
        second = (i % (2 * s)) >= s
        fwd.append(np.where(second, (t >= mid) & (t <= i), (t > i) & (t < mid)))
        bwd.append(np.where(second, (t >= mid) & (t < i), (t >= i) & (t < mid)))
        s //= 2
    to = lambda blocks: jnp.asarray(np.concatenate(blocks, axis=0).astype(np.float32), dtype=BF16)
    return to(fwd), to(bwd)


_GLA_LEVELS = int(np.log2(GLA_CHUNK))


def _cum(mat, la):
    hi, lo = _split2(la)
    r = _dot(mat, jnp.concatenate([hi, lo], axis=-1))
    return r[:, :LANES] + r[:, LANES:]


def _gla_kernel(q_ref, k_ref, v_ref, vt_ref, laf_ref, lab_ref, kc_ref, vtc_ref, lafc_ref, labc_ref,
                mf_ref, mb_ref, o_ref, st_ref, dec_ref, *, cpt):
    c = GLA_CHUNK
    n_lat = k_ref.shape[1] // c
    n_ctx = kc_ref.shape[1] // c
    n_all = n_ctx + n_lat
    tile = pl.program_id(2)
    lane = lax.broadcasted_iota(jnp.int32, (c, LANES), 1)
    head_masks = [lane < GLA_DK, lane >= GLA_DK]
    lane2 = lax.broadcasted_iota(jnp.int32, (c, 2 * LANES), 1) % LANES
    pair_masks = [lane2 < GLA_DK, lane2 >= GLA_DK]

    def increments(k_r, vt_r, laf_r, lab_r, n0, slot0, count):
        offs = [pl.multiple_of((n0 + i) * c, c) for i in range(count)]
        rfs = [_cum(mf_ref[0:2 * c, :], laf_r[0, pl.ds(off, c), :]) for off in offs]
        rbs = [_cum(mb_ref[0:2 * c, :], lab_r[0, pl.ds(off, c), :]) for off in offs]
        for i, (off, rf, rb) in enumerate(zip(offs, rfs, rbs)):
            k = k_r[0, pl.ds(off, c), :].astype(F32)
            kfb = jnp.concatenate([k * jnp.exp(rf[0:c]),
                                   k * jnp.exp(rb[0:c])], axis=-1)
            kk = jnp.concatenate([jnp.where(pair_masks[0], kfb, 0.0), jnp.where(pair_masks[1], kfb, 0.0)],
                                 axis=0).astype(BF16)
            vt2 = jnp.concatenate([vt_r[0, 0:GLA_DV, pl.ds(off, c)], vt_r[0, GLA_DV:2 * GLA_DV, pl.ds(off, c)]],
                                  axis=-1)
            st_ref[slot0 + i] = _dot(vt2, kk)
            dec_ref[slot0 + i, 0:1, :] = jnp.exp(rf[2 * c - 1:2 * c, :])
            dec_ref[slot0 + i, 1:2, :] = jnp.exp(rb[c:c + 1, :])

    def group_size(n):
        return max(g for g in (4, 2, 1) if n % g == 0)

    @pl.when(tile == 0)
    def _():
        gc, gl = group_size(n_ctx), group_size(n_lat)

        def ctx_body(i, carry):
            increments(kc_ref, vtc_ref, lafc_ref, labc_ref, i * gc, i * gc, gc)
            return carry
        lax.fori_loop(0, n_ctx // gc, ctx_body, 0)

        def lat_body(i, carry):
            increments(k_ref, vt_ref, laf_ref, lab_ref, i * gl, n_ctx + i * gl, gl)
            return carry
        lax.fori_loop(0, n_lat // gl, lat_body, 0)

        def fwd_body(s, st):
            inc = st_ref[s, :, 0:LANES]
            st_ref[s, :, 0:LANES] = st
            return st * dec_ref[s, 0:1, :] + inc
        lax.fori_loop(0, n_all, fwd_body, jnp.zeros((GLA_DV, LANES), F32))

        def bwd_body(j, st, base, count):
            s = base + count - 1 - j
            inc = st_ref[s, :, LANES:2 * LANES]
            st_ref[s, :, LANES:2 * LANES] = st
            return st * dec_ref[s, 1:2, :] + inc
        st = lax.fori_loop(0, n_ctx, functools.partial(bwd_body, base=0, count=n_ctx),
                           jnp.zeros((GLA_DV, LANES), F32))
        lax.fori_loop(0, n_lat, functools.partial(bwd_body, base=n_ctx, count=n_lat), st)

    xor2 = lax.broadcasted_iota(jnp.int32, (c, 2 * c), 0) ^ (lax.broadcasted_iota(jnp.int32, (c, 2 * c), 1) & (c - 1))
    row_l = lax.broadcasted_iota(jnp.int32, (c, LANES), 0)
    lane2v = lax.broadcasted_iota(jnp.int32, (c, 2 * GLA_DV), 1)
    lane2s = lax.broadcasted_iota(jnp.int32, (GLA_DV, 2 * LANES), 1) % LANES

    def stack_heads(x):
        return jnp.concatenate([jnp.where(head_masks[0], x, 0.0), jnp.where(head_masks[1], x, 0.0)], axis=0)

    go = 2 if cpt % 2 == 0 else 1

    def out_body(i, carry):
        idx = range(go)
        ns = [tile * cpt + i * go + u for u in idx]
        offs = [pl.multiple_of(n * c, c) for n in ns]
        locs = [pl.multiple_of((i * go + u) * c, c) for u in idx]
        rfs = [_cum(mf_ref[c:, :], laf_ref[0, pl.ds(off, c), :]) for off in offs]
        rbs = [_cum(mb_ref[c:, :], lab_ref[0, pl.ds(off, c), :]) for off in offs]
        qs = [q_ref[0, pl.ds(loc, c), :].astype(F32) for loc in locs]
        ks = [k_ref[0, pl.ds(off, c), :].astype(F32) for off in offs]
        a = [2.0 * _dot_nt(qs[u].astype(BF16), stack_heads(ks[u]).astype(BF16)) for u in idx]
        for lvl in range(_GLA_LEVELS):
            sh = _GLA_LEVELS - 1 - lvl
            second = ((row_l >> sh) & 1) == 1
            for u in idx:
                ef = jnp.exp(rfs[u][(1 + lvl) * c:(2 + lvl) * c])
                eb = jnp.exp(rbs[u][(1 + lvl) * c:(2 + lvl) * c])
                ql = (qs[u] * jnp.where(second, ef, eb)).astype(BF16)
                kl = stack_heads(ks[u] * jnp.where(second, eb, ef)).astype(BF16)
                a[u] = jnp.where((xor2 >> sh) == 1, _dot_nt(ql, kl), a[u])
        for u in idx:
            v = v_ref[0, pl.ds(locs[u], c), :]
            v_bd = jnp.concatenate([jnp.where(lane2v < GLA_DV, v, jnp.zeros_like(v)),
                                    jnp.where(lane2v >= GLA_DV, v, jnp.zeros_like(v))], axis=0)
            q_inter = jnp.concatenate([qs[u] * jnp.exp(rfs[u][0:c]), qs[u] * jnp.exp(rbs[u][0:c])],
                                      axis=-1).astype(BF16)
            states = st_ref[n_ctx + ns[u]]
            st2 = jnp.concatenate([jnp.where(lane2s < GLA_DK, states, 0.0),
                                   jnp.where(lane2s >= GLA_DK, states, 0.0)], axis=0).astype(BF16)
            o_ref[0, pl.ds(locs[u], c), :] = _dot(a[u].astype(BF16), v_bd) + _dot_nt(q_inter, st2)
        return carry

    lax.fori_loop(0, cpt // go, out_body, 0)


def _gla(gq, gk, gv, gvt, laf, lab, gk_c, gvt_c, laf_c, lab_c, mf, mb, *, tile=1024):
    nb, t, _ = gq.shape
    tc = gk_c.shape[1] // nb
    tile = min(tile, t)
    c = GLA_CHUNK
    n_all = (t + tc) // c
    pair_k = 2 * GLA_DK
    pair_v = 2 * GLA_DV
    return pl.pallas_call(
        functools.partial(_gla_kernel, cpt=tile // c),
        grid=(nb, GLA_HEADS // 2, t // tile),
        in_specs=[
            pl.BlockSpec((1, tile, pair_k), lambda b, p, i: (b, i, p)),
            pl.BlockSpec((1, t, pair_k), lambda b, p, i: (b, 0, p)),
            pl.BlockSpec((1, tile, pair_v), lambda b, p, i: (b, i, p)),
            pl.BlockSpec((1, pair_v, t), lambda b, p, i: (b, p, 0)),
            pl.BlockSpec((1, t, pair_k), lambda b, p, i: (b, 0, p)),
            pl.BlockSpec((1, t, pair_k), lambda b, p, i: (b, 0, p)),
            pl.BlockSpec((1, tc, pair_k), lambda b, p, i: (0, b, p)),
            pl.BlockSpec((1, pair_v, tc), lambda b, p, i: (0, p, b)),
            pl.BlockSpec((1, tc, pair_k), lambda b, p, i: (0, b, p)),
            pl.BlockSpec((1, tc, pair_k), lambda b, p, i: (0, b, p)),
            _const_spec(tuple(mf.shape)), _const_spec(tuple(mb.shape)),
        ],
        out_specs=pl.BlockSpec((1, tile, pair_v), lambda b, p, i: (b, i, p)),
        out_shape=jax.ShapeDtypeStruct((nb, t, GLA_V_W), F32),
        scratch_shapes=[pltpu.VMEM((n_all, GLA_DV, 2 * LANES), F32),
                        pltpu.VMEM((n_all, 8, LANES), F32)],
        compiler_params=pltpu.CompilerParams(dimension_semantics=("arbitrary", "arbitrary", "arbitrary"),
                                             vmem_limit_bytes=VMEM_LIMIT),
        name="gla",
    )(gq, gk, gv, gvt, laf, lab, gk_c, gvt_c, laf_c, lab_c, mf, mb)


def _rope_tables(t):
    rows = t // GRID_W
    row = jnp.repeat(jnp.arange(rows, dtype=F32), GRID_W)
    col = jnp.tile(jnp.arange(GRID_W, dtype=F32), rows)
    freqs = ROPE_THETA ** (-jnp.arange(0, ROPE_AXIS_DIM, 2, dtype=F32) / ROPE_AXIS_DIM)
    ang = jnp.concatenate([row[:, None] * freqs, col[:, None] * freqs], axis=-1)
    cos, sin = jnp.cos(ang), jnp.sin(ang)
    reps = LANES // HEAD_DIM
    return (jnp.tile(jnp.concatenate([cos, cos], axis=-1), (1, reps)),
            jnp.tile(jnp.concatenate([-sin, sin], axis=-1), (1, reps)))


def kernel(x, c, ctx, c_ctx, w_mod, b_mod, g_norm, w_ffn_up, w_ffn_down, w_in, g_q, g_k,
           w_decay, b_decay, g_gla, w_branch, b_gate, w_out, g_final):
    nb, t, d = x.shape
    tc = ctx.shape[1]
    f = w_ffn_down.shape[2]
    assert w_mod.shape[0] == 1, "single layer"
    assert t % GLA_CHUNK == 0 and tc % GLA_CHUNK == 0 and t % GRID_W == 0

    rows = -(-(nb + 1) // 8) * 8
    c_rows = jnp.zeros((rows, d), F32).at[:nb].set(c).at[nb].set(c_ctx)
    m = _modulation(c_rows, w_mod[0], b_mod[0]).reshape(rows, N_MOD, d)
    m_lat, m_ctx = m[:nb], m[nb:nb + 1]

    wa = [w_ffn_up[0, i, :, :f].astype(BF16) for i in range(2)]
    wb = [w_ffn_up[0, i, :, f:].astype(BF16) for i in range(2)]
    wd = [w_ffn_down[0, i].astype(BF16) for i in range(2)]
    c_low = _C_GT
    w_low = w_in[0][:, c_low:c_low + 2 * GLA_RANK]
    w_r = jnp.concatenate([w_in[0][:, :c_low], w_in[0][:, c_low + 2 * GLA_RANK:], w_low,
                           jnp.zeros((d, LANES - 2 * GLA_RANK), F32)], axis=-1).astype(BF16)
    w_bd = jnp.zeros((2 * GLA_RANK, 2 * GLA_K_W), F32)
    w_bd = w_bd.at[:GLA_RANK, :GLA_K_W].set(w_decay[0, 0]).at[GLA_RANK:, GLA_K_W:].set(w_decay[0, 1])
    w_bd_hi = w_bd.astype(BF16)
    w_bd_lo = (w_bd - w_bd_hi.astype(F32)).astype(BF16)
    w_dec = jnp.concatenate([w_bd_hi, w_bd_hi, w_bd_lo, jnp.zeros_like(w_bd_hi)], axis=0)
    b_dec = b_decay[0].reshape(1, 2 * GLA_K_W)
    gqk = jnp.concatenate([jnp.tile(g_q[0], ATTN_HEADS), jnp.tile(g_k[0], ATTN_KV_HEADS)]).reshape(1, -1)
    lane = np.arange(LANES)
    seg = jnp.asarray(np.tile((lane[:, None] // HEAD_DIM == lane[None, :] // HEAD_DIM) / HEAD_DIM, (2, 1)), dtype=BF16)
    cos_t, sin_t = _rope_tables(t)
    ones_t, zeros_t = jnp.ones((nb * tc, LANES), F32), jnp.zeros((nb * tc, LANES), F32)
    mf, mb = _gla_constants()

    proj = functools.partial(_in_proj, g=g_norm[0, 1], w_r=w_r, gqk=gqk, seg=seg, w_dec=w_dec,
                             b_dec=b_dec, b_gate=b_gate[0].reshape(1, 2 * d))

    hc = _half_ffn(ctx.reshape(1, nb * tc, d), m_ctx[:, 0:3], g_norm[0, 0], wa[0], wb[0], wd[0])
    pc = proj(hc, m_ctx[:, 3:5], cos_t=ones_t, sin_t=zeros_t)
    _, k_c, vt_c, _, gk_c, _, gvt_c, laf_c, lab_c, _, _ = pc

    h1 = _half_ffn(x, m_lat[:, 0:3], g_norm[0, 0], wa[0], wb[0], wd[0])
    qt, k, vt, gq, gk, gv, gvt, laf, lab, gs, gt = proj(h1, m_lat[:, 3:5], cos_t=cos_t, sin_t=sin_t)
    k_all = jnp.concatenate(
        [k, k_c.reshape(ATTN_KV_HEADS, nb, tc, HEAD_DIM).transpose(1, 0, 2, 3)], axis=2)
    vt_all = jnp.concatenate(
        [vt, vt_c.reshape(ATTN_KV_HEADS, V_ROWS, nb, tc).transpose(2, 0, 1, 3)], axis=3)
    attn_o = _attention(qt, k_all, vt_all)
    gla_o = _gla(gq, gk, gv, gvt, laf, lab, gk_c, gvt_c, laf_c, lab_c, mf, mb)
    return _merge_ffn(h1, attn_o, gla_o, gs, gt, m_lat[:, 5:6], g_gla[0],
                      w_branch[0, 0].astype(BF16), w_branch[0, 1].astype(BF16), w_out[0].astype(BF16),
                      m_lat[:, 6:9], g_norm[0, 2], wa[1], wb[1], wd[1], g_final)
```

```python
import functools

import numpy as np
import jax
import jax.numpy as jnp
from jax import lax
from jax.experimental import pallas as pl
from jax.experimental.pallas import tpu as pltpu

F32 = jnp.float32
BF16 = jnp.bfloat16

EPS = 1e-6
GRID_W = 64
N_MOD = 9
ATTN_HEADS = 8
ATTN_KV_HEADS = 2
HEAD_DIM = 64
ROPE_AXIS_DIM = HEAD_DIM // 2
ROPE_THETA = 10000.0
GLA_HEADS = 4
GLA_DK = 64
GLA_DV = 128
GLA_RANK = 16
GLA_GATE_NORM = 16.0
ATTN_Q_W = ATTN_HEADS * HEAD_DIM
ATTN_KV_W = ATTN_KV_HEADS * HEAD_DIM
GLA_K_W = GLA_HEADS * GLA_DK
GLA_V_W = GLA_HEADS * GLA_DV
Q_GROUP = ATTN_HEADS // ATTN_KV_HEADS

LANES = 128
V_ROWS = 80
GLA_CHUNK = 128
VMEM_LIMIT = 56 * 1024 * 1024


def _dot(a, b):
    return jnp.dot(a, b, preferred_element_type=F32)


def _dot_nt(a, b):
    return lax.dot_general(a, b, (((1,), (1,)), ((), ())), preferred_element_type=F32)


def _sigmoid(x):
    return 1.0 / (1.0 + jnp.exp(-x))


def _split2(x):
    hi = x.astype(BF16)
    lo = (x - hi.astype(F32)).astype(BF16)
    return hi, lo


def _rms(x):
    return x * lax.rsqrt(jnp.mean(x * x, axis=-1, keepdims=True) + EPS)


def _const_spec(shape):
    nd = len(shape)
    return pl.BlockSpec(shape, lambda *_: (0,) * nd, pipeline_mode=pl.Buffered(1))


def _mod_kernel(c_ref, w_ref, b_ref, o_ref):
    c = c_ref[...]
    s_hi, s_lo = _split2(c * _sigmoid(c))
    w_hi, w_lo = _split2(w_ref[...])
    o_ref[...] = _dot(s_hi, w_hi) + _dot(s_hi, w_lo) + _dot(s_lo, w_hi) + b_ref[...]


def _modulation(c_rows, w_mod, b_mod):
    rows, d = c_rows.shape
    n = w_mod.shape[1]
    tn = 1024
    return pl.pallas_call(
        _mod_kernel,
        grid=(n // tn,),
        in_specs=[pl.BlockSpec((rows, d), lambda j: (0, 0)),
                  pl.BlockSpec((d, tn), lambda j: (0, j)),
                  pl.BlockSpec((1, tn), lambda j: (0, j))],
        out_specs=pl.BlockSpec((rows, tn), lambda j: (0, j)),
        out_shape=jax.ShapeDtypeStruct((rows, n), F32),
        compiler_params=pltpu.CompilerParams(dimension_semantics=("arbitrary",),
                                             vmem_limit_bytes=VMEM_LIMIT),
        name="modulation",
    )(c_rows, w_mod, b_mod.reshape(1, n))


def _ffn_math(x, mod_ref, g_ref, wa_ref, wb_ref, wd_ref, n_chunks):
    shift, scale, gate = mod_ref[0, 0:1, :], mod_ref[0, 1:2, :], mod_ref[0, 2:3, :]
    n = ((_rms(x) * g_ref[...]) * (1.0 + scale) + shift).astype(BF16)
    fc = wa_ref.shape[1] // n_chunks
    acc = None
    for c in range(n_chunks):
        a = _dot(n, wa_ref[:, c * fc:(c + 1) * fc])
        b = _dot(n, wb_ref[:, c * fc:(c + 1) * fc])
        act = ((a * _sigmoid(a)) * b).astype(BF16)
        part = _dot(act, wd_ref[c * fc:(c + 1) * fc, :])
        acc = part if acc is None else acc + part
    return x + 0.5 * gate * acc


def _ffn_kernel(h_ref, mod_ref, g_ref, wa_ref, wb_ref, wd_ref, o_ref, *, n_chunks):
    o_ref[0] = _ffn_math(h_ref[0], mod_ref, g_ref, wa_ref, wb_ref, wd_ref, n_chunks)


def _merge_ffn_kernel(h_ref, ao_ref, go_ref, gs_ref, gt_ref, mg_ref, gg_ref, wb0_ref, wb1_ref, wo_ref,
                      mod_ref, g_ref, wa_ref, wb_ref, wd_ref, gf_ref, o_ref, *, n_chunks):
    d = h_ref.shape[2]
    go = go_ref[0]
    normed = [_rms(go[:, h * GLA_DV:(h + 1) * GLA_DV]) * gg_ref[...] for h in range(GLA_HEADS)]
    gn = (jnp.concatenate(normed, axis=-1) * gs_ref[0].astype(F32)).astype(BF16)
    y_attn = _dot(ao_ref[0], wb0_ref[...])
    y_gla = _dot(gn, wb1_ref[...])
    gt = gt_ref[0].astype(F32)
    z = (gt[:, :d] * y_attn + gt[:, d:] * y_gla).astype(BF16)
    h2 = h_ref[0] + mg_ref[0] * _dot(z, wo_ref[...])
    out = _ffn_math(h2, mod_ref, g_ref, wa_ref, wb_ref, wd_ref, n_chunks)
    o_ref[0] = _rms(out) * gf_ref[...]


def _ffn_weight_specs(d, f):
    return [_const_spec((1, d)), _const_spec((d, f)), _const_spec((d, f)), _const_spec((f, d))]


def _half_ffn(h, mod3, g, wa, wb, wd, *, tm=512, n_chunks=1):
    nb, t, d = h.shape
    f = wa.shape[1]
    tm = min(tm, t)
    return pl.pallas_call(
        functools.partial(_ffn_kernel, n_chunks=n_chunks),
        grid=(nb, t // tm),
        in_specs=[pl.BlockSpec((1, tm, d), lambda b, i: (b, i, 0)),
                  pl.BlockSpec((1, 3, d), lambda b, i: (b, 0, 0))] + _ffn_weight_specs(d, f),
        out_specs=pl.BlockSpec((1, tm, d), lambda b, i: (b, i, 0)),
        out_shape=jax.ShapeDtypeStruct((nb, t, d), F32),
        compiler_params=pltpu.CompilerParams(dimension_semantics=("arbitrary", "arbitrary"),
                                             vmem_limit_bytes=VMEM_LIMIT),
        name="half_ffn",
    )(h, mod3, g.reshape(1, d), wa, wb, wd)


def _merge_ffn(h, attn_o, gla_o, gs, gt, m_gate, g_gla, wb0, wb1, wo, mod3, g, wa, wb, wd, g_final,
               *, tm=512, n_chunks=1):
    nb, t, d = h.shape
    f = wa.shape[1]
    tm = min(tm, t)
    tok = lambda w: pl.BlockSpec((1, tm, w), lambda b, i: (b, i, 0))
    return pl.pallas_call(
        functools.partial(_merge_ffn_kernel, n_chunks=n_chunks),
        grid=(nb, t // tm),
        in_specs=[tok(d), tok(ATTN_Q_W), tok(GLA_V_W), tok(GLA_V_W), tok(2 * d),
                  pl.BlockSpec((1, 1, d), lambda b, i: (b, 0, 0)),
                  _const_spec((1, GLA_DV)), _const_spec((ATTN_Q_W, d)), _const_spec((GLA_V_W, d)),
                  _const_spec((d, d)),
                  pl.BlockSpec((1, 3, d), lambda b, i: (b, 0, 0))] + _ffn_weight_specs(d, f)
                 + [_const_spec((1, d))],
        out_specs=tok(d),
        out_shape=jax.ShapeDtypeStruct((nb, t, d), F32),
        compiler_params=pltpu.CompilerParams(dimension_semantics=("arbitrary", "arbitrary"),
                                             vmem_limit_bytes=VMEM_LIMIT),
        name="merge_ffn_final",
    )(h, attn_o, gla_o, gs, gt, m_gate, g_gla.reshape(1, GLA_DV), wb0, wb1, wo,
      mod3, g.reshape(1, d), wa, wb, wd, g_final.reshape(1, d))


_C_Q = 0
_C_K = _C_Q + ATTN_Q_W
_C_V = _C_K + ATTN_KV_W
_C_GQ = _C_V + ATTN_KV_W
_C_GK = _C_GQ + GLA_K_W
_C_GV = _C_GK + GLA_K_W
_C_GS = _C_GV + GLA_V_W
_C_GT = _C_GS + GLA_V_W


def _proj_kernel(h_ref, mod_ref, g_ref, w_ref, gqk_ref, seg_ref, cos_ref, sin_ref, wdec_ref,
                 bdec_ref, bgate_ref,
                 qt_ref, k_ref, vt_ref, gq_ref, gk_ref, gv_ref, gvt_ref, laf_ref, lab_ref, gs_ref, gt_ref):
    d = h_ref.shape[2]
    c_low = _C_GT + 2 * d
    x = h_ref[0]
    shift, scale = mod_ref[0, 0:1, :], mod_ref[0, 1:2, :]
    n = ((_rms(x) * g_ref[...]) * (1.0 + scale) + shift).astype(BF16)
    tm = x.shape[0]

    qk = _dot(n, w_ref[:, _C_Q:_C_V])
    low = _dot(n, w_ref[:, c_low:c_low + LANES])
    gv = _dot(n, w_ref[:, _C_GV:_C_GS])

    seg = seg_ref[...]
    ms = []
    for j in range((ATTN_Q_W + ATTN_KV_W) // LANES):
        sq = qk[:, j * LANES:(j + 1) * LANES]
        hi, lo = _split2(sq * sq)
        ms.append(_dot(jnp.concatenate([hi, lo], axis=-1), seg))
    ms = jnp.concatenate(ms, axis=-1)
    gqk = _dot(n, w_ref[:, _C_GQ:_C_GV])

    l_hi = low.astype(BF16).astype(F32)
    l_lo = low - l_hi
    packed = l_hi + pltpu.roll(l_lo, 2 * GLA_RANK, 1) + pltpu.roll(l_hi, 4 * GLA_RANK, 1)
    z = _dot(packed.astype(BF16), wdec_ref[...]) + bdec_ref[...]
    vt = _dot(n, w_ref[:, _C_V:_C_GQ]).T
    gs = _dot(n, w_ref[:, _C_GS:_C_GT])
    gt = _dot(n, w_ref[:, _C_GT:c_low]) + bgate_ref[...]

    qk = qk * lax.rsqrt(ms + EPS) * gqk_ref[...]
    lane = lax.broadcasted_iota(jnp.int32, (tm, LANES), 1)
    first = (lane % HEAD_DIM) < (HEAD_DIM // 2)
    cos, sin = cos_ref[...], sin_ref[...]
    rot = []
    for j in range((ATTN_Q_W + ATTN_KV_W) // LANES):
        xs = qk[:, j * LANES:(j + 1) * LANES]
        other = jnp.where(first, pltpu.roll(xs, LANES - HEAD_DIM // 2, 1), pltpu.roll(xs, HEAD_DIM // 2, 1))
        rot.append(xs * cos + other * sin)
    q_scale = HEAD_DIM ** -0.5 * float(np.log2(np.e))
    for j in range(ATTN_Q_W // LANES):
        qt_ref[0, j * LANES:(j + 1) * LANES, :] = (rot[j] * q_scale).T.astype(BF16)
    k_rot = rot[ATTN_Q_W // LANES]
    for g in range(ATTN_KV_HEADS):
        k_ref[0, g] = k_rot[:, g * HEAD_DIM:(g + 1) * HEAD_DIM].astype(BF16)

    tail = jnp.where(lax.broadcasted_iota(jnp.int32, (V_ROWS - HEAD_DIM, tm), 0) == 0, 1.0, 0.0)
    for g in range(ATTN_KV_HEADS):
        vt_ref[0, g] = jnp.concatenate([vt[g * HEAD_DIM:(g + 1) * HEAD_DIM], tail], axis=0).astype(BF16)

    gq_ref[0] = (gqk[:, :GLA_K_W] * (GLA_DK ** -0.5)).astype(BF16)
    gk_ref[0] = gqk[:, GLA_K_W:].astype(BF16)
    gv_ref[0] = gv.astype(BF16)
    gvt_ref[0] = gv.T.astype(BF16)

    la = (jnp.minimum(z, 0.0) - jnp.log(1.0 + jnp.exp(-jnp.abs(z)))) * (1.0 / GLA_GATE_NORM)
    laf_ref[0] = la[:, :GLA_K_W]
    lab_ref[0] = la[:, GLA_K_W:]

    gs_ref[0] = (gs * _sigmoid(gs)).astype(BF16)
    gt_ref[0] = _sigmoid(gt).astype(BF16)


def _in_proj(h, mod2, g, w_r, gqk, seg, cos_t, sin_t, w_dec, b_dec, b_gate, *, tm=512):
    nb, t, d = h.shape
    tm = min(tm, t)
    wp = w_r.shape[1]
    tok = lambda w: pl.BlockSpec((1, tm, w), lambda b, i: (b, i, 0))
    out_shape = [
        jax.ShapeDtypeStruct((nb, ATTN_Q_W, t), BF16),
        jax.ShapeDtypeStruct((nb, ATTN_KV_HEADS, t, HEAD_DIM), BF16),
        jax.ShapeDtypeStruct((nb, ATTN_KV_HEADS, V_ROWS, t), BF16),
        jax.ShapeDtypeStruct((nb, t, GLA_K_W), BF16),
        jax.ShapeDtypeStruct((nb, t, GLA_K_W), BF16),
        jax.ShapeDtypeStruct((nb, t, GLA_V_W), BF16),
        jax.ShapeDtypeStruct((nb, GLA_V_W, t), BF16),
        jax.ShapeDtypeStruct((nb, t, GLA_K_W), F32),
        jax.ShapeDtypeStruct((nb, t, GLA_K_W), F32),
        jax.ShapeDtypeStruct((nb, t, GLA_V_W), BF16),
        jax.ShapeDtypeStruct((nb, t, 2 * d), BF16),
    ]
    out_specs = [
        pl.BlockSpec((1, ATTN_Q_W, tm), lambda b, i: (b, 0, i)),
        pl.BlockSpec((1, ATTN_KV_HEADS, tm, HEAD_DIM), lambda b, i: (b, 0, i, 0)),
        pl.BlockSpec((1, ATTN_KV_HEADS, V_ROWS, tm), lambda b, i: (b, 0, 0, i)),
        tok(GLA_K_W), tok(GLA_K_W), tok(GLA_V_W),
        pl.BlockSpec((1, GLA_V_W, tm), lambda b, i: (b, 0, i)),
        tok(GLA_K_W), tok(GLA_K_W), tok(GLA_V_W), tok(2 * d),
    ]
    in_specs = [
        pl.BlockSpec((1, tm, d), lambda b, i: (b, i, 0)),
        pl.BlockSpec((1, 2, d), lambda b, i: (b, 0, 0)),
        _const_spec((1, d)), _const_spec((d, wp)), _const_spec((1, ATTN_Q_W + ATTN_KV_W)),
        _const_spec((2 * LANES, LANES)),
        pl.BlockSpec((tm, LANES), lambda b, i: (i, 0)),
        pl.BlockSpec((tm, LANES), lambda b, i: (i, 0)),
        _const_spec((LANES, 2 * GLA_K_W)),
        _const_spec((1, 2 * GLA_K_W)), _const_spec((1, 2 * d)),
    ]
    return pl.pallas_call(
        _proj_kernel,
        grid=(nb, t // tm),
        in_specs=in_specs,
        out_specs=out_specs,
        out_shape=out_shape,
        compiler_params=pltpu.CompilerParams(dimension_semantics=("arbitrary", "arbitrary"),
                                             vmem_limit_bytes=VMEM_LIMIT),
        name="in_proj",
    )(h, mod2, g.reshape(1, d), w_r, gqk, seg, cos_t, sin_t, w_dec, b_dec, b_gate)


ATTN_GROUPS = 2
ATTN_ROW_BLOCK = 256
ATTN_KC = 1024
ATTN_UNROLL = 3


def _attn_plan(t, tc):
    if tc % (2 * LANES) == 0:
        kc = max(c for c in range(LANES, ATTN_KC + 1, LANES) if t % c == 0)
        edge = tc // 2
        chunks = [(t, edge)] + [(i * kc, kc) for i in range(t // kc)] + [(t + edge, edge)]
        return chunks, (1, len(chunks) - 1)
    s_len = t + tc
    kc = max(c for c in range(LANES, ATTN_KC + 1, LANES) if s_len % c == 0)
    chunks = [(i * kc, kc) for i in range(s_len // kc)]
    return chunks, (0, len(chunks))


def _attn_kernel(qt_ref, k_ref, vt_ref, o_ref, qs_ref, *group_refs, chunks, run):
    tq = qt_ref.shape[2]
    gw = Q_GROUP * tq // ATTN_GROUPS
    per = len(group_refs) // ATTN_GROUPS
    m_refs, cmax_refs, acc_refs, s_refs = (
        [group_refs[g * per + j] for g in range(ATTN_GROUPS)] for j in range(per))
    for h in range(Q_GROUP):
        qs_ref[:, h * tq:(h + 1) * tq] = qt_ref[0, h * HEAD_DIM:(h + 1) * HEAD_DIM, :]
    for g in range(ATTN_GROUPS):
        m_refs[g][...] = jnp.full(m_refs[g].shape, -jnp.inf, F32)
        acc_refs[g][...] = jnp.zeros(acc_refs[g].shape, F32)

    def rows(off, j, rb):
        start = off + j * rb
        return pl.ds(start if isinstance(start, int) else pl.multiple_of(start, LANES), rb)

    def stage(score, apply):
        n_s = n_a = 0
        if score is not None:
            off_s, size_s, gs = score
            rb_s = min(ATTN_ROW_BLOCK, size_s)
            n_s = size_s // rb_s
        if apply is not None:
            off_a, size_a, ga = apply
            rb_a = min(ATTN_ROW_BLOCK, size_a)
            n_a = size_a // rb_a
            m_old = m_refs[ga][...]
            m_new = jnp.maximum(m_old, cmax_refs[ga][...])
            m_refs[ga][...] = m_new
        cmax = pv_sum = None
        for j in range(max(n_s, n_a)):
            if j < n_a:
                p = jnp.exp2(s_refs[ga][j * rb_a:(j + 1) * rb_a, :] - m_new).astype(BF16)
            if j < n_s:
                s = _dot(k_ref[0, 0, rows(off_s, j, rb_s), :], qs_ref[:, gs * gw:(gs + 1) * gw])
                s_refs[gs][j * rb_s:(j + 1) * rb_s, :] = s
                bmax = jnp.max(s, axis=0, keepdims=True)
                cmax = bmax if cmax is None else jnp.maximum(cmax, bmax)
            if j < n_a:
                part = _dot(vt_ref[0, 0, :, rows(off_a, j, rb_a)], p)
                pv_sum = part if pv_sum is None else pv_sum + part
        if score is not None:
            cmax_refs[gs][...] = cmax
        if apply is not None:
            acc_refs[ga][...] = jnp.exp2(m_old - m_new) * acc_refs[ga][...] + pv_sum

    last = ATTN_GROUPS - 1

    def chunk_stages(cur, prev):
        stage(cur + (0,), None if prev is None else prev + (last,))
        for g in range(1, ATTN_GROUPS):
            stage(cur + (g,), cur + (g - 1,))

    a, b = run
    trips = b - a - 1
    unroll = ATTN_UNROLL if trips >= ATTN_UNROLL else 1
    first_loop = a + 1 + trips % unroll
    for i in range(first_loop):
        chunk_stages(chunks[i], chunks[i - 1] if i else None)
    if b > first_loop:
        off0, kc = chunks[first_loop]

        def body(i, carry):
            for u in range(unroll):
                off = off0 + (i * unroll + u) * kc
                chunk_stages((off, kc), (off - kc, kc))
            return carry

        lax.fori_loop(0, (b - first_loop) // unroll, body, 0)
    for i in range(b, len(chunks)):
        chunk_stages(chunks[i], chunks[i - 1])
    stage(None, chunks[-1] + (last,))

    acc = jnp.concatenate([r[...] for r in acc_refs], axis=1)
    ot = acc[:HEAD_DIM] / acc[HEAD_DIM:HEAD_DIM + 1]
    for j in range(Q_GROUP // 2):
        pair = jnp.concatenate([ot[:, (2 * j) * tq:(2 * j + 1) * tq], ot[:, (2 * j + 1) * tq:(2 * j + 2) * tq]],
                               axis=0)
        o_ref[0, :, j * LANES:(j + 1) * LANES] = pair.T.astype(o_ref.dtype)


def _attention(qt, k, vt, t_ctx, *, tq=512):
    nb, _, t = qt.shape
    s_len = k.shape[2]
    tq = min(tq, t)
    chunks, run = _attn_plan(t, t_ctx)
    kc = max(size for _, size in chunks)
    gw = Q_GROUP * tq // ATTN_GROUPS
    return pl.pallas_call(
        functools.partial(_attn_kernel, chunks=tuple(chunks), run=run),
        grid=(nb, ATTN_KV_HEADS, t // tq),
        in_specs=[
            pl.BlockSpec((1, Q_GROUP * HEAD_DIM, tq), lambda b, g, i: (b, g, i)),
            pl.BlockSpec((1, 1, s_len, HEAD_DIM), lambda b, g, i: (b, g, 0, 0)),
            pl.BlockSpec((1, 1, V_ROWS, s_len), lambda b, g, i: (b, g, 0, 0)),
        ],
        out_specs=pl.BlockSpec((1, tq, Q_GROUP * HEAD_DIM), lambda b, g, i: (b, i, g)),
        out_shape=jax.ShapeDtypeStruct((nb, t, ATTN_Q_W), BF16),
        scratch_shapes=[pltpu.VMEM((HEAD_DIM, Q_GROUP * tq), BF16)] + ATTN_GROUPS * [
            pltpu.VMEM((1, gw), F32),
            pltpu.VMEM((1, gw), F32),
            pltpu.VMEM((V_ROWS, gw), F32),
            pltpu.VMEM((kc, gw), F32)],
        compiler_params=pltpu.CompilerParams(dimension_semantics=("arbitrary", "arbitrary", "arbitrary"),
                                             vmem_limit_bytes=VMEM_LIMIT),
        name="flash_attention",
    )(qt, k, vt)


def _gla_constants():
    c = GLA_CHUNK
    i = np.arange(c)[:, None]
    t = np.arange(c)[None, :]
    fwd = [t > i, t <= i]
    bwd = [t < i, t >= i]
    s = c // 2
    while s >= 1:
        mid = (i // (2 * s)) * (2 * s) + s
        second = (i % (2 * s)) >= s
        fwd.append(np.where(second, (t >= mid) & (t <= i), (t > i) & (t < mid)))
        bwd.append(np.where(second, (t >= mid) & (t < i), (t >= i) & (t < mid)))
        s //= 2
    to = lambda blocks: jnp.asarray(np.concatenate(blocks, axis=0).astype(np.float32), dtype=BF16)
    return to(fwd), to(bwd)


_GLA_LEVELS = int(np.log2(GLA_CHUNK))


def _cum(mat, la):
    hi, lo = _split2(la)
    r = _dot(mat, jnp.concatenate([hi, lo], axis=-1))
    return r[:, :LANES] + r[:, LANES:]


def _gla_kernel(q_ref, k_ref, v_ref, vt_ref, laf_ref, lab_ref, kc_ref, vtc_ref, lafc_ref, labc_ref,
                mf_ref, mb_ref, o_ref, st_ref, dec_ref, *, cpt):
    c = GLA_CHUNK
    n_lat = k_ref.shape[1] // c
    n_ctx = kc_ref.shape[1] // c
    n_all = n_ctx + n_lat
    tile = pl.program_id(2)
    lane = lax.broadcasted_iota(jnp.int32, (c, LANES), 1)
    head_masks = [lane < GLA_DK, lane >= GLA_DK]
    lane2 = lax.broadcasted_iota(jnp.int32, (c, 2 * LANES), 1) % LANES
    pair_masks = [lane2 < GLA_DK, lane2 >= GLA_DK]

    def increments(k_r, vt_r, laf_r, lab_r, n0, slot0, count):
        offs = [pl.multiple_of((n0 + i) * c, c) for i in range(count)]
        rfs = [_cum(mf_ref[0:2 * c, :], laf_r[0, pl.ds(off, c), :]) for off in offs]
        rbs = [_cum(mb_ref[0:2 * c, :], lab_r[0, pl.ds(off, c), :]) for off in offs]
        for i, (off, rf, rb) in enumerate(zip(offs, rfs, rbs)):
            k = k_r[0, pl.ds(off, c), :].astype(F32)
            kfb = jnp.concatenate([k * jnp.exp(rf[0:c]),
                                   k * jnp.exp(rb[0:c])], axis=-1)
            kk = jnp.concatenate([jnp.where(pair_masks[0], kfb, 0.0), jnp.where(pair_masks[1], kfb, 0.0)],
                                 axis=0).astype(BF16)
            vt2 = jnp.concatenate([vt_r[0, 0:GLA_DV, pl.ds(off, c)], vt_r[0, GLA_DV:2 * GLA_DV, pl.ds(off, c)]],
                                  axis=-1)
            st_ref[slot0 + i] = _dot(vt2, kk)
            dec_ref[slot0 + i, 0:1, :] = jnp.exp(rf[2 * c - 1:2 * c, :])
            dec_ref[slot0 + i, 1:2, :] = jnp.exp(rb[c:c + 1, :])

    def group_size(n):
        return max(g for g in (4, 2, 1) if n % g == 0)

    @pl.when(tile == 0)
    def _():
        gc, gl = group_size(n_ctx), group_size(n_lat)

        def ctx_body(i, carry):
            increments(kc_ref, vtc_ref, lafc_ref, labc_ref, i * gc, i * gc, gc)
            return carry
        lax.fori_loop(0, n_ctx // gc, ctx_body, 0)

        def lat_body(i, carry):
            increments(k_ref, vt_ref, laf_ref, lab_ref, i * gl, n_ctx + i * gl, gl)
            return carry
        lax.fori_loop(0, n_lat // gl, lat_body, 0)

        def fwd_body(s, st):
            inc = st_ref[s, :, 0:LANES]
            st_ref[s, :, 0:LANES] = st
            return st * dec_ref[s, 0:1, :] + inc
        lax.fori_loop(0, n_all, fwd_body, jnp.zeros((GLA_DV, LANES), F32))

        def bwd_body(j, st, base, count):
            s = base + count - 1 - j
            inc = st_ref[s, :, LANES:2 * LANES]
            st_ref[s, :, LANES:2 * LANES] = st
            return st * dec_ref[s, 1:2, :] + inc
        st = lax.fori_loop(0, n_ctx, functools.partial(bwd_body, base=0, count=n_ctx),
                           jnp.zeros((GLA_DV, LANES), F32))
        lax.fori_loop(0, n_lat, functools.partial(bwd_body, base=n_ctx, count=n_lat), st)

    xor2 = lax.broadcasted_iota(jnp.int32, (c, 2 * c), 0) ^ (lax.broadcasted_iota(jnp.int32, (c, 2 * c), 1) & (c - 1))
    row_l = lax.broadcasted_iota(jnp.int32, (c, LANES), 0)
    lane2v = lax.broadcasted_iota(jnp.int32, (c, 2 * GLA_DV), 1)
    lane2s = lax.broadcasted_iota(jnp.int32, (GLA_DV, 2 * LANES), 1) % LANES

    def stack_heads(x):
        return jnp.concatenate([jnp.where(head_masks[0], x, 0.0), jnp.where(head_masks[1], x, 0.0)], axis=0)

    go = 2 if cpt % 2 == 0 else 1

    def out_body(i, carry):
        idx = range(go)
        ns = [tile * cpt + i * go + u for u in idx]
        offs = [pl.multiple_of(n * c, c) for n in ns]
        locs = [pl.multiple_of((i * go + u) * c, c) for u in idx]
        rfs = [_cum(mf_ref[c:, :], laf_ref[0, pl.ds(off, c), :]) for off in offs]
        rbs = [_cum(mb_ref[c:, :], lab_ref[0, pl.ds(off, c), :]) for off in offs]
        qs = [q_ref[0, pl.ds(loc, c), :].astype(F32) for loc in locs]
        ks = [k_ref[0, pl.ds(off, c), :].astype(F32) for off in offs]
        a = [2.0 * _dot_nt(qs[u].astype(BF16), stack_heads(ks[u]).astype(BF16)) for u in idx]
        for lvl in range(_GLA_LEVELS):
            sh = _GLA_LEVELS - 1 - lvl
            second = ((row_l >> sh) & 1) == 1
            for u in idx:
                ef = jnp.exp(rfs[u][(1 + lvl) * c:(2 + lvl) * c])
                eb = jnp.exp(rbs[u][(1 + lvl) * c:(2 + lvl) * c])
                ql = (qs[u] * jnp.where(second, ef, eb)).astype(BF16)
                kl = stack_heads(ks[u] * jnp.where(second, eb, ef)).astype(BF16)
                a[u] = jnp.where((xor2 >> sh) == 1, _dot_nt(ql, kl), a[u])
        for u in idx:
            v = v_ref[0, pl.ds(locs[u], c), :]
            v_bd = jnp.concatenate([jnp.where(lane2v < GLA_DV, v, jnp.zeros_like(v)),
                                    jnp.where(lane2v >= GLA_DV, v, jnp.zeros_like(v))], axis=0)
            q_inter = jnp.concatenate([qs[u] * jnp.exp(rfs[u][0:c]), qs[u] * jnp.exp(rbs[u][0:c])],
                                      axis=-1).astype(BF16)
            states = st_ref[n_ctx + ns[u]]
            st2 = jnp.concatenate([jnp.where(lane2s < GLA_DK, states, 0.0),
                                   jnp.where(lane2s >= GLA_DK, states, 0.0)], axis=0).astype(BF16)
            o_ref[0, pl.ds(locs[u], c), :] = _dot(a[u].astype(BF16), v_bd) + _dot_nt(q_inter, st2)
        return carry

    lax.fori_loop(0, cpt // go, out_body, 0)


def _gla(gq, gk, gv, gvt, laf, lab, gk_c, gvt_c, laf_c, lab_c, mf, mb, *, tile=1024):
    nb, t, _ = gq.shape
    tc = gk_c.shape[1] // nb
    tile = min(tile, t)
    c = GLA_CHUNK
    n_all = (t + tc) // c
    pair_k = 2 * GLA_DK
    pair_v = 2 * GLA_DV
    return pl.pallas_call(
        functools.partial(_gla_kernel, cpt=tile // c),
        grid=(nb, GLA_HEADS // 2, t // tile),
        in_specs=[
            pl.BlockSpec((1, tile, pair_k), lambda b, p, i: (b, i, p)),
            pl.BlockSpec((1, t, pair_k), lambda b, p, i: (b, 0, p)),
            pl.BlockSpec((1, tile, pair_v), lambda b, p, i: (b, i, p)),
            pl.BlockSpec((1, pair_v, t), lambda b, p, i: (b, p, 0)),
            pl.BlockSpec((1, t, pair_k), lambda b, p, i: (b, 0, p)),
            pl.BlockSpec((1, t, pair_k), lambda b, p, i: (b, 0, p)),
            pl.BlockSpec((1, tc, pair_k), lambda b, p, i: (0, b, p)),
            pl.BlockSpec((1, pair_v, tc), lambda b, p, i: (0, p, b)),
            pl.BlockSpec((1, tc, pair_k), lambda b, p, i: (0, b, p)),
            pl.BlockSpec((1, tc, pair_k), lambda b, p, i: (0, b, p)),
            _const_spec(tuple(mf.shape)), _const_spec(tuple(mb.shape)),
        ],
        out_specs=pl.BlockSpec((1, tile, pair_v), lambda b, p, i: (b, i, p)),
        out_shape=jax.ShapeDtypeStruct((nb, t, GLA_V_W), F32),
        scratch_shapes=[pltpu.VMEM((n_all, GLA_DV, 2 * LANES), F32),
                        pltpu.VMEM((n_all, 8, LANES), F32)],
        compiler_params=pltpu.CompilerParams(dimension_semantics=("arbitrary", "arbitrary", "arbitrary"),
                                             vmem_limit_bytes=VMEM_LIMIT),
        name="gla",
    )(gq, gk, gv, gvt, laf, lab, gk_c, gvt_c, laf_c, lab_c, mf, mb)


def _rope_tables(t):
    rows = t // GRID_W
    row = jnp.repeat(jnp.arange(rows, dtype=F32), GRID_W)
    col = jnp.tile(jnp.arange(GRID_W, dtype=F32), rows)
    freqs = ROPE_THETA ** (-jnp.arange(0, ROPE_AXIS_DIM, 2, dtype=F32) / ROPE_AXIS_DIM)
    ang = jnp.concatenate([row[:, None] * freqs, col[:, None] * freqs], axis=-1)
    cos, sin = jnp.cos(ang), jnp.sin(ang)
    reps = LANES // HEAD_DIM
    return (jnp.tile(jnp.concatenate([cos, cos], axis=-1), (1, reps)),
            jnp.tile(jnp.concatenate([-sin, sin], axis=-1), (1, reps)))


def kernel(x, c, ctx, c_ctx, w_mod, b_mod, g_norm, w_ffn_up, w_ffn_down, w_in, g_q, g_k,
           w_decay, b_decay, g_gla, w_branch, b_gate, w_out, g_final):
    nb, t, d = x.shape
    tc = ctx.shape[1]
    f = w_ffn_down.shape[2]
    assert w_mod.shape[0] == 1, "single layer"
    assert t % GLA_CHUNK == 0 and tc % GLA_CHUNK == 0 and t % GRID_W == 0

    rows = -(-(nb + 1) // 8) * 8
    c_rows = jnp.zeros((rows, d), F32).at[:nb].set(c).at[nb].set(c_ctx)
    m = _modulation(c_rows, w_mod[0], b_mod[0]).reshape(rows, N_MOD, d)
    m_lat, m_ctx = m[:nb], m[nb:nb + 1]

    wa = [w_ffn_up[0, i, :, :f].astype(BF16) for i in range(2)]
    wb = [w_ffn_up[0, i, :, f:].astype(BF16) for i in range(2)]
    wd = [w_ffn_down[0, i].astype(BF16) for i in range(2)]
    c_low = _C_GT
    w_low = w_in[0][:, c_low:c_low + 2 * GLA_RANK]
    w_r = jnp.concatenate([w_in[0][:, :c_low], w_in[0][:, c_low + 2 * GLA_RANK:], w_low,
                           jnp.zeros((d, LANES - 2 * GLA_RANK), F32)], axis=-1).astype(BF16)
    w_bd = jnp.zeros((2 * GLA_RANK, 2 * GLA_K_W), F32)
    w_bd = w_bd.at[:GLA_RANK, :GLA_K_W].set(w_decay[0, 0]).at[GLA_RANK:, GLA_K_W:].set(w_decay[0, 1])
    w_bd_hi = w_bd.astype(BF16)
    w_bd_lo = (w_bd - w_bd_hi.astype(F32)).astype(BF16)
    w_dec = jnp.concatenate([w_bd_hi, w_bd_hi, w_bd_lo, jnp.zeros_like(w_bd_hi)], axis=0)
    b_dec = b_decay[0].reshape(1, 2 * GLA_K_W)
    gqk = jnp.concatenate([jnp.tile(g_q[0], ATTN_HEADS), jnp.tile(g_k[0], ATTN_KV_HEADS)]).reshape(1, -1)
    lane = np.arange(LANES)
    seg = jnp.asarray(np.tile((lane[:, None] // HEAD_DIM == lane[None, :] // HEAD_DIM) / HEAD_DIM, (2, 1)), dtype=BF16)
    cos_t, sin_t = _rope_tables(t)
    ones_t, zeros_t = jnp.ones((nb * tc, LANES), F32), jnp.zeros((nb * tc, LANES), F32)
    mf, mb = _gla_constants()

    proj = functools.partial(_in_proj, g=g_norm[0, 1], w_r=w_r, gqk=gqk, seg=seg, w_dec=w_dec,
                             b_dec=b_dec, b_gate=b_gate[0].reshape(1, 2 * d))

    hc = _half_ffn(ctx.reshape(1, nb * tc, d), m_ctx[:, 0:3], g_norm[0, 0], wa[0], wb[0], wd[0])
    pc = proj(hc, m_ctx[:, 3:5], cos_t=ones_t, sin_t=zeros_t)
    _, k_c, vt_c, _, gk_c, _, gvt_c, laf_c, lab_c, _, _ = pc

    h1 = _half_ffn(x, m_lat[:, 0:3], g_norm[0, 0], wa[0], wb[0], wd[0])
    qt, k, vt, gq, gk, gv, gvt, laf, lab, gs, gt = proj(h1, m_lat[:, 3:5], cos_t=cos_t, sin_t=sin_t)
    k_all = jnp.concatenate(
        [k, k_c.reshape(ATTN_KV_HEADS, nb, tc, HEAD_DIM).transpose(1, 0, 2, 3)], axis=2)
    vt_all = jnp.concatenate(
        [vt, vt_c.reshape(ATTN_KV_HEADS, V_ROWS, nb, tc).transpose(2, 0, 1, 3)], axis=3)
    attn_o = _attention(qt, k_all, vt_all, tc)
    gla_o = _gla(gq, gk, gv, gvt, laf, lab, gk_c, gvt_c, laf_c, lab_c, mf, mb)
    return _merge_ffn(h1, attn_o, gla_o, gs, gt, m_lat[:, 5:6], g_gla[0],
                      w_branch[0, 0].astype(BF16), w_branch[0, 1].astype(BF16), w_out[0].astype(BF16),
                      m_lat[:, 6:9], g_norm[0, 2], wa[1], wb[1], wd[1], g_final)
```

```python
import functools

import numpy as np
import jax
import jax.numpy as jnp
from jax import lax
from jax.experimental import pallas as pl
from jax.experimental.pallas import tpu as pltpu

F32 = jnp.float32
BF16 = jnp.bfloat16

EPS = 1e-6
GRID_W = 64
N_MOD = 9
ATTN_HEADS = 8
ATTN_KV_HEADS = 2
HEAD_DIM = 64
ROPE_AXIS_DIM = HEAD_DIM // 2
ROPE_THETA = 10000.0
GLA_HEADS = 4
GLA_DK = 64
GLA_DV = 128
GLA_RANK = 16
GLA_GATE_NORM = 16.0
ATTN_Q_W = ATTN_HEADS * HEAD_DIM
ATTN_KV_W = ATTN_KV_HEADS * HEAD_DIM
GLA_K_W = GLA_HEADS * GLA_DK
GLA_V_W = GLA_HEADS * GLA_DV
Q_GROUP = ATTN_HEADS // ATTN_KV_HEADS

LANES = 128
V_ROWS = 80
GLA_CHUNK = 128
VMEM_LIMIT = 56 * 1024 * 1024


def _dot(a, b):
    return jnp.dot(a, b, preferred_element_type=F32)


def _dot_nt(a, b):
    return lax.dot_general(a, b, (((1,), (1,)), ((), ())), preferred_element_type=F32)


def _sigmoid(x):
    return 1.0 / (1.0 + jnp.exp(-x))


def _split2(x):
    hi = x.astype(BF16)
    lo = (x - hi.astype(F32)).astype(BF16)
    return hi, lo


def _rms(x):
    return x * lax.rsqrt(jnp.mean(x * x, axis=-1, keepdims=True) + EPS)


def _const_spec(shape):
    nd = len(shape)
    return pl.BlockSpec(shape, lambda *_: (0,) * nd, pipeline_mode=pl.Buffered(1))


def _mod_kernel(c_ref, w_ref, b_ref, o_ref):
    c = c_ref[...]
    s_hi, s_lo = _split2(c * _sigmoid(c))
    w_hi, w_lo = _split2(w_ref[...])
    o_ref[...] = _dot(s_hi, w_hi) + _dot(s_hi, w_lo) + _dot(s_lo, w_hi) + b_ref[...]


def _modulation(c_rows, w_mod, b_mod):
    rows, d = c_rows.shape
    n = w_mod.shape[1]
    tn = 1024
    return pl.pallas_call(
        _mod_kernel,
        grid=(n // tn,),
        in_specs=[pl.BlockSpec((rows, d), lambda j: (0, 0)),
                  pl.BlockSpec((d, tn), lambda j: (0, j)),
                  pl.BlockSpec((1, tn), lambda j: (0, j))],
        out_specs=pl.BlockSpec((rows, tn), lambda j: (0, j)),
        out_shape=jax.ShapeDtypeStruct((rows, n), F32),
        compiler_params=pltpu.CompilerParams(dimension_semantics=("arbitrary",),
                                             vmem_limit_bytes=VMEM_LIMIT),
        name="modulation",
    )(c_rows, w_mod, b_mod.reshape(1, n))


def _ffn_math(x, mod_ref, g_ref, wa_ref, wb_ref, wd_ref, n_chunks):
    shift, scale, gate = mod_ref[0, 0:1, :], mod_ref[0, 1:2, :], mod_ref[0, 2:3, :]
    n = ((_rms(x) * g_ref[...]) * (1.0 + scale) + shift).astype(BF16)
    fc = wa_ref.shape[1] // n_chunks
    acc = None
    for c in range(n_chunks):
        a = _dot(n, wa_ref[:, c * fc:(c + 1) * fc])
        b = _dot(n, wb_ref[:, c * fc:(c + 1) * fc])
        act = ((a * _sigmoid(a)) * b).astype(BF16)
        part = _dot(act, wd_ref[c * fc:(c + 1) * fc, :])
        acc = part if acc is None else acc + part
    return x + 0.5 * gate * acc


def _ffn_kernel(h_ref, mod_ref, g_ref, wa_ref, wb_ref, wd_ref, o_ref, *, n_chunks):
    o_ref[0] = _ffn_math(h_ref[0], mod_ref, g_ref, wa_ref, wb_ref, wd_ref, n_chunks)


def _merge_ffn_kernel(h_ref, ao_ref, go_ref, gs_ref, gt_ref, mg_ref, gg_ref, wb0_ref, wb1_ref, wo_ref,
                      mod_ref, g_ref, wa_ref, wb_ref, wd_ref, gf_ref, o_ref, *, n_chunks):
    d = h_ref.shape[2]
    go = go_ref[0]
    normed = [_rms(go[:, h * GLA_DV:(h + 1) * GLA_DV]) * gg_ref[...] for h in range(GLA_HEADS)]
    gn = (jnp.concatenate(normed, axis=-1) * gs_ref[0].astype(F32)).astype(BF16)
    y_attn = _dot(ao_ref[0], wb0_ref[...])
    y_gla = _dot(gn, wb1_ref[...])
    gt = gt_ref[0].astype(F32)
    z = (gt[:, :d] * y_attn + gt[:, d:] * y_gla).astype(BF16)
    h2 = h_ref[0] + mg_ref[0] * _dot(z, wo_ref[...])
    out = _ffn_math(h2, mod_ref, g_ref, wa_ref, wb_ref, wd_ref, n_chunks)
    o_ref[0] = _rms(out) * gf_ref[...]


def _ffn_weight_specs(d, f):
    return [_const_spec((1, d)), _const_spec((d, f)), _const_spec((d, f)), _const_spec((f, d))]


def _half_ffn(h, mod3, g, wa, wb, wd, *, tm=512, n_chunks=1):
    nb, t, d = h.shape
    f = wa.shape[1]
    tm = min(tm, t)
    return pl.pallas_call(
        functools.partial(_ffn_kernel, n_chunks=n_chunks),
        grid=(nb, t // tm),
        in_specs=[pl.BlockSpec((1, tm, d), lambda b, i: (b, i, 0)),
                  pl.BlockSpec((1, 3, d), lambda b, i: (b, 0, 0))] + _ffn_weight_specs(d, f),
        out_specs=pl.BlockSpec((1, tm, d), lambda b, i: (b, i, 0)),
        out_shape=jax.ShapeDtypeStruct((nb, t, d), F32),
        compiler_params=pltpu.CompilerParams(dimension_semantics=("arbitrary", "arbitrary"),
                                             vmem_limit_bytes=VMEM_LIMIT),
        name="half_ffn",
    )(h, mod3, g.reshape(1, d), wa, wb, wd)


def _merge_ffn(h, attn_o, gla_o, gs, gt, m_gate, g_gla, wb0, wb1, wo, mod3, g, wa, wb, wd, g_final,
               *, tm=512, n_chunks=1):
    nb, t, d = h.shape
    f = wa.shape[1]
    tm = min(tm, t)
    tok = lambda w: pl.BlockSpec((1, tm, w), lambda b, i: (b, i, 0))
    return pl.pallas_call(
        functools.partial(_merge_ffn_kernel, n_chunks=n_chunks),
        grid=(nb, t // tm),
        in_specs=[tok(d), tok(ATTN_Q_W), tok(GLA_V_W), tok(GLA_V_W), tok(2 * d),
                  pl.BlockSpec((1, 1, d), lambda b, i: (b, 0, 0)),
                  _const_spec((1, GLA_DV)), _const_spec((ATTN_Q_W, d)), _const_spec((GLA_V_W, d)),
                  _const_spec((d, d)),
                  pl.BlockSpec((1, 3, d), lambda b, i: (b, 0, 0))] + _ffn_weight_specs(d, f)
                 + [_const_spec((1, d))],
        out_specs=tok(d),
        out_shape=jax.ShapeDtypeStruct((nb, t, d), F32),
        compiler_params=pltpu.CompilerParams(dimension_semantics=("arbitrary", "arbitrary"),
                                             vmem_limit_bytes=VMEM_LIMIT),
        name="merge_ffn_final",
    )(h, attn_o, gla_o, gs, gt, m_gate, g_gla.reshape(1, GLA_DV), wb0, wb1, wo,
      mod3, g.reshape(1, d), wa, wb, wd, g_final.reshape(1, d))


_C_Q = 0
_C_K = _C_Q + ATTN_Q_W
_C_V = _C_K + ATTN_KV_W
_C_GQ = _C_V + ATTN_KV_W
_C_GK = _C_GQ + GLA_K_W
_C_GV = _C_GK + GLA_K_W
_C_GS = _C_GV + GLA_V_W
_C_GT = _C_GS + GLA_V_W


def _proj_kernel(h_ref, mod_ref, g_ref, w_ref, gqk_ref, seg_ref, cos_ref, sin_ref, wdec_ref,
                 bdec_ref, bgate_ref,
                 qt_ref, k_ref, vt_ref, gq_ref, gk_ref, gv_ref, gvt_ref, laf_ref, lab_ref, gs_ref, gt_ref):
    d = h_ref.shape[2]
    c_low = _C_GT + 2 * d
    x = h_ref[0]
    shift, scale = mod_ref[0, 0:1, :], mod_ref[0, 1:2, :]
    n = ((_rms(x) * g_ref[...]) * (1.0 + scale) + shift).astype(BF16)
    tm = x.shape[0]

    qk = _dot(n, w_ref[:, _C_Q:_C_V])
    low = _dot(n, w_ref[:, c_low:c_low + LANES])
    gt = _dot(n, w_ref[:, _C_GT:c_low]) + bgate_ref[...]

    seg = seg_ref[...]
    ms = []
    for j in range((ATTN_Q_W + ATTN_KV_W) // LANES):
        sq = qk[:, j * LANES:(j + 1) * LANES]
        hi, lo = _split2(sq * sq)
        ms.append(_dot(jnp.concatenate([hi, lo], axis=-1), seg))
    ms = jnp.concatenate(ms, axis=-1)
    gs = _dot(n, w_ref[:, _C_GS:_C_GT])

    l_hi = low.astype(BF16).astype(F32)
    l_lo = low - l_hi
    packed = l_hi + pltpu.roll(l_lo, 2 * GLA_RANK, 1) + pltpu.roll(l_hi, 4 * GLA_RANK, 1)
    z = _dot(packed.astype(BF16), wdec_ref[...]) + bdec_ref[...]
    gv = _dot(n, w_ref[:, _C_GV:_C_GS])
    vt = _dot(n, w_ref[:, _C_V:_C_GQ]).T
    gqk = _dot(n, w_ref[:, _C_GQ:_C_GV])

    qk = qk * lax.rsqrt(ms + EPS) * gqk_ref[...]
    lane = lax.broadcasted_iota(jnp.int32, (tm, LANES), 1)
    first = (lane % HEAD_DIM) < (HEAD_DIM // 2)
    cos, sin = cos_ref[...], sin_ref[...]
    rot = []
    for j in range((ATTN_Q_W + ATTN_KV_W) // LANES):
        xs = qk[:, j * LANES:(j + 1) * LANES]
        other = jnp.where(first, pltpu.roll(xs, LANES - HEAD_DIM // 2, 1), pltpu.roll(xs, HEAD_DIM // 2, 1))
        rot.append(xs * cos + other * sin)
    q_scale = HEAD_DIM ** -0.5 * float(np.log2(np.e))
    for j in range(ATTN_Q_W // LANES):
        qt_ref[0, j * LANES:(j + 1) * LANES, :] = (rot[j] * q_scale).T.astype(BF16)
    k_rot = rot[ATTN_Q_W // LANES]
    for g in range(ATTN_KV_HEADS):
        k_ref[0, g] = k_rot[:, g * HEAD_DIM:(g + 1) * HEAD_DIM].astype(BF16)

    tail = jnp.where(lax.broadcasted_iota(jnp.int32, (V_ROWS - HEAD_DIM, tm), 0) == 0, 1.0, 0.0)
    for g in range(ATTN_KV_HEADS):
        vt_ref[0, g] = jnp.concatenate([vt[g * HEAD_DIM:(g + 1) * HEAD_DIM], tail], axis=0).astype(BF16)

    gq_ref[0] = (gqk[:, :GLA_K_W] * (GLA_DK ** -0.5)).astype(BF16)
    gk_ref[0] = gqk[:, GLA_K_W:].astype(BF16)
    gv_ref[0] = gv.astype(BF16)
    gvt_ref[0] = gv.T.astype(BF16)

    la = (jnp.minimum(z, 0.0) - jnp.log(1.0 + jnp.exp(-jnp.abs(z)))) * (1.0 / GLA_GATE_NORM)
    laf_ref[0] = la[:, :GLA_K_W]
    lab_ref[0] = la[:, GLA_K_W:]

    gs_ref[0] = (gs * _sigmoid(gs)).astype(BF16)
    gt_ref[0] = _sigmoid(gt).astype(BF16)


def _in_proj(h, mod2, g, w_r, gqk, seg, cos_t, sin_t, w_dec, b_dec, b_gate, *, tm=512):
    nb, t, d = h.shape
    tm = min(tm, t)
    wp = w_r.shape[1]
    tok = lambda w: pl.BlockSpec((1, tm, w), lambda b, i: (b, i, 0))
    out_shape = [
        jax.ShapeDtypeStruct((nb, ATTN_Q_W, t), BF16),
        jax.ShapeDtypeStruct((nb, ATTN_KV_HEADS, t, HEAD_DIM), BF16),
        jax.ShapeDtypeStruct((nb, ATTN_KV_HEADS, V_ROWS, t), BF16),
        jax.ShapeDtypeStruct((nb, t, GLA_K_W), BF16),
        jax.ShapeDtypeStruct((nb, t, GLA_K_W), BF16),
        jax.ShapeDtypeStruct((nb, t, GLA_V_W), BF16),
        jax.ShapeDtypeStruct((nb, GLA_V_W, t), BF16),
        jax.ShapeDtypeStruct((nb, t, GLA_K_W), F32),
        jax.ShapeDtypeStruct((nb, t, GLA_K_W), F32),
        jax.ShapeDtypeStruct((nb, t, GLA_V_W), BF16),
        jax.ShapeDtypeStruct((nb, t, 2 * d), BF16),
    ]
    out_specs = [
        pl.BlockSpec((1, ATTN_Q_W, tm), lambda b, i: (b, 0, i)),
        pl.BlockSpec((1, ATTN_KV_HEADS, tm, HEAD_DIM), lambda b, i: (b, 0, i, 0)),
        pl.BlockSpec((1, ATTN_KV_HEADS, V_ROWS, tm), lambda b, i: (b, 0, 0, i)),
        tok(GLA_K_W), tok(GLA_K_W), tok(GLA_V_W),
        pl.BlockSpec((1, GLA_V_W, tm), lambda b, i: (b, 0, i)),
        tok(GLA_K_W), tok(GLA_K_W), tok(GLA_V_W), tok(2 * d),
    ]
    in_specs = [
        pl.BlockSpec((1, tm, d), lambda b, i: (b, i, 0)),
        pl.BlockSpec((1, 2, d), lambda b, i: (b, 0, 0)),
        _const_spec((1, d)), _const_spec((d, wp)), _const_spec((1, ATTN_Q_W + ATTN_KV_W)),
        _const_spec((2 * LANES, LANES)),
        pl.BlockSpec((tm, LANES), lambda b, i: (i, 0)),
        pl.BlockSpec((tm, LANES), lambda b, i: (i, 0)),
        _const_spec((LANES, 2 * GLA_K_W)),
        _const_spec((1, 2 * GLA_K_W)), _const_spec((1, 2 * d)),
    ]
    return pl.pallas_call(
        _proj_kernel,
        grid=(nb, t // tm),
        in_specs=in_specs,
        out_specs=out_specs,
        out_shape=out_shape,
        compiler_params=pltpu.CompilerParams(dimension_semantics=("arbitrary", "arbitrary"),
                                             vmem_limit_bytes=VMEM_LIMIT),
        name="in_proj",
    )(h, mod2, g.reshape(1, d), w_r, gqk, seg, cos_t, sin_t, w_dec, b_dec, b_gate)


ATTN_GROUPS = 2
ATTN_ROW_BLOCK = 256
ATTN_KC = 1024
ATTN_UNROLL = 3


def _attn_plan(t, tc):
    if tc % (2 * LANES) == 0:
        kc = max(c for c in range(LANES, ATTN_KC + 1, LANES) if t % c == 0)
        edge = tc // 2
        chunks = [(t, edge)] + [(i * kc, kc) for i in range(t // kc)] + [(t + edge, edge)]
        return chunks, (1, len(chunks) - 1)
    s_len = t + tc
    kc = max(c for c in range(LANES, ATTN_KC + 1, LANES) if s_len % c == 0)
    chunks = [(i * kc, kc) for i in range(s_len // kc)]
    return chunks, (0, len(chunks))


def _attn_kernel(qt_ref, k_ref, vt_ref, o_ref, qs_ref, *group_refs, chunks, run):
    tq = qt_ref.shape[2]
    gw = Q_GROUP * tq // ATTN_GROUPS
    per = len(group_refs) // ATTN_GROUPS
    m_refs, cmax_refs, acc_refs, s_refs = (
        [group_refs[g * per + j] for g in range(ATTN_GROUPS)] for j in range(per))
    for h in range(Q_GROUP):
        qs_ref[:, h * tq:(h + 1) * tq] = qt_ref[0, h * HEAD_DIM:(h + 1) * HEAD_DIM, :]
    for g in range(ATTN_GROUPS):
        m_refs[g][...] = jnp.full(m_refs[g].shape, -jnp.inf, F32)
        acc_refs[g][...] = jnp.zeros(acc_refs[g].shape, F32)

    def rows(off, j, rb):
        start = off + j * rb
        return pl.ds(start if isinstance(start, int) else pl.multiple_of(start, LANES), rb)

    def stage(score, apply):
        n_s = n_a = 0
        if score is not None:
            off_s, size_s, gs = score
            rb_s = min(ATTN_ROW_BLOCK, size_s)
            n_s = size_s // rb_s
        if apply is not None:
            off_a, size_a, ga = apply
            rb_a = min(ATTN_ROW_BLOCK, size_a)
            n_a = size_a // rb_a
            m_old = m_refs[ga][...]
            m_new = jnp.maximum(m_old, cmax_refs[ga][...])
            m_refs[ga][...] = m_new
        cmax = pv_sum = None
        for j in range(max(n_s, n_a)):
            if j < n_a:
                p = jnp.exp2(s_refs[ga][j * rb_a:(j + 1) * rb_a, :] - m_new).astype(BF16)
            if j < n_s:
                s = _dot(k_ref[0, 0, rows(off_s, j, rb_s), :], qs_ref[:, gs * gw:(gs + 1) * gw])
                s_refs[gs][j * rb_s:(j + 1) * rb_s, :] = s
                bmax = jnp.max(s, axis=0, keepdims=True)
                cmax = bmax if cmax is None else jnp.maximum(cmax, bmax)
            if j < n_a:
                part = _dot(vt_ref[0, 0, :, rows(off_a, j, rb_a)], p)
                pv_sum = part if pv_sum is None else pv_sum + part
        if score is not None:
            cmax_refs[gs][...] = cmax
        if apply is not None:
            acc_refs[ga][...] = jnp.exp2(m_old - m_new) * acc_refs[ga][...] + pv_sum

    last = ATTN_GROUPS - 1

    def chunk_stages(cur, prev):
        stage(cur + (0,), None if prev is None else prev + (last,))
        for g in range(1, ATTN_GROUPS):
            stage(cur + (g,), cur + (g - 1,))

    a, b = run
    trips = b - a - 1
    unroll = ATTN_UNROLL if trips >= ATTN_UNROLL else 1
    first_loop = a + 1 + trips % unroll
    for i in range(first_loop):
        chunk_stages(chunks[i], chunks[i - 1] if i else None)
    if b > first_loop:
        off0, kc = chunks[first_loop]

        def body(i, carry):
            for u in range(unroll):
                off = off0 + (i * unroll + u) * kc
                chunk_stages((off, kc), (off - kc, kc))
            return carry

        lax.fori_loop(0, (b - first_loop) // unroll, body, 0)
    for i in range(b, len(chunks)):
        chunk_stages(chunks[i], chunks[i - 1])
    stage(None, chunks[-1] + (last,))

    acc = jnp.concatenate([r[...] for r in acc_refs], axis=1)
    ot = acc[:HEAD_DIM] / acc[HEAD_DIM:HEAD_DIM + 1]
    for j in range(Q_GROUP // 2):
        pair = jnp.concatenate([ot[:, (2 * j) * tq:(2 * j + 1) * tq], ot[:, (2 * j + 1) * tq:(2 * j + 2) * tq]],
                               axis=0)
        o_ref[0, :, j * LANES:(j + 1) * LANES] = pair.T.astype(o_ref.dtype)


def _attention(qt, k, vt, t_ctx, *, tq=512):
    nb, _, t = qt.shape
    s_len = k.shape[2]
    tq = min(tq, t)
    chunks, run = _attn_plan(t, t_ctx)
    kc = max(size for _, size in chunks)
    gw = Q_GROUP * tq // ATTN_GROUPS
    return pl.pallas_call(
        functools.partial(_attn_kernel, chunks=tuple(chunks), run=run),
        grid=(nb, ATTN_KV_HEADS, t // tq),
        in_specs=[
            pl.BlockSpec((1, Q_GROUP * HEAD_DIM, tq), lambda b, g, i: (b, g, i)),
            pl.BlockSpec((1, 1, s_len, HEAD_DIM), lambda b, g, i: (b, g, 0, 0)),
            pl.BlockSpec((1, 1, V_ROWS, s_len), lambda b, g, i: (b, g, 0, 0)),
        ],
        out_specs=pl.BlockSpec((1, tq, Q_GROUP * HEAD_DIM), lambda b, g, i: (b, i, g)),
        out_shape=jax.ShapeDtypeStruct((nb, t, ATTN_Q_W), BF16),
        scratch_shapes=[pltpu.VMEM((HEAD_DIM, Q_GROUP * tq), BF16)] + ATTN_GROUPS * [
            pltpu.VMEM((1, gw), F32),
            pltpu.VMEM((1, gw), F32),
            pltpu.VMEM((V_ROWS, gw), F32),
            pltpu.VMEM((kc, gw), F32)],
        compiler_params=pltpu.CompilerParams(dimension_semantics=("arbitrary", "arbitrary", "arbitrary"),
                                             vmem_limit_bytes=VMEM_LIMIT),
        name="flash_attention",
    )(qt, k, vt)


def _gla_constants():
    c = GLA_CHUNK
    i = np.arange(c)[:, None]
    t = np.arange(c)[None, :]
    fwd = [t > i, t <= i]
    bwd = [t < i, t >= i]
    s = c // 2
    while s >= 1:
        mid = (i // (2 * s)) * (2 * s) + s
        second = (i % (2 * s)) >= s
        fwd.append(np.where(second, (t >= mid) & (t <= i), (t > i) & (t < mid)))
        bwd.append(np.where(second, (t >= mid) & (t < i), (t >= i) & (t < mid)))
        s //= 2
    to = lambda blocks: jnp.asarray(np.concatenate(blocks, axis=0).astype(np.float32), dtype=BF16)
    return to(fwd), to(bwd)


_GLA_LEVELS = int(np.log2(GLA_CHUNK))


def _cum(mat, la):
    hi, lo = _split2(la)
    r = _dot(mat, jnp.concatenate([hi, lo], axis=-1))
    return r[:, :LANES] + r[:, LANES:]


def _gla_kernel(q_ref, k_ref, v_ref, vt_ref, laf_ref, lab_ref, kc_ref, vtc_ref, lafc_ref, labc_ref,
                mf_ref, mb_ref, o_ref, st_ref, dec_ref, *, cpt):
    c = GLA_CHUNK
    n_lat = k_ref.shape[1] // c
    n_ctx = kc_ref.shape[1] // c
    n_all = n_ctx + n_lat
    tile = pl.program_id(2)
    lane = lax.broadcasted_iota(jnp.int32, (c, LANES), 1)
    head_masks = [lane < GLA_DK, lane >= GLA_DK]
    lane2 = lax.broadcasted_iota(jnp.int32, (c, 2 * LANES), 1) % LANES
    pair_masks = [lane2 < GLA_DK, lane2 >= GLA_DK]

    def increments(k_r, vt_r, laf_r, lab_r, n0, slot0, count):
        offs = [pl.multiple_of((n0 + i) * c, c) for i in range(count)]
        rfs = [_cum(mf_ref[0:2 * c, :], laf_r[0, pl.ds(off, c), :]) for off in offs]
        rbs = [_cum(mb_ref[0:2 * c, :], lab_r[0, pl.ds(off, c), :]) for off in offs]
        for i, (off, rf, rb) in enumerate(zip(offs, rfs, rbs)):
            k = k_r[0, pl.ds(off, c), :].astype(F32)
            kfb = jnp.concatenate([k * jnp.exp(rf[0:c]),
                                   k * jnp.exp(rb[0:c])], axis=-1)
            kk = jnp.concatenate([jnp.where(pair_masks[0], kfb, 0.0), jnp.where(pair_masks[1], kfb, 0.0)],
                                 axis=0).astype(BF16)
            vt2 = jnp.concatenate([vt_r[0, 0:GLA_DV, pl.ds(off, c)], vt_r[0, GLA_DV:2 * GLA_DV, pl.ds(off, c)]],
                                  axis=-1)
            st_ref[slot0 + i] = _dot(vt2, kk)
            dec_ref[slot0 + i, 0:1, :] = jnp.exp(rf[2 * c - 1:2 * c, :])
            dec_ref[slot0 + i, 1:2, :] = jnp.exp(rb[c:c + 1, :])

    def group_size(n):
        return max(g for g in (4, 2, 1) if n % g == 0)

    @pl.when(tile == 0)
    def _():
        gc, gl = group_size(n_ctx), group_size(n_lat)

        def ctx_body(i, carry):
            increments(kc_ref, vtc_ref, lafc_ref, labc_ref, i * gc, i * gc, gc)
            return carry
        lax.fori_loop(0, n_ctx // gc, ctx_body, 0)

        def lat_body(i, carry):
            increments(k_ref, vt_ref, laf_ref, lab_ref, i * gl, n_ctx + i * gl, gl)
            return carry
        lax.fori_loop(0, n_lat // gl, lat_body, 0)

        def fwd_body(s, st):
            inc = st_ref[s, :, 0:LANES]
            st_ref[s, :, 0:LANES] = st
            return st * dec_ref[s, 0:1, :] + inc
        lax.fori_loop(0, n_all, fwd_body, jnp.zeros((GLA_DV, LANES), F32))

        def bwd_body(j, st, base, count):
            s = base + count - 1 - j
            inc = st_ref[s, :, LANES:2 * LANES]
            st_ref[s, :, LANES:2 * LANES] = st
            return st * dec_ref[s, 1:2, :] + inc
        st = lax.fori_loop(0, n_ctx, functools.partial(bwd_body, base=0, count=n_ctx),
                           jnp.zeros((GLA_DV, LANES), F32))
        lax.fori_loop(0, n_lat, functools.partial(bwd_body, base=n_ctx, count=n_lat), st)

    xor2 = lax.broadcasted_iota(jnp.int32, (c, 2 * c), 0) ^ (lax.broadcasted_iota(jnp.int32, (c, 2 * c), 1) & (c - 1))
    row_l = lax.broadcasted_iota(jnp.int32, (c, LANES), 0)
    lane2v = lax.broadcasted_iota(jnp.int32, (c, 2 * GLA_DV), 1)
    lane2s = lax.broadcasted_iota(jnp.int32, (GLA_DV, 2 * LANES), 1) % LANES

    def stack_heads(x):
        return jnp.concatenate([jnp.where(head_masks[0], x, 0.0), jnp.where(head_masks[1], x, 0.0)], axis=0)

    go = group_size(cpt)

    def out_body(i, carry):
        idx = range(go)
        ns = [tile * cpt + i * go + u for u in idx]
        offs = [pl.multiple_of(n * c, c) for n in ns]
        locs = [pl.multiple_of((i * go + u) * c, c) for u in idx]
        rfs = [_cum(mf_ref[c:, :], laf_ref[0, pl.ds(off, c), :]) for off in offs]
        rbs = [_cum(mb_ref[c:, :], lab_ref[0, pl.ds(off, c), :]) for off in offs]
        qs = [q_ref[0, pl.ds(loc, c), :].astype(F32) for loc in locs]
        ks = [k_ref[0, pl.ds(off, c), :].astype(F32) for off in offs]
        a = [2.0 * _dot_nt(qs[u].astype(BF16), stack_heads(ks[u]).astype(BF16)) for u in idx]
        for lvl in range(_GLA_LEVELS):
            sh = _GLA_LEVELS - 1 - lvl
            second = ((row_l >> sh) & 1) == 1
            for u in idx:
                ef = jnp.exp(rfs[u][(1 + lvl) * c:(2 + lvl) * c])
                eb = jnp.exp(rbs[u][(1 + lvl) * c:(2 + lvl) * c])
                ql = (qs[u] * jnp.where(second, ef, eb)).astype(BF16)
                kl = stack_heads(ks[u] * jnp.where(second, eb, ef)).astype(BF16)
                a[u] = jnp.where((xor2 >> sh) == 1, _dot_nt(ql, kl), a[u])
        for u in idx:
            v = v_ref[0, pl.ds(locs[u], c), :]
            v_bd = jnp.concatenate([jnp.where(lane2v < GLA_DV, v, jnp.zeros_like(v)),
                                    jnp.where(lane2v >= GLA_DV, v, jnp.zeros_like(v))], axis=0)
            q_inter = jnp.concatenate([qs[u] * jnp.exp(rfs[u][0:c]), qs[u] * jnp.exp(rbs[u][0:c])],
                                      axis=-1).astype(BF16)
            states = st_ref[n_ctx + ns[u]]
            st2 = jnp.concatenate([jnp.where(lane2s < GLA_DK, states, 0.0),
                                   jnp.where(lane2s >= GLA_DK, states, 0.0)], axis=0).astype(BF16)
            o_ref[0, pl.ds(locs[u], c), :] = _dot(a[u].astype(BF16), v_bd) + _dot_nt(q_inter, st2)
        return carry

    lax.fori_loop(0, cpt // go, out_body, 0)


def _gla(gq, gk, gv, gvt, laf, lab, gk_c, gvt_c, laf_c, lab_c, mf, mb, *, tile=1024):
    nb, t, _ = gq.shape
    tc = gk_c.shape[1] // nb
    tile = min(tile, t)
    c = GLA_CHUNK
    n_all = (t + tc) // c
    pair_k = 2 * GLA_DK
    pair_v = 2 * GLA_DV
    return pl.pallas_call(
        functools.partial(_gla_kernel, cpt=tile // c),
        grid=(nb, GLA_HEADS // 2, t // tile),
        in_specs=[
            pl.BlockSpec((1, tile, pair_k), lambda b, p, i: (b, i, p)),
            pl.BlockSpec((1, t, pair_k), lambda b, p, i: (b, 0, p)),
            pl.BlockSpec((1, tile, pair_v), lambda b, p, i: (b, i, p)),
            pl.BlockSpec((1, pair_v, t), lambda b, p, i: (b, p, 0)),
            pl.BlockSpec((1, t, pair_k), lambda b, p, i: (b, 0, p)),
            pl.BlockSpec((1, t, pair_k), lambda b, p, i: (b, 0, p)),
            pl.BlockSpec((1, tc, pair_k), lambda b, p, i: (0, b, p)),
            pl.BlockSpec((1, pair_v, tc), lambda b, p, i: (0, p, b)),
            pl.BlockSpec((1, tc, pair_k), lambda b, p, i: (0, b, p)),
            pl.BlockSpec((1, tc, pair_k), lambda b, p, i: (0, b, p)),
            _const_spec(tuple(mf.shape)), _const_spec(tuple(mb.shape)),
        ],
        out_specs=pl.BlockSpec((1, tile, pair_v), lambda b, p, i: (b, i, p)),
        out_shape=jax.ShapeDtypeStruct((nb, t, GLA_V_W), F32),
        scratch_shapes=[pltpu.VMEM((n_all, GLA_DV, 2 * LANES), F32),
                        pltpu.VMEM((n_all, 8, LANES), F32)],
        compiler_params=pltpu.CompilerParams(dimension_semantics=("arbitrary", "arbitrary", "arbitrary"),
                                             vmem_limit_bytes=VMEM_LIMIT),
        name="gla",
    )(gq, gk, gv, gvt, laf, lab, gk_c, gvt_c, laf_c, lab_c, mf, mb)


def _rope_tables(t):
    rows = t // GRID_W
    row = jnp.repeat(jnp.arange(rows, dtype=F32), GRID_W)
    col = jnp.tile(jnp.arange(GRID_W, dtype=F32), rows)
    freqs = ROPE_THETA ** (-jnp.arange(0, ROPE_AXIS_DIM, 2, dtype=F32) / ROPE_AXIS_DIM)
    ang = jnp.concatenate([row[:, None] * freqs, col[:, None] * freqs], axis=-1)
    cos, sin = jnp.cos(ang), jnp.sin(ang)
    reps = LANES // HEAD_DIM
    return (jnp.tile(jnp.concatenate([cos, cos], axis=-1), (1, reps)),
            jnp.tile(jnp.concatenate([-sin, sin], axis=-1), (1, reps)))


def kernel(x, c, ctx, c_ctx, w_mod, b_mod, g_norm, w_ffn_up, w_ffn_down, w_in, g_q, g_k,
           w_decay, b_decay, g_gla, w_branch, b_gate, w_out, g_final):
    nb, t, d = x.shape
    tc = ctx.shape[1]
    f = w_ffn_down.shape[2]
    assert w_mod.shape[0] == 1, "single layer"
    assert t % GLA_CHUNK == 0 and tc % GLA_CHUNK == 0 and t % GRID_W == 0

    rows = -(-(nb + 1) // 8) * 8
    c_rows = jnp.zeros((rows, d), F32).at[:nb].set(c).at[nb].set(c_ctx)
    m = _modulation(c_rows, w_mod[0], b_mod[0]).reshape(rows, N_MOD, d)
    m_lat, m_ctx = m[:nb], m[nb:nb + 1]

    wa = [w_ffn_up[0, i, :, :f].astype(BF16) for i in range(2)]
    wb = [w_ffn_up[0, i, :, f:].astype(BF16) for i in range(2)]
    wd = [w_ffn_down[0, i].astype(BF16) for i in range(2)]
    c_low = _C_GT
    w_low = w_in[0][:, c_low:c_low + 2 * GLA_RANK]
    w_r = jnp.concatenate([w_in[0][:, :c_low], w_in[0][:, c_low + 2 * GLA_RANK:], w_low,
                           jnp.zeros((d, LANES - 2 * GLA_RANK), F32)], axis=-1).astype(BF16)
    w_bd = jnp.zeros((2 * GLA_RANK, 2 * GLA_K_W), F32)
    w_bd = w_bd.at[:GLA_RANK, :GLA_K_W].set(w_decay[0, 0]).at[GLA_RANK:, GLA_K_W:].set(w_decay[0, 1])
    w_bd_hi = w_bd.astype(BF16)
    w_bd_lo = (w_bd - w_bd_hi.astype(F32)).astype(BF16)
    w_dec = jnp.concatenate([w_bd_hi, w_bd_hi, w_bd_lo, jnp.zeros_like(w_bd_hi)], axis=0)
    b_dec = b_decay[0].reshape(1, 2 * GLA_K_W)
    gqk = jnp.concatenate([jnp.tile(g_q[0], ATTN_HEADS), jnp.tile(g_k[0], ATTN_KV_HEADS)]).reshape(1, -1)
    lane = np.arange(LANES)
    seg = jnp.asarray(np.tile((lane[:, None] // HEAD_DIM == lane[None, :] // HEAD_DIM) / HEAD_DIM, (2, 1)), dtype=BF16)
    cos_t, sin_t = _rope_tables(t)
    ones_t, zeros_t = jnp.ones((nb * tc, LANES), F32), jnp.zeros((nb * tc, LANES), F32)
    mf, mb = _gla_constants()

    proj = functools.partial(_in_proj, g=g_norm[0, 1], w_r=w_r, gqk=gqk, seg=seg, w_dec=w_dec,
                             b_dec=b_dec, b_gate=b_gate[0].reshape(1, 2 * d))

    hc = _half_ffn(ctx.reshape(1, nb * tc, d), m_ctx[:, 0:3], g_norm[0, 0], wa[0], wb[0], wd[0])
    pc = proj(hc, m_ctx[:, 3:5], cos_t=ones_t, sin_t=zeros_t)
    _, k_c, vt_c, _, gk_c, _, gvt_c, laf_c, lab_c, _, _ = pc

    h1 = _half_ffn(x, m_lat[:, 0:3], g_norm[0, 0], wa[0], wb[0], wd[0])
    qt, k, vt, gq, gk, gv, gvt, laf, lab, gs, gt = proj(h1, m_lat[:, 3:5], cos_t=cos_t, sin_t=sin_t)
    k_all = jnp.concatenate(
        [k, k_c.reshape(ATTN_KV_HEADS, nb, tc, HEAD_DIM).transpose(1, 0, 2, 3)], axis=2)
    vt_all = jnp.concatenate(
        [vt, vt_c.reshape(ATTN_KV_HEADS, V_ROWS, nb, tc).transpose(2, 0, 1, 3)], axis=3)
    attn_o = _attention(qt, k_all, vt_all, tc)
    gla_o = _gla(gq, gk, gv, gvt, laf, lab, gk_c, gvt_c, laf_c, lab_c, mf, mb)
    return _merge_ffn(h1, attn_o, gla_o, gs, gt, m_lat[:, 5:6], g_gla[0],
                      w_branch[0, 0].astype(BF16), w_branch[0, 1].astype(BF16), w_out[0].astype(BF16),
                      m_lat[:, 6:9], g_norm[0, 2], wa[1], wb[1], wd[1], g_final)
```

```python
import functools

import numpy as np
import jax
import jax.numpy as jnp
from jax import lax
from jax.experimental import pallas as pl
from jax.experimental.pallas import tpu as pltpu

F32 = jnp.float32
BF16 = jnp.bfloat16

EPS = 1e-6
GRID_W = 64
N_MOD = 9
ATTN_HEADS = 8
ATTN_KV_HEADS = 2
HEAD_DIM = 64
ROPE_AXIS_DIM = HEAD_DIM // 2
ROPE_THETA = 10000.0
GLA_HEADS = 4
GLA_DK = 64
GLA_DV = 128
GLA_RANK = 16
GLA_GATE_NORM = 16.0
ATTN_Q_W = ATTN_HEADS * HEAD_DIM
ATTN_KV_W = ATTN_KV_HEADS * HEAD_DIM
GLA_K_W = GLA_HEADS * GLA_DK
GLA_V_W = GLA_HEADS * GLA_DV
Q_GROUP = ATTN_HEADS // ATTN_KV_HEADS

LANES = 128
V_ROWS = 80
GLA_CHUNK = 128
VMEM_LIMIT = 56 * 1024 * 1024


def _dot(a, b):
    return jnp.dot(a, b, preferred_element_type=F32)


def _dot_nt(a, b):
    return lax.dot_general(a, b, (((1,), (1,)), ((), ())), preferred_element_type=F32)


def _sigmoid(x):
    return 1.0 / (1.0 + jnp.exp(-x))


def _split2(x):
    hi = x.astype(BF16)
    lo = (x - hi.astype(F32)).astype(BF16)
    return hi, lo


def _rms(x):
    return x * lax.rsqrt(jnp.mean(x * x, axis=-1, keepdims=True) + EPS)


def _const_spec(shape):
    nd = len(shape)
    return pl.BlockSpec(shape, lambda *_: (0,) * nd, pipeline_mode=pl.Buffered(1))


def _mod_kernel(c_ref, w_ref, b_ref, o_ref):
    c = c_ref[...]
    s_hi, s_lo = _split2(c * _sigmoid(c))
    w_hi, w_lo = _split2(w_ref[...])
    o_ref[...] = _dot(s_hi, w_hi) + _dot(s_hi, w_lo) + _dot(s_lo, w_hi) + b_ref[...]


def _modulation(c_rows, w_mod, b_mod):
    rows, d = c_rows.shape
    n = w_mod.shape[1]
    tn = 1024
    return pl.pallas_call(
        _mod_kernel,
        grid=(n // tn,),
        in_specs=[pl.BlockSpec((rows, d), lambda j: (0, 0)),
                  pl.BlockSpec((d, tn), lambda j: (0, j)),
                  pl.BlockSpec((1, tn), lambda j: (0, j))],
        out_specs=pl.BlockSpec((rows, tn), lambda j: (0, j)),
        out_shape=jax.ShapeDtypeStruct((rows, n), F32),
        compiler_params=pltpu.CompilerParams(dimension_semantics=("arbitrary",),
                                             vmem_limit_bytes=VMEM_LIMIT),
        name="modulation",
    )(c_rows, w_mod, b_mod.reshape(1, n))


def _ffn_math(x, mod_ref, g_ref, wa_ref, wb_ref, wd_ref, n_chunks):
    shift, scale, gate = mod_ref[0, 0:1, :], mod_ref[0, 1:2, :], mod_ref[0, 2:3, :]
    n = ((_rms(x) * g_ref[...]) * (1.0 + scale) + shift).astype(BF16)
    fc = wa_ref.shape[1] // n_chunks
    acc = None
    for c in range(n_chunks):
        a = _dot(n, wa_ref[:, c * fc:(c + 1) * fc])
        b = _dot(n, wb_ref[:, c * fc:(c + 1) * fc])
        act = ((a * _sigmoid(a)) * b).astype(BF16)
        part = _dot(act, wd_ref[c * fc:(c + 1) * fc, :])
        acc = part if acc is None else acc + part
    return x + 0.5 * gate * acc


def _ffn_kernel(h_ref, mod_ref, g_ref, wa_ref, wb_ref, wd_ref, o_ref, *, n_chunks):
    o_ref[0] = _ffn_math(h_ref[0], mod_ref, g_ref, wa_ref, wb_ref, wd_ref, n_chunks)


def _merge_ffn_kernel(h_ref, ao_ref, go_ref, gs_ref, gt_ref, mg_ref, gg_ref, wb0_ref, wb1_ref, wo_ref,
                      mod_ref, g_ref, wa_ref, wb_ref, wd_ref, gf_ref, o_ref, *, n_chunks):
    d = h_ref.shape[2]
    go = go_ref[0]
    normed = [_rms(go[:, h * GLA_DV:(h + 1) * GLA_DV]) * gg_ref[...] for h in range(GLA_HEADS)]
    gn = (jnp.concatenate(normed, axis=-1) * gs_ref[0].astype(F32)).astype(BF16)
    y_attn = _dot(ao_ref[0], wb0_ref[...])
    y_gla = _dot(gn, wb1_ref[...])
    gt = gt_ref[0].astype(F32)
    z = (gt[:, :d] * y_attn + gt[:, d:] * y_gla).astype(BF16)
    h2 = h_ref[0] + mg_ref[0] * _dot(z, wo_ref[...])
    out = _ffn_math(h2, mod_ref, g_ref, wa_ref, wb_ref, wd_ref, n_chunks)
    o_ref[0] = _rms(out) * gf_ref[...]


def _ffn_weight_specs(d, f):
    return [_const_spec((1, d)), _const_spec((d, f)), _const_spec((d, f)), _const_spec((f, d))]


def _half_ffn(h, mod3, g, wa, wb, wd, *, tm=512, n_chunks=1):
    nb, t, d = h.shape
    f = wa.shape[1]
    tm = min(tm, t)
    return pl.pallas_call(
        functools.partial(_ffn_kernel, n_chunks=n_chunks),
        grid=(nb, t // tm),
        in_specs=[pl.BlockSpec((1, tm, d), lambda b, i: (b, i, 0)),
                  pl.BlockSpec((1, 3, d), lambda b, i: (b, 0, 0))] + _ffn_weight_specs(d, f),
        out_specs=pl.BlockSpec((1, tm, d), lambda b, i: (b, i, 0)),
        out_shape=jax.ShapeDtypeStruct((nb, t, d), F32),
        compiler_params=pltpu.CompilerParams(dimension_semantics=("arbitrary", "arbitrary"),
                                             vmem_limit_bytes=VMEM_LIMIT),
        name="half_ffn",
    )(h, mod3, g.reshape(1, d), wa, wb, wd)


def _merge_ffn(h, attn_o, gla_o, gs, gt, m_gate, g_gla, wb0, wb1, wo, mod3, g, wa, wb, wd, g_final,
               *, tm=512, n_chunks=1):
    nb, t, d = h.shape
    f = wa.shape[1]
    tm = min(tm, t)
    tok = lambda w: pl.BlockSpec((1, tm, w), lambda b, i: (b, i, 0))
    return pl.pallas_call(
        functools.partial(_merge_ffn_kernel, n_chunks=n_chunks),
        grid=(nb, t // tm),
        in_specs=[tok(d), tok(ATTN_Q_W), tok(GLA_V_W), tok(GLA_V_W), tok(2 * d),
                  pl.BlockSpec((1, 1, d), lambda b, i: (b, 0, 0)),
                  _const_spec((1, GLA_DV)), _const_spec((ATTN_Q_W, d)), _const_spec((GLA_V_W, d)),
                  _const_spec((d, d)),
                  pl.BlockSpec((1, 3, d), lambda b, i: (b, 0, 0))] + _ffn_weight_specs(d, f)
                 + [_const_spec((1, d))],
        out_specs=tok(d),
        out_shape=jax.ShapeDtypeStruct((nb, t, d), F32),
        compiler_params=pltpu.CompilerParams(dimension_semantics=("arbitrary", "arbitrary"),
                                             vmem_limit_bytes=VMEM_LIMIT),
        name="merge_ffn_final",
    )(h, attn_o, gla_o, gs, gt, m_gate, g_gla.reshape(1, GLA_DV), wb0, wb1, wo,
      mod3, g.reshape(1, d), wa, wb, wd, g_final.reshape(1, d))


_C_Q = 0
_C_K = _C_Q + ATTN_Q_W
_C_V = _C_K + ATTN_KV_W
_C_GQ = _C_V + ATTN_KV_W
_C_GK = _C_GQ + GLA_K_W
_C_GV = _C_GK + GLA_K_W
_C_GS = _C_GV + GLA_V_W
_C_GT = _C_GS + GLA_V_W


def _proj_kernel(h_ref, mod_ref, g_ref, w_ref, gqk_ref, seg_ref, cos_ref, sin_ref, wdec_ref,
                 bdec_ref, bgate_ref,
                 qt_ref, k_ref, vt_ref, gq_ref, gk_ref, gv_ref, gvt_ref, laf_ref, lab_ref, gs_ref, gt_ref):
    d = h_ref.shape[2]
    c_low = _C_GT + 2 * d
    x = h_ref[0]
    shift, scale = mod_ref[0, 0:1, :], mod_ref[0, 1:2, :]
    n = ((_rms(x) * g_ref[...]) * (1.0 + scale) + shift).astype(BF16)
    tm = x.shape[0]

    qk = _dot(n, w_ref[:, _C_Q:_C_V])
    low = _dot(n, w_ref[:, c_low:c_low + LANES])
    gt = _dot(n, w_ref[:, _C_GT:c_low]) + bgate_ref[...]

    seg = seg_ref[...]
    ms = []
    for j in range((ATTN_Q_W + ATTN_KV_W) // LANES):
        sq = qk[:, j * LANES:(j + 1) * LANES]
        hi, lo = _split2(sq * sq)
        ms.append(_dot(jnp.concatenate([hi, lo], axis=-1), seg))
    ms = jnp.concatenate(ms, axis=-1)
    gs = _dot(n, w_ref[:, _C_GS:_C_GT])

    l_hi = low.astype(BF16).astype(F32)
    l_lo = low - l_hi
    packed = l_hi + pltpu.roll(l_lo, 2 * GLA_RANK, 1) + pltpu.roll(l_hi, 4 * GLA_RANK, 1)
    z = _dot(packed.astype(BF16), wdec_ref[...]) + bdec_ref[...]
    gv = _dot(n, w_ref[:, _C_GV:_C_GS])
    vt = _dot(n, w_ref[:, _C_V:_C_GQ]).T
    gqk = _dot(n, w_ref[:, _C_GQ:_C_GV])

    qk = qk * lax.rsqrt(ms + EPS) * gqk_ref[...]
    lane = lax.broadcasted_iota(jnp.int32, (tm, LANES), 1)
    first = (lane % HEAD_DIM) < (HEAD_DIM // 2)
    cos, sin = cos_ref[...], sin_ref[...]
    rot = []
    for j in range((ATTN_Q_W + ATTN_KV_W) // LANES):
        xs = qk[:, j * LANES:(j + 1) * LANES]
        other = jnp.where(first, pltpu.roll(xs, LANES - HEAD_DIM // 2, 1), pltpu.roll(xs, HEAD_DIM // 2, 1))
        rot.append(xs * cos + other * sin)
    q_scale = HEAD_DIM ** -0.5 * float(np.log2(np.e))
    for j in range(ATTN_Q_W // LANES):
        qt_ref[0, j * LANES:(j + 1) * LANES, :] = (rot[j] * q_scale).T.astype(BF16)
    k_rot = rot[ATTN_Q_W // LANES]
    for g in range(ATTN_KV_HEADS):
        k_ref[0, g] = k_rot[:, g * HEAD_DIM:(g + 1) * HEAD_DIM].astype(BF16)

    tail = jnp.where(lax.broadcasted_iota(jnp.int32, (V_ROWS - HEAD_DIM, tm), 0) == 0, 1.0, 0.0)
    for g in range(ATTN_KV_HEADS):
        vt_ref[0, g] = jnp.concatenate([vt[g * HEAD_DIM:(g + 1) * HEAD_DIM], tail], axis=0).astype(BF16)

    gq_ref[0] = (gqk[:, :GLA_K_W] * (GLA_DK ** -0.5)).astype(BF16)
    gk_ref[0] = gqk[:, GLA_K_W:].astype(BF16)
    gv_ref[0] = gv.astype(BF16)
    gvt_ref[0] = gv.T.astype(BF16)

    la = (jnp.minimum(z, 0.0) - jnp.log(1.0 + jnp.exp(-jnp.abs(z)))) * (1.0 / GLA_GATE_NORM)
    laf_ref[0] = la[:, :GLA_K_W]
    lab_ref[0] = la[:, GLA_K_W:]

    gs_ref[0] = (gs * _sigmoid(gs)).astype(BF16)
    gt_ref[0] = _sigmoid(gt).astype(BF16)


def _in_proj(h, mod2, g, w_r, gqk, seg, cos_t, sin_t, w_dec, b_dec, b_gate, *, tm=512):
    nb, t, d = h.shape
    tm = min(tm, t)
    wp = w_r.shape[1]
    tok = lambda w: pl.BlockSpec((1, tm, w), lambda b, i: (b, i, 0))
    out_shape = [
        jax.ShapeDtypeStruct((nb, ATTN_Q_W, t), BF16),
        jax.ShapeDtypeStruct((nb, ATTN_KV_HEADS, t, HEAD_DIM), BF16),
        jax.ShapeDtypeStruct((nb, ATTN_KV_HEADS, V_ROWS, t), BF16),
        jax.ShapeDtypeStruct((nb, t, GLA_K_W), BF16),
        jax.ShapeDtypeStruct((nb, t, GLA_K_W), BF16),
        jax.ShapeDtypeStruct((nb, t, GLA_V_W), BF16),
        jax.ShapeDtypeStruct((nb, GLA_V_W, t), BF16),
        jax.ShapeDtypeStruct((nb, t, GLA_K_W), F32),
        jax.ShapeDtypeStruct((nb, t, GLA_K_W), F32),
        jax.ShapeDtypeStruct((nb, t, GLA_V_W), BF16),
        jax.ShapeDtypeStruct((nb, t, 2 * d), BF16),
    ]
    out_specs = [
        pl.BlockSpec((1, ATTN_Q_W, tm), lambda b, i: (b, 0, i)),
        pl.BlockSpec((1, ATTN_KV_HEADS, tm, HEAD_DIM), lambda b, i: (b, 0, i, 0)),
        pl.BlockSpec((1, ATTN_KV_HEADS, V_ROWS, tm), lambda b, i: (b, 0, 0, i)),
        tok(GLA_K_W), tok(GLA_K_W), tok(GLA_V_W),
        pl.BlockSpec((1, GLA_V_W, tm), lambda b, i: (b, 0, i)),
        tok(GLA_K_W), tok(GLA_K_W), tok(GLA_V_W), tok(2 * d),
    ]
    in_specs = [
        pl.BlockSpec((1, tm, d), lambda b, i: (b, i, 0)),
        pl.BlockSpec((1, 2, d), lambda b, i: (b, 0, 0)),
        _const_spec((1, d)), _const_spec((d, wp)), _const_spec((1, ATTN_Q_W + ATTN_KV_W)),
        _const_spec((2 * LANES, LANES)),
        pl.BlockSpec((tm, LANES), lambda b, i: (i, 0)),
        pl.BlockSpec((tm, LANES), lambda b, i: (i, 0)),
        _const_spec((LANES, 2 * GLA_K_W)),
        _const_spec((1, 2 * GLA_K_W)), _const_spec((1, 2 * d)),
    ]
    return pl.pallas_call(
        _proj_kernel,
        grid=(nb, t // tm),
        in_specs=in_specs,
        out_specs=out_specs,
        out_shape=out_shape,
        compiler_params=pltpu.CompilerParams(dimension_semantics=("arbitrary", "arbitrary"),
                                             vmem_limit_bytes=VMEM_LIMIT),
        name="in_proj",
    )(h, mod2, g.reshape(1, d), w_r, gqk, seg, cos_t, sin_t, w_dec, b_dec, b_gate)


ATTN_GROUPS = 2
ATTN_ROW_BLOCK = 256
ATTN_KC = 1024
ATTN_UNROLL = 3


def _attn_plan(t, tc):
    if tc % (2 * LANES) == 0:
        kc = max(c for c in range(LANES, ATTN_KC + 1, LANES) if t % c == 0)
        edge = tc // 2
        chunks = [(t, edge)] + [(i * kc, kc) for i in range(t // kc)] + [(t + edge, edge)]
        return chunks, (1, len(chunks) - 1)
    s_len = t + tc
    kc = max(c for c in range(LANES, ATTN_KC + 1, LANES) if s_len % c == 0)
    chunks = [(i * kc, kc) for i in range(s_len // kc)]
    return chunks, (0, len(chunks))


def _attn_kernel(qt_ref, k_ref, vt_ref, o_ref, qs_ref, *group_refs, chunks, run):
    tq = qt_ref.shape[2]
    gw = Q_GROUP * tq // ATTN_GROUPS
    per = len(group_refs) // ATTN_GROUPS
    m_refs, cmax_refs, acc_refs, s_refs = (
        [group_refs[g * per + j] for g in range(ATTN_GROUPS)] for j in range(per))
    for h in range(Q_GROUP):
        qs_ref[:, h * tq:(h + 1) * tq] = qt_ref[0, h * HEAD_DIM:(h + 1) * HEAD_DIM, :]
    for g in range(ATTN_GROUPS):
        m_refs[g][...] = jnp.full(m_refs[g].shape, -jnp.inf, F32)
        acc_refs[g][...] = jnp.zeros(acc_refs[g].shape, F32)

    def rows(off, j, rb):
        start = off + j * rb
        return pl.ds(start if isinstance(start, int) else pl.multiple_of(start, LANES), rb)

    def stage(score, apply):
        n_s = n_a = 0
        if score is not None:
            off_s, size_s, gs = score
            rb_s = min(ATTN_ROW_BLOCK, size_s)
            n_s = size_s // rb_s
        if apply is not None:
            off_a, size_a, ga = apply
            rb_a = min(ATTN_ROW_BLOCK, size_a)
            n_a = size_a // rb_a
            m_old = m_refs[ga][...]
            m_new = jnp.maximum(m_old, cmax_refs[ga][...])
            m_refs[ga][...] = m_new
        cmax = pv_sum = None
        for j in range(max(n_s, n_a)):
            if j < n_a:
                p = jnp.exp2(s_refs[ga][j * rb_a:(j + 1) * rb_a, 0:gw] - m_new).astype(BF16)
            if j < n_s:
                s = _dot(k_ref[0, 0, rows(off_s, j, rb_s), :], qs_ref[:, gs * gw:(gs + 1) * gw])
                s_refs[gs][j * rb_s:(j + 1) * rb_s, 0:gw] = s
                bmax = jnp.max(s, axis=0, keepdims=True)
                cmax = bmax if cmax is None else jnp.maximum(cmax, bmax)
            if j < n_a:
                part = _dot(vt_ref[0, 0, :, rows(off_a, j, rb_a)], p)
                pv_sum = part if pv_sum is None else pv_sum + part
        if score is not None:
            cmax_refs[gs][...] = cmax
        if apply is not None:
            acc_refs[ga][...] = jnp.exp2(m_old - m_new) * acc_refs[ga][...] + pv_sum

    last = ATTN_GROUPS - 1

    def chunk_stages(cur, prev):
        stage(cur + (0,), None if prev is None else prev + (last,))
        for g in range(1, ATTN_GROUPS):
            stage(cur + (g,), cur + (g - 1,))

    a, b = run
    trips = b - a - 1
    unroll = ATTN_UNROLL if trips >= ATTN_UNROLL else 1
    first_loop = a + 1 + trips % unroll
    for i in range(first_loop):
        chunk_stages(chunks[i], chunks[i - 1] if i else None)
    if b > first_loop:
        off0, kc = chunks[first_loop]

        def body(i, carry):
            for u in range(unroll):
                off = off0 + (i * unroll + u) * kc
                chunk_stages((off, kc), (off - kc, kc))
            return carry

        lax.fori_loop(0, (b - first_loop) // unroll, body, 0)
    for i in range(b, len(chunks)):
        chunk_stages(chunks[i], chunks[i - 1])
    stage(None, chunks[-1] + (last,))

    acc = jnp.concatenate([r[...] for r in acc_refs], axis=1)
    ot = acc[:HEAD_DIM] / acc[HEAD_DIM:HEAD_DIM + 1]
    for j in range(Q_GROUP // 2):
        pair = jnp.concatenate([ot[:, (2 * j) * tq:(2 * j + 1) * tq], ot[:, (2 * j + 1) * tq:(2 * j + 2) * tq]],
                               axis=0)
        o_ref[0, :, j * LANES:(j + 1) * LANES] = pair.T.astype(o_ref.dtype)


def _attention(qt, k, vt, t_ctx, *, tq=512):
    nb, _, t = qt.shape
    s_len = k.shape[2]
    tq = min(tq, t)
    chunks, run = _attn_plan(t, t_ctx)
    kc = max(size for _, size in chunks)
    gw = Q_GROUP * tq // ATTN_GROUPS
    return pl.pallas_call(
        functools.partial(_attn_kernel, chunks=tuple(chunks), run=run),
        grid=(nb, ATTN_KV_HEADS, t // tq),
        in_specs=[
            pl.BlockSpec((1, Q_GROUP * HEAD_DIM, tq), lambda b, g, i: (b, g, i)),
            pl.BlockSpec((1, 1, s_len, HEAD_DIM), lambda b, g, i: (b, g, 0, 0)),
            pl.BlockSpec((1, 1, V_ROWS, s_len), lambda b, g, i: (b, g, 0, 0)),
        ],
        out_specs=pl.BlockSpec((1, tq, Q_GROUP * HEAD_DIM), lambda b, g, i: (b, i, g)),
        out_shape=jax.ShapeDtypeStruct((nb, t, ATTN_Q_W), BF16),
        scratch_shapes=[pltpu.VMEM((HEAD_DIM, Q_GROUP * tq), BF16)] + ATTN_GROUPS * [
            pltpu.VMEM((1, gw), F32),
            pltpu.VMEM((1, gw), F32),
            pltpu.VMEM((V_ROWS, gw), F32),
            pltpu.VMEM((kc, gw + LANES), F32)],
        compiler_params=pltpu.CompilerParams(dimension_semantics=("arbitrary", "arbitrary", "arbitrary"),
                                             vmem_limit_bytes=VMEM_LIMIT),
        name="flash_attention",
    )(qt, k, vt)


def _gla_constants():
    c = GLA_CHUNK
    i = np.arange(c)[:, None]
    t = np.arange(c)[None, :]
    fwd = [t > i, t <= i]
    bwd = [t < i, t >= i]
    s = c // 2
    while s >= 1:
        mid = (i // (2 * s)) * (2 * s) + s
        second = (i % (2 * s)) >= s
        fwd.append(np.where(second, (t >= mid) & (t <= i), (t > i) & (t < mid)))
        bwd.append(np.where(second, (t >= mid) & (t < i), (t >= i) & (t < mid)))
        s //= 2
    to = lambda blocks: jnp.asarray(np.concatenate(blocks, axis=0).astype(np.float32), dtype=BF16)
    return to(fwd), to(bwd)


_GLA_LEVELS = int(np.log2(GLA_CHUNK))


def _cum(mat, la):
    hi, lo = _split2(la)
    r = _dot(mat, jnp.concatenate([hi, lo], axis=-1))
    return r[:, :LANES] + r[:, LANES:]


def _gla_kernel(q_ref, k_ref, v_ref, vt_ref, laf_ref, lab_ref, kc_ref, vtc_ref, lafc_ref, labc_ref,
                mf_ref, mb_ref, o_ref, st_ref, dec_ref, *, cpt):
    c = GLA_CHUNK
    n_lat = k_ref.shape[1] // c
    n_ctx = kc_ref.shape[1] // c
    n_all = n_ctx + n_lat
    tile = pl.program_id(2)
    lane = lax.broadcasted_iota(jnp.int32, (c, LANES), 1)
    head_masks = [lane < GLA_DK, lane >= GLA_DK]
    lane2 = lax.broadcasted_iota(jnp.int32, (c, 2 * LANES), 1) % LANES
    pair_masks = [lane2 < GLA_DK, lane2 >= GLA_DK]

    def increments(k_r, vt_r, laf_r, lab_r, n0, slot0, count):
        offs = [pl.multiple_of((n0 + i) * c, c) for i in range(count)]
        rfs = [_cum(mf_ref[0:2 * c, :], laf_r[0, pl.ds(off, c), :]) for off in offs]
        rbs = [_cum(mb_ref[0:2 * c, :], lab_r[0, pl.ds(off, c), :]) for off in offs]
        for i, (off, rf, rb) in enumerate(zip(offs, rfs, rbs)):
            k = k_r[0, pl.ds(off, c), :].astype(F32)
            kfb = jnp.concatenate([k * jnp.exp(rf[0:c]),
                                   k * jnp.exp(rb[0:c])], axis=-1)
            kk = jnp.concatenate([jnp.where(pair_masks[0], kfb, 0.0), jnp.where(pair_masks[1], kfb, 0.0)],
                                 axis=0).astype(BF16)
            vt2 = jnp.concatenate([vt_r[0, 0:GLA_DV, pl.ds(off, c)], vt_r[0, GLA_DV:2 * GLA_DV, pl.ds(off, c)]],
                                  axis=-1)
            st_ref[slot0 + i] = _dot(vt2, kk)
            dec_ref[slot0 + i, 0:1, :] = jnp.exp(rf[2 * c - 1:2 * c, :])
            dec_ref[slot0 + i, 1:2, :] = jnp.exp(rb[c:c + 1, :])

    def group_size(n):
        return max(g for g in (4, 2, 1) if n % g == 0)

    @pl.when(tile == 0)
    def _():
        gc, gl = group_size(n_ctx), group_size(n_lat)

        def ctx_body(i, carry):
            increments(kc_ref, vtc_ref, lafc_ref, labc_ref, i * gc, i * gc, gc)
            return carry
        lax.fori_loop(0, n_ctx // gc, ctx_body, 0)

        def lat_body(i, carry):
            increments(k_ref, vt_ref, laf_ref, lab_ref, i * gl, n_ctx + i * gl, gl)
            return carry
        lax.fori_loop(0, n_lat // gl, lat_body, 0)

        def fwd_body(s, st):
            inc = st_ref[s, :, 0:LANES]
            st_ref[s, :, 0:LANES] = st
            return st * dec_ref[s, 0:1, :] + inc
        lax.fori_loop(0, n_all, fwd_body, jnp.zeros((GLA_DV, LANES), F32))

        def bwd_body(j, st, base, count):
            s = base + count - 1 - j
            inc = st_ref[s, :, LANES:2 * LANES]
            st_ref[s, :, LANES:2 * LANES] = st
            return st * dec_ref[s, 1:2, :] + inc
        st = lax.fori_loop(0, n_ctx, functools.partial(bwd_body, base=0, count=n_ctx),
                           jnp.zeros((GLA_DV, LANES), F32))
        lax.fori_loop(0, n_lat, functools.partial(bwd_body, base=n_ctx, count=n_lat), st)

    xor2 = lax.broadcasted_iota(jnp.int32, (c, 2 * c), 0) ^ (lax.broadcasted_iota(jnp.int32, (c, 2 * c), 1) & (c - 1))
    row_l = lax.broadcasted_iota(jnp.int32, (c, LANES), 0)
    lane2v = lax.broadcasted_iota(jnp.int32, (c, 2 * GLA_DV), 1)
    lane2s = lax.broadcasted_iota(jnp.int32, (GLA_DV, 2 * LANES), 1) % LANES

    def stack_heads(x):
        return jnp.concatenate([jnp.where(head_masks[0], x, 0.0), jnp.where(head_masks[1], x, 0.0)], axis=0)

    go = group_size(cpt)

    def out_body(i, carry):
        idx = range(go)
        ns = [tile * cpt + i * go + u for u in idx]
        offs = [pl.multiple_of(n * c, c) for n in ns]
        locs = [pl.multiple_of((i * go + u) * c, c) for u in idx]
        rfs = [_cum(mf_ref[c:, :], laf_ref[0, pl.ds(off, c), :]) for off in offs]
        rbs = [_cum(mb_ref[c:, :], lab_ref[0, pl.ds(off, c), :]) for off in offs]
        qs = [q_ref[0, pl.ds(loc, c), :].astype(F32) for loc in locs]
        ks = [k_ref[0, pl.ds(off, c), :].astype(F32) for off in offs]
        a = [2.0 * _dot_nt(qs[u].astype(BF16), stack_heads(ks[u]).astype(BF16)) for u in idx]
        for lvl in range(_GLA_LEVELS):
            sh = _GLA_LEVELS - 1 - lvl
            second = ((row_l >> sh) & 1) == 1
            for u in idx:
                ef = jnp.exp(rfs[u][(1 + lvl) * c:(2 + lvl) * c])
                eb = jnp.exp(rbs[u][(1 + lvl) * c:(2 + lvl) * c])
                ql = (qs[u] * jnp.where(second, ef, eb)).astype(BF16)
                kl = stack_heads(ks[u] * jnp.where(second, eb, ef)).astype(BF16)
                a[u] = jnp.where((xor2 >> sh) == 1, _dot_nt(ql, kl), a[u])
        for u in idx:
            v = v_ref[0, pl.ds(locs[u], c), :]
            v_bd = jnp.concatenate([jnp.where(lane2v < GLA_DV, v, jnp.zeros_like(v)),
                                    jnp.where(lane2v >= GLA_DV, v, jnp.zeros_like(v))], axis=0)
            q_inter = jnp.concatenate([qs[u] * jnp.exp(rfs[u][0:c]), qs[u] * jnp.exp(rbs[u][0:c])],
                                      axis=-1).astype(BF16)
            states = st_ref[n_ctx + ns[u]]
            st2 = jnp.concatenate([jnp.where(lane2s < GLA_DK, states, 0.0),
                                   jnp.where(lane2s >= GLA_DK, states, 0.0)], axis=0).astype(BF16)
            o_ref[0, pl.ds(locs[u], c), :] = _dot(a[u].astype(BF16), v_bd) + _dot_nt(q_inter, st2)
        return carry

    lax.fori_loop(0, cpt // go, out_body, 0)


def _gla(gq, gk, gv, gvt, laf, lab, gk_c, gvt_c, laf_c, lab_c, mf, mb, *, tile=1024):
    nb, t, _ = gq.shape
    tc = gk_c.shape[1] // nb
    tile = min(tile, t)
    c = GLA_CHUNK
    n_all = (t + tc) // c
    pair_k = 2 * GLA_DK
    pair_v = 2 * GLA_DV
    return pl.pallas_call(
        functools.partial(_gla_kernel, cpt=tile // c),
        grid=(nb, GLA_HEADS // 2, t // tile),
        in_specs=[
            pl.BlockSpec((1, tile, pair_k), lambda b, p, i: (b, i, p)),
            pl.BlockSpec((1, t, pair_k), lambda b, p, i: (b, 0, p)),
            pl.BlockSpec((1, tile, pair_v), lambda b, p, i: (b, i, p)),
            pl.BlockSpec((1, pair_v, t), lambda b, p, i: (b, p, 0)),
            pl.BlockSpec((1, t, pair_k), lambda b, p, i: (b, 0, p)),
            pl.BlockSpec((1, t, pair_k), lambda b, p, i: (b, 0, p)),
            pl.BlockSpec((1, tc, pair_k), lambda b, p, i: (0, b, p)),
            pl.BlockSpec((1, pair_v, tc), lambda b, p, i: (0, p, b)),
            pl.BlockSpec((1, tc, pair_k), lambda b, p, i: (0, b, p)),
            pl.BlockSpec((1, tc, pair_k), lambda b, p, i: (0, b, p)),
            _const_spec(tuple(mf.shape)), _const_spec(tuple(mb.shape)),
        ],
        out_specs=pl.BlockSpec((1, tile, pair_v), lambda b, p, i: (b, i, p)),
        out_shape=jax.ShapeDtypeStruct((nb, t, GLA_V_W), F32),
        scratch_shapes=[pltpu.VMEM((n_all, GLA_DV, 2 * LANES), F32),
                        pltpu.VMEM((n_all, 8, LANES), F32)],
        compiler_params=pltpu.CompilerParams(dimension_semantics=("arbitrary", "arbitrary", "arbitrary"),
                                             vmem_limit_bytes=VMEM_LIMIT),
        name="gla",
    )(gq, gk, gv, gvt, laf, lab, gk_c, gvt_c, laf_c, lab_c, mf, mb)


def _rope_tables(t):
    rows = t // GRID_W
    row = jnp.repeat(jnp.arange(rows, dtype=F32), GRID_W)
    col = jnp.tile(jnp.arange(GRID_W, dtype=F32), rows)
    freqs = ROPE_THETA ** (-jnp.arange(0, ROPE_AXIS_DIM, 2, dtype=F32) / ROPE_AXIS_DIM)
    ang = jnp.concatenate([row[:, None] * freqs, col[:, None] * freqs], axis=-1)
    cos, sin = jnp.cos(ang), jnp.sin(ang)
    reps = LANES // HEAD_DIM
    return (jnp.tile(jnp.concatenate([cos, cos], axis=-1), (1, reps)),
            jnp.tile(jnp.concatenate([-sin, sin], axis=-1), (1, reps)))


def kernel(x, c, ctx, c_ctx, w_mod, b_mod, g_norm, w_ffn_up, w_ffn_down, w_in, g_q, g_k,
           w_decay, b_decay, g_gla, w_branch, b_gate, w_out, g_final):
    nb, t, d = x.shape
    tc = ctx.shape[1]
    f = w_ffn_down.shape[2]
    assert w_mod.shape[0] == 1, "single layer"
    assert t % GLA_CHUNK == 0 and tc % GLA_CHUNK == 0 and t % GRID_W == 0

    rows = -(-(nb + 1) // 8) * 8
    c_rows = jnp.zeros((rows, d), F32).at[:nb].set(c).at[nb].set(c_ctx)
    m = _modulation(c_rows, w_mod[0], b_mod[0]).reshape(rows, N_MOD, d)
    m_lat, m_ctx = m[:nb], m[nb:nb + 1]

    wa = [w_ffn_up[0, i, :, :f].astype(BF16) for i in range(2)]
    wb = [w_ffn_up[0, i, :, f:].astype(BF16) for i in range(2)]
    wd = [w_ffn_down[0, i].astype(BF16) for i in range(2)]
    c_low = _C_GT
    w_low = w_in[0][:, c_low:c_low + 2 * GLA_RANK]
    w_r = jnp.concatenate([w_in[0][:, :c_low], w_in[0][:, c_low + 2 * GLA_RANK:], w_low,
                           jnp.zeros((d, LANES - 2 * GLA_RANK), F32)], axis=-1).astype(BF16)
    w_bd = jnp.zeros((2 * GLA_RANK, 2 * GLA_K_W), F32)
    w_bd = w_bd.at[:GLA_RANK, :GLA_K_W].set(w_decay[0, 0]).at[GLA_RANK:, GLA_K_W:].set(w_decay[0, 1])
    w_bd_hi = w_bd.astype(BF16)
    w_bd_lo = (w_bd - w_bd_hi.astype(F32)).astype(BF16)
    w_dec = jnp.concatenate([w_bd_hi, w_bd_hi, w_bd_lo, jnp.zeros_like(w_bd_hi)], axis=0)
    b_dec = b_decay[0].reshape(1, 2 * GLA_K_W)
    gqk = jnp.concatenate([jnp.tile(g_q[0], ATTN_HEADS), jnp.tile(g_k[0], ATTN_KV_HEADS)]).reshape(1, -1)
    lane = np.arange(LANES)
    seg = jnp.asarray(np.tile((lane[:, None] // HEAD_DIM == lane[None, :] // HEAD_DIM) / HEAD_DIM, (2, 1)), dtype=BF16)
    cos_t, sin_t = _rope_tables(t)
    ones_t, zeros_t = jnp.ones((nb * tc, LANES), F32), jnp.zeros((nb * tc, LANES), F32)
    mf, mb = _gla_constants()

    proj = functools.partial(_in_proj, g=g_norm[0, 1], w_r=w_r, gqk=gqk, seg=seg, w_dec=w_dec,
                             b_dec=b_dec, b_gate=b_gate[0].reshape(1, 2 * d))

    hc = _half_ffn(ctx.reshape(1, nb * tc, d), m_ctx[:, 0:3], g_norm[0, 0], wa[0], wb[0], wd[0])
    pc = proj(hc, m_ctx[:, 3:5], cos_t=ones_t, sin_t=zeros_t)
    _, k_c, vt_c, _, gk_c, _, gvt_c, laf_c, lab_c, _, _ = pc

    h1 = _half_ffn(x, m_lat[:, 0:3], g_norm[0, 0], wa[0], wb[0], wd[0])
    qt, k, vt, gq, gk, gv, gvt, laf, lab, gs, gt = proj(h1, m_lat[:, 3:5], cos_t=cos_t, sin_t=sin_t)
    k_all = jnp.concatenate(
        [k, k_c.reshape(ATTN_KV_HEADS, nb, tc, HEAD_DIM).transpose(1, 0, 2, 3)], axis=2)
    vt_all = jnp.concatenate(
        [vt, vt_c.reshape(ATTN_KV_HEADS, V_ROWS, nb, tc).transpose(2, 0, 1, 3)], axis=3)
    attn_o = _attention(qt, k_all, vt_all, tc)
    gla_o = _gla(gq, gk, gv, gvt, laf, lab, gk_c, gvt_c, laf_c, lab_c, mf, mb)
    return _merge_ffn(h1, attn_o, gla_o, gs, gt, m_lat[:, 5:6], g_gla[0],
                      w_branch[0, 0].astype(BF16), w_branch[0, 1].astype(BF16), w_out[0].astype(BF16),
                      m_lat[:, 6:9], g_norm[0, 2], wa[1], wb[1], wd[1], g_final)
```

```python
import functools

import numpy as np
import jax
import jax.numpy as jnp
from jax import lax
from jax.experimental import pallas as pl
from jax.experimental.pallas import tpu as pltpu

F32 = jnp.float32
BF16 = jnp.bfloat16

EPS = 1e-6
GRID_W = 64
N_MOD = 9
ATTN_HEADS = 8
ATTN_KV_HEADS = 2
HEAD_DIM = 64
ROPE_AXIS_DIM = HEAD_DIM // 2
ROPE_THETA = 10000.0
GLA_HEADS = 4
GLA_DK = 64
GLA_DV = 128
GLA_RANK = 16
GLA_GATE_NORM = 16.0
ATTN_Q_W = ATTN_HEADS * HEAD_DIM
ATTN_KV_W = ATTN_KV_HEADS * HEAD_DIM
GLA_K_W = GLA_HEADS * GLA_DK
GLA_V_W = GLA_HEADS * GLA_DV
Q_GROUP = ATTN_HEADS // ATTN_KV_HEADS

LANES = 128
V_ROWS = 80
GLA_CHUNK = 128
VMEM_LIMIT = 56 * 1024 * 1024


def _dot(a, b):
    return jnp.dot(a, b, preferred_element_type=F32)


def _dot_nt(a, b):
    return lax.dot_general(a, b, (((1,), (1,)), ((), ())), preferred_element_type=F32)


def _sigmoid(x):
    return 1.0 / (1.0 + jnp.exp(-x))


def _split2(x):
    hi = x.astype(BF16)
    lo = (x - hi.astype(F32)).astype(BF16)
    return hi, lo


def _rms(x):
    return x * lax.rsqrt(jnp.mean(x * x, axis=-1, keepdims=True) + EPS)


def _const_spec(shape):
    nd = len(shape)
    return pl.BlockSpec(shape, lambda *_: (0,) * nd, pipeline_mode=pl.Buffered(1))


def _mod_kernel(c_ref, w_ref, b_ref, o_ref):
    c = c_ref[...]
    s_hi, s_lo = _split2(c * _sigmoid(c))
    w_hi, w_lo = _split2(w_ref[...])
    o_ref[...] = _dot(s_hi, w_hi) + _dot(s_hi, w_lo) + _dot(s_lo, w_hi) + b_ref[...]


def _modulation(c_rows, w_mod, b_mod):
    rows, d = c_rows.shape
    n = w_mod.shape[1]
    tn = 1024
    return pl.pallas_call(
        _mod_kernel,
        grid=(n // tn,),
        in_specs=[pl.BlockSpec((rows, d), lambda j: (0, 0)),
                  pl.BlockSpec((d, tn), lambda j: (0, j)),
                  pl.BlockSpec((1, tn), lambda j: (0, j))],
        out_specs=pl.BlockSpec((rows, tn), lambda j: (0, j)),
        out_shape=jax.ShapeDtypeStruct((rows, n), F32),
        compiler_params=pltpu.CompilerParams(dimension_semantics=("arbitrary",),
                                             vmem_limit_bytes=VMEM_LIMIT),
        name="modulation",
    )(c_rows, w_mod, b_mod.reshape(1, n))


def _ffn_math(x, mod_ref, g_ref, wa_ref, wb_ref, wd_ref, n_chunks):
    shift, scale, gate = mod_ref[0, 0:1, :], mod_ref[0, 1:2, :], mod_ref[0, 2:3, :]
    n = ((_rms(x) * g_ref[...]) * (1.0 + scale) + shift).astype(BF16)
    fc = wa_ref.shape[1] // n_chunks
    acc = None
    for c in range(n_chunks):
        a = _dot(n, wa_ref[:, c * fc:(c + 1) * fc])
        b = _dot(n, wb_ref[:, c * fc:(c + 1) * fc])
        act = ((a * _sigmoid(a)) * b).astype(BF16)
        part = _dot(act, wd_ref[c * fc:(c + 1) * fc, :])
        acc = part if acc is None else acc + part
    return x + 0.5 * gate * acc


def _ffn_kernel(h_ref, mod_ref, g_ref, wa_ref, wb_ref, wd_ref, o_ref, *, n_chunks):
    o_ref[0] = _ffn_math(h_ref[0], mod_ref, g_ref, wa_ref, wb_ref, wd_ref, n_chunks)


def _merge_ffn_kernel(h_ref, ao_ref, go_ref, gs_ref, gt_ref, mg_ref, gg_ref, wb0_ref, wb1_ref, wo_ref,
                      mod_ref, g_ref, wa_ref, wb_ref, wd_ref, gf_ref, o_ref, *, n_chunks):
    d = h_ref.shape[2]
    go = go_ref[0]
    normed = [_rms(go[:, h * GLA_DV:(h + 1) * GLA_DV]) * gg_ref[...] for h in range(GLA_HEADS)]
    gn = (jnp.concatenate(normed, axis=-1) * gs_ref[0].astype(F32)).astype(BF16)
    y_attn = _dot(ao_ref[0], wb0_ref[...])
    y_gla = _dot(gn, wb1_ref[...])
    gt = gt_ref[0].astype(F32)
    z = (gt[:, :d] * y_attn + gt[:, d:] * y_gla).astype(BF16)
    h2 = h_ref[0] + mg_ref[0] * _dot(z, wo_ref[...])
    out = _ffn_math(h2, mod_ref, g_ref, wa_ref, wb_ref, wd_ref, n_chunks)
    o_ref[0] = _rms(out) * gf_ref[...]


def _ffn_weight_specs(d, f):
    return [_const_spec((1, d)), _const_spec((d, f)), _const_spec((d, f)), _const_spec((f, d))]


def _half_ffn(h, mod3, g, wa, wb, wd, *, tm=512, n_chunks=1):
    nb, t, d = h.shape
    f = wa.shape[1]
    tm = min(tm, t)
    return pl.pallas_call(
        functools.partial(_ffn_kernel, n_chunks=n_chunks),
        grid=(nb, t // tm),
        in_specs=[pl.BlockSpec((1, tm, d), lambda b, i: (b, i, 0)),
                  pl.BlockSpec((1, 3, d), lambda b, i: (b, 0, 0))] + _ffn_weight_specs(d, f),
        out_specs=pl.BlockSpec((1, tm, d), lambda b, i: (b, i, 0)),
        out_shape=jax.ShapeDtypeStruct((nb, t, d), F32),
        compiler_params=pltpu.CompilerParams(dimension_semantics=("arbitrary", "arbitrary"),
                                             vmem_limit_bytes=VMEM_LIMIT),
        name="half_ffn",
    )(h, mod3, g.reshape(1, d), wa, wb, wd)


def _merge_ffn(h, attn_o, gla_o, gs, gt, m_gate, g_gla, wb0, wb1, wo, mod3, g, wa, wb, wd, g_final,
               *, tm=512, n_chunks=1):
    nb, t, d = h.shape
    f = wa.shape[1]
    tm = min(tm, t)
    tok = lambda w: pl.BlockSpec((1, tm, w), lambda b, i: (b, i, 0))
    return pl.pallas_call(
        functools.partial(_merge_ffn_kernel, n_chunks=n_chunks),
        grid=(nb, t // tm),
        in_specs=[tok(d), tok(ATTN_Q_W), tok(GLA_V_W), tok(GLA_V_W), tok(2 * d),
                  pl.BlockSpec((1, 1, d), lambda b, i: (b, 0, 0)),
                  _const_spec((1, GLA_DV)), _const_spec((ATTN_Q_W, d)), _const_spec((GLA_V_W, d)),
                  _const_spec((d, d)),
                  pl.BlockSpec((1, 3, d), lambda b, i: (b, 0, 0))] + _ffn_weight_specs(d, f)
                 + [_const_spec((1, d))],
        out_specs=tok(d),
        out_shape=jax.ShapeDtypeStruct((nb, t, d), F32),
        compiler_params=pltpu.CompilerParams(dimension_semantics=("arbitrary", "arbitrary"),
                                             vmem_limit_bytes=VMEM_LIMIT),
        name="merge_ffn_final",
    )(h, attn_o, gla_o, gs, gt, m_gate, g_gla.reshape(1, GLA_DV), wb0, wb1, wo,
      mod3, g.reshape(1, d), wa, wb, wd, g_final.reshape(1, d))


_C_Q = 0
_C_K = _C_Q + ATTN_Q_W
_C_V = _C_K + ATTN_KV_W
_C_GQ = _C_V + ATTN_KV_W
_C_GK = _C_GQ + GLA_K_W
_C_GV = _C_GK + GLA_K_W
_C_GS = _C_GV + GLA_V_W
_C_GT = _C_GS + GLA_V_W


def _proj_kernel(h_ref, mod_ref, g_ref, w_ref, gqk_ref, seg_ref, cos_ref, sin_ref, wdec_ref,
                 bdec_ref, bgate_ref,
                 qt_ref, k_ref, vt_ref, gq_ref, gk_ref, gv_ref, gvt_ref, laf_ref, lab_ref, gs_ref, gt_ref):
    d = h_ref.shape[2]
    c_low = _C_GT + 2 * d
    x = h_ref[0]
    shift, scale = mod_ref[0, 0:1, :], mod_ref[0, 1:2, :]
    n = ((_rms(x) * g_ref[...]) * (1.0 + scale) + shift).astype(BF16)
    tm = x.shape[0]

    qk = _dot(n, w_ref[:, _C_Q:_C_V])
    low = _dot(n, w_ref[:, c_low:c_low + LANES])
    gt = _dot(n, w_ref[:, _C_GT:c_low]) + bgate_ref[...]

    seg = seg_ref[...]
    ms = []
    for j in range((ATTN_Q_W + ATTN_KV_W) // LANES):
        sq = qk[:, j * LANES:(j + 1) * LANES]
        hi, lo = _split2(sq * sq)
        ms.append(_dot(jnp.concatenate([hi, lo], axis=-1), seg))
    ms = jnp.concatenate(ms, axis=-1)
    gs = _dot(n, w_ref[:, _C_GS:_C_GT])

    l_hi = low.astype(BF16).astype(F32)
    l_lo = low - l_hi
    packed = l_hi + pltpu.roll(l_lo, 2 * GLA_RANK, 1) + pltpu.roll(l_hi, 4 * GLA_RANK, 1)
    z = _dot(packed.astype(BF16), wdec_ref[...]) + bdec_ref[...]
    gv = _dot(n, w_ref[:, _C_GV:_C_GS])
    vt = _dot(n, w_ref[:, _C_V:_C_GQ]).T
    gqk = _dot(n, w_ref[:, _C_GQ:_C_GV])

    qk = qk * lax.rsqrt(ms + EPS) * gqk_ref[...]
    lane = lax.broadcasted_iota(jnp.int32, (tm, LANES), 1)
    first = (lane % HEAD_DIM) < (HEAD_DIM // 2)
    cos, sin = cos_ref[...], sin_ref[...]
    rot = []
    for j in range((ATTN_Q_W + ATTN_KV_W) // LANES):
        xs = qk[:, j * LANES:(j + 1) * LANES]
        other = jnp.where(first, pltpu.roll(xs, LANES - HEAD_DIM // 2, 1), pltpu.roll(xs, HEAD_DIM // 2, 1))
        rot.append(xs * cos + other * sin)
    q_scale = HEAD_DIM ** -0.5 * float(np.log2(np.e))
    for j in range(ATTN_Q_W // LANES):
        qt_ref[0, j * LANES:(j + 1) * LANES, :] = (rot[j] * q_scale).T.astype(BF16)
    k_rot = rot[ATTN_Q_W // LANES]
    ones_lane = jnp.where(lane == HEAD_DIM, 1.0, 0.0)
    for g in range(ATTN_KV_HEADS):
        k_head = k_rot if g == 0 else pltpu.roll(k_rot, LANES - g * HEAD_DIM, 1)
        k_ref[0, g] = jnp.where(lane < HEAD_DIM, k_head, ones_lane).astype(BF16)

    tail = jnp.where(lax.broadcasted_iota(jnp.int32, (V_ROWS - HEAD_DIM, tm), 0) == 0, 1.0, 0.0)
    for g in range(ATTN_KV_HEADS):
        vt_ref[0, g] = jnp.concatenate([vt[g * HEAD_DIM:(g + 1) * HEAD_DIM], tail], axis=0).astype(BF16)

    gq_ref[0] = (gqk[:, :GLA_K_W] * (GLA_DK ** -0.5)).astype(BF16)
    gk_ref[0] = gqk[:, GLA_K_W:].astype(BF16)
    gv_ref[0] = gv.astype(BF16)
    gvt_ref[0] = gv.T.astype(BF16)

    la = (jnp.minimum(z, 0.0) - jnp.log(1.0 + jnp.exp(-jnp.abs(z)))) * (1.0 / GLA_GATE_NORM)
    laf_ref[0] = la[:, :GLA_K_W]
    lab_ref[0] = la[:, GLA_K_W:]

    gs_ref[0] = (gs * _sigmoid(gs)).astype(BF16)
    gt_ref[0] = _sigmoid(gt).astype(BF16)


def _in_proj(h, mod2, g, w_r, gqk, seg, cos_t, sin_t, w_dec, b_dec, b_gate, *, tm=512):
    nb, t, d = h.shape
    tm = min(tm, t)
    wp = w_r.shape[1]
    tok = lambda w: pl.BlockSpec((1, tm, w), lambda b, i: (b, i, 0))
    out_shape = [
        jax.ShapeDtypeStruct((nb, ATTN_Q_W, t), BF16),
        jax.ShapeDtypeStruct((nb, ATTN_KV_HEADS, t, LANES), BF16),
        jax.ShapeDtypeStruct((nb, ATTN_KV_HEADS, V_ROWS, t), BF16),
        jax.ShapeDtypeStruct((nb, t, GLA_K_W), BF16),
        jax.ShapeDtypeStruct((nb, t, GLA_K_W), BF16),
        jax.ShapeDtypeStruct((nb, t, GLA_V_W), BF16),
        jax.ShapeDtypeStruct((nb, GLA_V_W, t), BF16),
        jax.ShapeDtypeStruct((nb, t, GLA_K_W), F32),
        jax.ShapeDtypeStruct((nb, t, GLA_K_W), F32),
        jax.ShapeDtypeStruct((nb, t, GLA_V_W), BF16),
        jax.ShapeDtypeStruct((nb, t, 2 * d), BF16),
    ]
    out_specs = [
        pl.BlockSpec((1, ATTN_Q_W, tm), lambda b, i: (b, 0, i)),
        pl.BlockSpec((1, ATTN_KV_HEADS, tm, LANES), lambda b, i: (b, 0, i, 0)),
        pl.BlockSpec((1, ATTN_KV_HEADS, V_ROWS, tm), lambda b, i: (b, 0, 0, i)),
        tok(GLA_K_W), tok(GLA_K_W), tok(GLA_V_W),
        pl.BlockSpec((1, GLA_V_W, tm), lambda b, i: (b, 0, i)),
        tok(GLA_K_W), tok(GLA_K_W), tok(GLA_V_W), tok(2 * d),
    ]
    in_specs = [
        pl.BlockSpec((1, tm, d), lambda b, i: (b, i, 0)),
        pl.BlockSpec((1, 2, d), lambda b, i: (b, 0, 0)),
        _const_spec((1, d)), _const_spec((d, wp)), _const_spec((1, ATTN_Q_W + ATTN_KV_W)),
        _const_spec((2 * LANES, LANES)),
        pl.BlockSpec((tm, LANES), lambda b, i: (i, 0)),
        pl.BlockSpec((tm, LANES), lambda b, i: (i, 0)),
        _const_spec((LANES, 2 * GLA_K_W)),
        _const_spec((1, 2 * GLA_K_W)), _const_spec((1, 2 * d)),
    ]
    return pl.pallas_call(
        _proj_kernel,
        grid=(nb, t // tm),
        in_specs=in_specs,
        out_specs=out_specs,
        out_shape=out_shape,
        compiler_params=pltpu.CompilerParams(dimension_semantics=("arbitrary", "arbitrary"),
                                             vmem_limit_bytes=VMEM_LIMIT),
        name="in_proj",
    )(h, mod2, g.reshape(1, d), w_r, gqk, seg, cos_t, sin_t, w_dec, b_dec, b_gate)


ATTN_GROUPS = 2
ATTN_ROW_BLOCK = 256
ATTN_KC = 1024
ATTN_UNROLL = 3
ATTN_BLOCKS_PER_ITER = 12
ATTN_MAX_SHIFT = 50.0


def _attn_plan(t, tc):
    if tc % (2 * LANES) == 0:
        kc = max(c for c in range(LANES, ATTN_KC + 1, LANES) if t % c == 0)
        edge = tc // 2
        chunks = [(t, edge)] + [(i * kc, kc) for i in range(t // kc)] + [(t + edge, edge)]
        return chunks, (1, len(chunks) - 1)
    s_len = t + tc
    kc = max(c for c in range(LANES, ATTN_KC + 1, LANES) if s_len % c == 0)
    chunks = [(i * kc, kc) for i in range(s_len // kc)]
    return chunks, (0, len(chunks))


def _attn_kernel(par_ref, qt_ref, k_ref, vt_ref, o_ref, qs_ref, *group_refs, chunks, run):
    tq = qt_ref.shape[2]
    s_len = k_ref.shape[2]
    gw = Q_GROUP * tq // ATTN_GROUPS
    per = len(group_refs) // ATTN_GROUPS
    m_refs, cmax_refs, acc_refs, s_refs = (
        [group_refs[g * per + j] for g in range(ATTN_GROUPS)] for j in range(per))
    for h in range(Q_GROUP):
        qs_ref[0:HEAD_DIM, h * tq:(h + 1) * tq] = qt_ref[0, h * HEAD_DIM:(h + 1) * HEAD_DIM, :]
    extra = lax.broadcasted_iota(jnp.int32, (LANES - HEAD_DIM, Q_GROUP * tq), 0)
    qs_ref[HEAD_DIM:LANES, :] = jnp.where(extra == 0, -par_ref[1], 0.0).astype(BF16)
    for g in range(ATTN_GROUPS):
        acc_refs[g][...] = jnp.zeros(acc_refs[g].shape, F32)

    def rows(off, j, rb):
        start = off + j * rb
        return pl.ds(start if isinstance(start, int) else pl.multiple_of(start, LANES), rb)

    def bounded_scores():
        rb = ATTN_ROW_BLOCK if s_len % ATTN_ROW_BLOCK == 0 else LANES
        n_blocks = s_len // rb
        per_iter = max(d for d in range(1, ATTN_BLOCKS_PER_ITER + 1) if n_blocks % d == 0)

        def body(it, carry):
            sums = [None] * ATTN_GROUPS
            pending = None

            def finish(unit):
                blk, g, p = unit
                part = _dot(vt_ref[0, 0, :, blk], p)
                sums[g] = part if sums[g] is None else sums[g] + part

            for j in range(per_iter):
                blk = rows(it * (per_iter * rb), j, rb)
                for g in range(ATTN_GROUPS):
                    s = _dot(k_ref[0, 0, blk, :], qs_ref[:, g * gw:(g + 1) * gw])
                    if pending is not None:
                        finish(pending)
                    pending = (blk, g, jnp.exp2(s).astype(BF16))
            finish(pending)
            for g in range(ATTN_GROUPS):
                acc_refs[g][...] += sums[g]
            return carry

        lax.fori_loop(0, n_blocks // per_iter, body, 0)

    def online_softmax():
        for g in range(ATTN_GROUPS):
            m_refs[g][...] = jnp.full(m_refs[g].shape, -jnp.inf, F32)

        def stage(score, apply):
            n_s = n_a = 0
            if score is not None:
                off_s, size_s, gs = score
                rb_s = min(ATTN_ROW_BLOCK, size_s)
                n_s = size_s // rb_s
            if apply is not None:
                off_a, size_a, ga = apply
                rb_a = min(ATTN_ROW_BLOCK, size_a)
                n_a = size_a // rb_a
                m_old = m_refs[ga][...]
                m_new = jnp.maximum(m_old, cmax_refs[ga][...])
                m_refs[ga][...] = m_new
            cmax = pv_sum = None
            for j in range(max(n_s, n_a)):
                if j < n_a:
                    p = jnp.exp2(s_refs[ga][j * rb_a:(j + 1) * rb_a, :] - m_new).astype(BF16)
                if j < n_s:
                    s = _dot(k_ref[0, 0, rows(off_s, j, rb_s), :], qs_ref[:, gs * gw:(gs + 1) * gw])
                    s_refs[gs][j * rb_s:(j + 1) * rb_s, :] = s
                    bmax = jnp.max(s, axis=0, keepdims=True)
                    cmax = bmax if cmax is None else jnp.maximum(cmax, bmax)
                if j < n_a:
                    part = _dot(vt_ref[0, 0, :, rows(off_a, j, rb_a)], p)
                    pv_sum = part if pv_sum is None else pv_sum + part
            if score is not None:
                cmax_refs[gs][...] = cmax
            if apply is not None:
                acc_refs[ga][...] = jnp.exp2(m_old - m_new) * acc_refs[ga][...] + pv_sum

        last = ATTN_GROUPS - 1

        def chunk_stages(cur, prev):
            stage(cur + (0,), None if prev is None else prev + (last,))
            for g in range(1, ATTN_GROUPS):
                stage(cur + (g,), cur + (g - 1,))

        a, b = run
        trips = b - a - 1
        unroll = ATTN_UNROLL if trips >= ATTN_UNROLL else 1
        first_loop = a + 1 + trips % unroll
        for i in range(first_loop):
            chunk_stages(chunks[i], chunks[i - 1] if i else None)
        if b > first_loop:
            off0, kc = chunks[first_loop]

            def body(i, carry):
                for u in range(unroll):
                    off = off0 + (i * unroll + u) * kc
                    chunk_stages((off, kc), (off - kc, kc))
                return carry

            lax.fori_loop(0, (b - first_loop) // unroll, body, 0)
        for i in range(b, len(chunks)):
            chunk_stages(chunks[i], chunks[i - 1])
        stage(None, chunks[-1] + (last,))

    pl.when(par_ref[0] > 0.0)(bounded_scores)
    pl.when(par_ref[0] <= 0.0)(online_softmax)

    acc = jnp.concatenate([r[...] for r in acc_refs], axis=1)
    ot = acc[:HEAD_DIM] / acc[HEAD_DIM:HEAD_DIM + 1]
    for j in range(Q_GROUP // 2):
        pair = jnp.concatenate([ot[:, (2 * j) * tq:(2 * j + 1) * tq], ot[:, (2 * j + 1) * tq:(2 * j + 2) * tq]],
                               axis=0)
        o_ref[0, :, j * LANES:(j + 1) * LANES] = pair.T.astype(o_ref.dtype)


def _attention(par, qt, k, vt, t_ctx, *, tq=512):
    nb, _, t = qt.shape
    s_len = k.shape[2]
    tq = min(tq, t)
    chunks, run = _attn_plan(t, t_ctx)
    kc = max(size for _, size in chunks)
    gw = Q_GROUP * tq // ATTN_GROUPS
    return pl.pallas_call(
        functools.partial(_attn_kernel, chunks=tuple(chunks), run=run),
        grid=(nb, ATTN_KV_HEADS, t // tq),
        in_specs=[
            pl.BlockSpec(memory_space=pltpu.SMEM),
            pl.BlockSpec((1, Q_GROUP * HEAD_DIM, tq), lambda b, g, i: (b, g, i)),
            pl.BlockSpec((1, 1, s_len, LANES), lambda b, g, i: (b, g, 0, 0)),
            pl.BlockSpec((1, 1, V_ROWS, s_len), lambda b, g, i: (b, g, 0, 0)),
        ],
        out_specs=pl.BlockSpec((1, tq, Q_GROUP * HEAD_DIM), lambda b, g, i: (b, i, g)),
        out_shape=jax.ShapeDtypeStruct((nb, t, ATTN_Q_W), BF16),
        scratch_shapes=[pltpu.VMEM((LANES, Q_GROUP * tq), BF16)] + ATTN_GROUPS * [
            pltpu.VMEM((1, gw), F32),
            pltpu.VMEM((1, gw), F32),
            pltpu.VMEM((V_ROWS, gw), F32),
            pltpu.VMEM((kc, gw), F32)],
        compiler_params=pltpu.CompilerParams(dimension_semantics=("arbitrary", "arbitrary", "arbitrary"),
                                             vmem_limit_bytes=VMEM_LIMIT),
        name="flash_attention",
    )(par, qt, k, vt)


def _gla_constants():
    c = GLA_CHUNK
    i = np.arange(c)[:, None]
    t = np.arange(c)[None, :]
    fwd = [t > i, t <= i]
    bwd = [t < i, t >= i]
    s = c // 2
    while s >= 1:
        mid = (i // (2 * s)) * (2 * s) + s
        second = (i % (2 * s)) >= s
        fwd.append(np.where(second, (t >= mid) & (t <= i), (t > i) & (t < mid)))
        bwd.append(np.where(second, (t >= mid) & (t < i), (t >= i) & (t < mid)))
        s //= 2
    to = lambda blocks: jnp.asarray(np.concatenate(blocks, axis=0).astype(np.float32), dtype=BF16)
    return to(fwd), to(bwd)


_GLA_LEVELS = int(np.log2(GLA_CHUNK))


def _cum(mat, la):
    hi, lo = _split2(la)
    r = _dot(mat, jnp.concatenate([hi, lo], axis=-1))
    return r[:, :LANES] + r[:, LANES:]


def _gla_kernel(q_ref, k_ref, v_ref, vt_ref, laf_ref, lab_ref, kc_ref, vtc_ref, lafc_ref, labc_ref,
                mf_ref, mb_ref, o_ref, st_ref, dec_ref, *, cpt):
    c = GLA_CHUNK
    n_lat = k_ref.shape[1] // c
    n_ctx = kc_ref.shape[1] // c
    n_all = n_ctx + n_lat
    tile = pl.program_id(2)
    lane = lax.broadcasted_iota(jnp.int32, (c, LANES), 1)
    head_masks = [lane < GLA_DK, lane >= GLA_DK]
    lane2 = lax.broadcasted_iota(jnp.int32, (c, 2 * LANES), 1) % LANES
    pair_masks = [lane2 < GLA_DK, lane2 >= GLA_DK]

    def increments(k_r, vt_r, laf_r, lab_r, n0, slot0, count):
        offs = [pl.multiple_of((n0 + i) * c, c) for i in range(count)]
        rfs = [_cum(mf_ref[0:2 * c, :], laf_r[0, pl.ds(off, c), :]) for off in offs]
        rbs = [_cum(mb_ref[0:2 * c, :], lab_r[0, pl.ds(off, c), :]) for off in offs]
        for i, (off, rf, rb) in enumerate(zip(offs, rfs, rbs)):
            k = k_r[0, pl.ds(off, c), :].astype(F32)
            kfb = jnp.concatenate([k * jnp.exp(rf[0:c]),
                                   k * jnp.exp(rb[0:c])], axis=-1)
            kk = jnp.concatenate([jnp.where(pair_masks[0], kfb, 0.0), jnp.where(pair_masks[1], kfb, 0.0)],
                                 axis=0).astype(BF16)
            vt2 = jnp.concatenate([vt_r[0, 0:GLA_DV, pl.ds(off, c)], vt_r[0, GLA_DV:2 * GLA_DV, pl.ds(off, c)]],
                                  axis=-1)
            st_ref[slot0 + i] = _dot(vt2, kk)
            dec_ref[slot0 + i, 0:1, :] = jnp.exp(rf[2 * c - 1:2 * c, :])
            dec_ref[slot0 + i, 1:2, :] = jnp.exp(rb[c:c + 1, :])

    def group_size(n):
        return max(g for g in (4, 2, 1) if n % g == 0)

    @pl.when(tile == 0)
    def _():
        gc, gl = group_size(n_ctx), group_size(n_lat)

        def ctx_body(i, carry):
            increments(kc_ref, vtc_ref, lafc_ref, labc_ref, i * gc, i * gc, gc)
            return carry
        lax.fori_loop(0, n_ctx // gc, ctx_body, 0)

        def lat_body(i, carry):
            increments(k_ref, vt_ref, laf_ref, lab_ref, i * gl, n_ctx + i * gl, gl)
            return carry
        lax.fori_loop(0, n_lat // gl, lat_body, 0)

        def fwd_body(s, st):
            inc = st_ref[s, :, 0:LANES]
            st_ref[s, :, 0:LANES] = st
            return st * dec_ref[s, 0:1, :] + inc
        lax.fori_loop(0, n_all, fwd_body, jnp.zeros((GLA_DV, LANES), F32))

        def bwd_body(j, st, base, count):
            s = base + count - 1 - j
            inc = st_ref[s, :, LANES:2 * LANES]
            st_ref[s, :, LANES:2 * LANES] = st
            return st * dec_ref[s, 1:2, :] + inc
        st = lax.fori_loop(0, n_ctx, functools.partial(bwd_body, base=0, count=n_ctx),
                           jnp.zeros((GLA_DV, LANES), F32))
        lax.fori_loop(0, n_lat, functools.partial(bwd_body, base=n_ctx, count=n_lat), st)

    xor2 = lax.broadcasted_iota(jnp.int32, (c, 2 * c), 0) ^ (lax.broadcasted_iota(jnp.int32, (c, 2 * c), 1) & (c - 1))
    row_l = lax.broadcasted_iota(jnp.int32, (c, LANES), 0)
    lane2v = lax.broadcasted_iota(jnp.int32, (c, 2 * GLA_DV), 1)
    lane2s = lax.broadcasted_iota(jnp.int32, (GLA_DV, 2 * LANES), 1) % LANES

    def stack_heads(x):
        return jnp.concatenate([jnp.where(head_masks[0], x, 0.0), jnp.where(head_masks[1], x, 0.0)], axis=0)

    go = group_size(cpt)

    def out_body(i, carry):
        idx = range(go)
        ns = [tile * cpt + i * go + u for u in idx]
        offs = [pl.multiple_of(n * c, c) for n in ns]
        locs = [pl.multiple_of((i * go + u) * c, c) for u in idx]
        rfs = [_cum(mf_ref[c:, :], laf_ref[0, pl.ds(off, c), :]) for off in offs]
        rbs = [_cum(mb_ref[c:, :], lab_ref[0, pl.ds(off, c), :]) for off in offs]
        qs = [q_ref[0, pl.ds(loc, c), :].astype(F32) for loc in locs]
        ks = [k_ref[0, pl.ds(off, c), :].astype(F32) for off in offs]
        a = [2.0 * _dot_nt(qs[u].astype(BF16), stack_heads(ks[u]).astype(BF16)) for u in idx]
        for lvl in range(_GLA_LEVELS):
            sh = _GLA_LEVELS - 1 - lvl
            second = ((row_l >> sh) & 1) == 1
            for u in idx:
                ef = jnp.exp(rfs[u][(1 + lvl) * c:(2 + lvl) * c])
                eb = jnp.exp(rbs[u][(1 + lvl) * c:(2 + lvl) * c])
                ql = (qs[u] * jnp.where(second, ef, eb)).astype(BF16)
                kl = stack_heads(ks[u] * jnp.where(second, eb, ef)).astype(BF16)
                a[u] = jnp.where((xor2 >> sh) == 1, _dot_nt(ql, kl), a[u])
        for u in idx:
            v = v_ref[0, pl.ds(locs[u], c), :]
            v_bd = jnp.concatenate([jnp.where(lane2v < GLA_DV, v, jnp.zeros_like(v)),
                                    jnp.where(lane2v >= GLA_DV, v, jnp.zeros_like(v))], axis=0)
            q_inter = jnp.concatenate([qs[u] * jnp.exp(rfs[u][0:c]), qs[u] * jnp.exp(rbs[u][0:c])],
                                      axis=-1).astype(BF16)
            states = st_ref[n_ctx + ns[u]]
            st2 = jnp.concatenate([jnp.where(lane2s < GLA_DK, states, 0.0),
                                   jnp.where(lane2s >= GLA_DK, states, 0.0)], axis=0).astype(BF16)
            o_ref[0, pl.ds(locs[u], c), :] = _dot(a[u].astype(BF16), v_bd) + _dot_nt(q_inter, st2)
        return carry

    lax.fori_loop(0, cpt // go, out_body, 0)


def _gla(gq, gk, gv, gvt, laf, lab, gk_c, gvt_c, laf_c, lab_c, mf, mb, *, tile=1024):
    nb, t, _ = gq.shape
    tc = gk_c.shape[1] // nb
    tile = min(tile, t)
    c = GLA_CHUNK
    n_all = (t + tc) // c
    pair_k = 2 * GLA_DK
    pair_v = 2 * GLA_DV
    return pl.pallas_call(
        functools.partial(_gla_kernel, cpt=tile // c),
        grid=(nb, GLA_HEADS // 2, t // tile),
        in_specs=[
            pl.BlockSpec((1, tile, pair_k), lambda b, p, i: (b, i, p)),
            pl.BlockSpec((1, t, pair_k), lambda b, p, i: (b, 0, p)),
            pl.BlockSpec((1, tile, pair_v), lambda b, p, i: (b, i, p)),
            pl.BlockSpec((1, pair_v, t), lambda b, p, i: (b, p, 0)),
            pl.BlockSpec((1, t, pair_k), lambda b, p, i: (b, 0, p)),
            pl.BlockSpec((1, t, pair_k), lambda b, p, i: (b, 0, p)),
            pl.BlockSpec((1, tc, pair_k), lambda b, p, i: (0, b, p)),
            pl.BlockSpec((1, pair_v, tc), lambda b, p, i: (0, p, b)),
            pl.BlockSpec((1, tc, pair_k), lambda b, p, i: (0, b, p)),
            pl.BlockSpec((1, tc, pair_k), lambda b, p, i: (0, b, p)),
            _const_spec(tuple(mf.shape)), _const_spec(tuple(mb.shape)),
        ],
        out_specs=pl.BlockSpec((1, tile, pair_v), lambda b, p, i: (b, i, p)),
        out_shape=jax.ShapeDtypeStruct((nb, t, GLA_V_W), F32),
        scratch_shapes=[pltpu.VMEM((n_all, GLA_DV, 2 * LANES), F32),
                        pltpu.VMEM((n_all, 8, LANES), F32)],
        compiler_params=pltpu.CompilerParams(dimension_semantics=("arbitrary", "arbitrary", "arbitrary"),
                                             vmem_limit_bytes=VMEM_LIMIT),
        name="gla",
    )(gq, gk, gv, gvt, laf, lab, gk_c, gvt_c, laf_c, lab_c, mf, mb)


def _rope_tables(t):
    rows = t // GRID_W
    row = jnp.repeat(jnp.arange(rows, dtype=F32), GRID_W)
    col = jnp.tile(jnp.arange(GRID_W, dtype=F32), rows)
    freqs = ROPE_THETA ** (-jnp.arange(0, ROPE_AXIS_DIM, 2, dtype=F32) / ROPE_AXIS_DIM)
    ang = jnp.concatenate([row[:, None] * freqs, col[:, None] * freqs], axis=-1)
    cos, sin = jnp.cos(ang), jnp.sin(ang)
    reps = LANES // HEAD_DIM
    return (jnp.tile(jnp.concatenate([cos, cos], axis=-1), (1, reps)),
            jnp.tile(jnp.concatenate([-sin, sin], axis=-1), (1, reps)))


def kernel(x, c, ctx, c_ctx, w_mod, b_mod, g_norm, w_ffn_up, w_ffn_down, w_in, g_q, g_k,
           w_decay, b_decay, g_gla, w_branch, b_gate, w_out, g_final):
    nb, t, d = x.shape
    tc = ctx.shape[1]
    f = w_ffn_down.shape[2]
    assert w_mod.shape[0] == 1, "single layer"
    assert t % GLA_CHUNK == 0 and tc % GLA_CHUNK == 0 and t % GRID_W == 0

    rows = -(-(nb + 1) // 8) * 8
    c_rows = jnp.zeros((rows, d), F32).at[:nb].set(c).at[nb].set(c_ctx)
    m = _modulation(c_rows, w_mod[0], b_mod[0]).reshape(rows, N_MOD, d)
    m_lat, m_ctx = m[:nb], m[nb:nb + 1]

    wa = [w_ffn_up[0, i, :, :f].astype(BF16) for i in range(2)]
    wb = [w_ffn_up[0, i, :, f:].astype(BF16) for i in range(2)]
    wd = [w_ffn_down[0, i].astype(BF16) for i in range(2)]
    c_low = _C_GT
    w_low = w_in[0][:, c_low:c_low + 2 * GLA_RANK]
    w_r = jnp.concatenate([w_in[0][:, :c_low], w_in[0][:, c_low + 2 * GLA_RANK:], w_low,
                           jnp.zeros((d, LANES - 2 * GLA_RANK), F32)], axis=-1).astype(BF16)
    w_bd = jnp.zeros((2 * GLA_RANK, 2 * GLA_K_W), F32)
    w_bd = w_bd.at[:GLA_RANK, :GLA_K_W].set(w_decay[0, 0]).at[GLA_RANK:, GLA_K_W:].set(w_decay[0, 1])
    w_bd_hi = w_bd.astype(BF16)
    w_bd_lo = (w_bd - w_bd_hi.astype(F32)).astype(BF16)
    w_dec = jnp.concatenate([w_bd_hi, w_bd_hi, w_bd_lo, jnp.zeros_like(w_bd_hi)], axis=0)
    b_dec = b_decay[0].reshape(1, 2 * GLA_K_W)
    gqk = jnp.concatenate([jnp.tile(g_q[0], ATTN_HEADS), jnp.tile(g_k[0], ATTN_KV_HEADS)]).reshape(1, -1)
    lane = np.arange(LANES)
    seg = jnp.asarray(np.tile((lane[:, None] // HEAD_DIM == lane[None, :] // HEAD_DIM) / HEAD_DIM, (2, 1)), dtype=BF16)
    cos_t, sin_t = _rope_tables(t)
    ones_t, zeros_t = jnp.ones((nb * tc, LANES), F32), jnp.zeros((nb * tc, LANES), F32)
    mf, mb = _gla_constants()

    proj = functools.partial(_in_proj, g=g_norm[0, 1], w_r=w_r, gqk=gqk, seg=seg, w_dec=w_dec,
                             b_dec=b_dec, b_gate=b_gate[0].reshape(1, 2 * d))

    hc = _half_ffn(ctx.reshape(1, nb * tc, d), m_ctx[:, 0:3], g_norm[0, 0], wa[0], wb[0], wd[0])
    pc = proj(hc, m_ctx[:, 3:5], cos_t=ones_t, sin_t=zeros_t)
    _, k_c, vt_c, _, gk_c, _, gvt_c, laf_c, lab_c, _, _ = pc

    h1 = _half_ffn(x, m_lat[:, 0:3], g_norm[0, 0], wa[0], wb[0], wd[0])
    qt, k, vt, gq, gk, gv, gvt, laf, lab, gs, gt = proj(h1, m_lat[:, 3:5], cos_t=cos_t, sin_t=sin_t)
    k_all = jnp.concatenate(
        [k, k_c.reshape(ATTN_KV_HEADS, nb, tc, LANES).transpose(1, 0, 2, 3)], axis=2)
    vt_all = jnp.concatenate(
        [vt, vt_c.reshape(ATTN_KV_HEADS, V_ROWS, nb, tc).transpose(2, 0, 1, 3)], axis=3)
    bound = jnp.ceil(1.01 * HEAD_DIM ** 0.5 * float(np.log2(np.e)) * jnp.max(jnp.abs(g_q[0])) * jnp.max(jnp.abs(g_k[0])))
    bounded = bound <= ATTN_MAX_SHIFT
    par = jnp.stack([bounded.astype(F32), jnp.where(bounded, bound, 0.0)])
    attn_o = _attention(par, qt, k_all, vt_all, tc)
    gla_o = _gla(gq, gk, gv, gvt, laf, lab, gk_c, gvt_c, laf_c, lab_c, mf, mb)
    return _merge_ffn(h1, attn_o, gla_o, gs, gt, m_lat[:, 5:6], g_gla[0],
                      w_branch[0, 0].astype(BF16), w_branch[0, 1].astype(BF16), w_out[0].astype(BF16),
                      m_lat[:, 6:9], g_norm[0, 2], wa[1], wb[1], wd[1], g_final)
```

```python
import functools

import numpy as np
import jax
import jax.numpy as jnp
from jax import lax
from jax.experimental import pallas as pl
from jax.experimental.pallas import tpu as pltpu

F32 = jnp.float32
BF16 = jnp.bfloat16

EPS = 1e-6
GRID_W = 64
N_MOD = 9
ATTN_HEADS = 8
ATTN_KV_HEADS = 2
HEAD_DIM = 64
ROPE_AXIS_DIM = HEAD_DIM // 2
ROPE_THETA = 10000.0
GLA_HEADS = 4
GLA_DK = 64
GLA_DV = 128
GLA_RANK = 16
GLA_GATE_NORM = 16.0
ATTN_Q_W = ATTN_HEADS * HEAD_DIM
ATTN_KV_W = ATTN_KV_HEADS * HEAD_DIM
GLA_K_W = GLA_HEADS * GLA_DK
GLA_V_W = GLA_HEADS * GLA_DV
Q_GROUP = ATTN_HEADS // ATTN_KV_HEADS

LANES = 128
GLA_CHUNK = 128
VMEM_LIMIT = 56 * 1024 * 1024


def _dot(a, b):
    return jnp.dot(a, b, preferred_element_type=F32)


def _dot_nt(a, b):
    return lax.dot_general(a, b, (((1,), (1,)), ((), ())), preferred_element_type=F32)


def _sigmoid(x):
    return 1.0 / (1.0 + jnp.exp(-x))


def _split2(x):
    hi = x.astype(BF16)
    lo = (x - hi.astype(F32)).astype(BF16)
    return hi, lo


def _rms(x):
    return x * lax.rsqrt(jnp.mean(x * x, axis=-1, keepdims=True) + EPS)


def _const_spec(shape):
    nd = len(shape)
    return pl.BlockSpec(shape, lambda *_: (0,) * nd, pipeline_mode=pl.Buffered(1))


def _mod_kernel(c_ref, w_ref, b_ref, o_ref):
    c = c_ref[...]
    s_hi, s_lo = _split2(c * _sigmoid(c))
    w_hi, w_lo = _split2(w_ref[...])
    o_ref[...] = _dot(s_hi, w_hi) + _dot(s_hi, w_lo) + _dot(s_lo, w_hi) + b_ref[...]


def _modulation(c_rows, w_mod, b_mod):
    rows, d = c_rows.shape
    n = w_mod.shape[1]
    tn = 1024
    return pl.pallas_call(
        _mod_kernel,
        grid=(n // tn,),
        in_specs=[pl.BlockSpec((rows, d), lambda j: (0, 0)),
                  pl.BlockSpec((d, tn), lambda j: (0, j)),
                  pl.BlockSpec((1, tn), lambda j: (0, j))],
        out_specs=pl.BlockSpec((rows, tn), lambda j: (0, j)),
        out_shape=jax.ShapeDtypeStruct((rows, n), F32),
        compiler_params=pltpu.CompilerParams(dimension_semantics=("arbitrary",),
                                             vmem_limit_bytes=VMEM_LIMIT),
        name="modulation",
    )(c_rows, w_mod, b_mod.reshape(1, n))


def _ffn_math(x, mod_ref, g_ref, wa_ref, wb_ref, wd_ref, n_chunks):
    shift, scale, gate = mod_ref[0, 0:1, :], mod_ref[0, 1:2, :], mod_ref[0, 2:3, :]
    n = ((_rms(x) * g_ref[...]) * (1.0 + scale) + shift).astype(BF16)
    fc = wa_ref.shape[1] // n_chunks
    acc = None
    for c in range(n_chunks):
        a = _dot(n, wa_ref[:, c * fc:(c + 1) * fc])
        b = _dot(n, wb_ref[:, c * fc:(c + 1) * fc])
        act = ((a * _sigmoid(a)) * b).astype(BF16)
        part = _dot(act, wd_ref[c * fc:(c + 1) * fc, :])
        acc = part if acc is None else acc + part
    return x + 0.5 * gate * acc


def _ffn_kernel(h_ref, mod_ref, g_ref, wa_ref, wb_ref, wd_ref, o_ref, *, n_chunks):
    o_ref[0] = _ffn_math(h_ref[0], mod_ref, g_ref, wa_ref, wb_ref, wd_ref, n_chunks)


def _merge_ffn_kernel(h_ref, ao_ref, go_ref, gs_ref, gt_ref, mg_ref, gg_ref, wb0_ref, wb1_ref, wo_ref,
                      mod_ref, g_ref, wa_ref, wb_ref, wd_ref, gf_ref, o_ref, *, n_chunks):
    d = h_ref.shape[2]
    go = go_ref[0]
    normed = [_rms(go[:, h * GLA_DV:(h + 1) * GLA_DV]) * gg_ref[...] for h in range(GLA_HEADS)]
    gn = (jnp.concatenate(normed, axis=-1) * gs_ref[0].astype(F32)).astype(BF16)
    y_attn = _dot(ao_ref[0], wb0_ref[...])
    y_gla = _dot(gn, wb1_ref[...])
    gt = gt_ref[0].astype(F32)
    z = (gt[:, :d] * y_attn + gt[:, d:] * y_gla).astype(BF16)
    h2 = h_ref[0] + mg_ref[0] * _dot(z, wo_ref[...])
    out = _ffn_math(h2, mod_ref, g_ref, wa_ref, wb_ref, wd_ref, n_chunks)
    o_ref[0] = _rms(out) * gf_ref[...]


def _ffn_weight_specs(d, f):
    return [_const_spec((1, d)), _const_spec((d, f)), _const_spec((d, f)), _const_spec((f, d))]


def _half_ffn(h, mod3, g, wa, wb, wd, *, tm=512, n_chunks=1):
    nb, t, d = h.shape
    f = wa.shape[1]
    tm = min(tm, t)
    return pl.pallas_call(
        functools.partial(_ffn_kernel, n_chunks=n_chunks),
        grid=(nb, t // tm),
        in_specs=[pl.BlockSpec((1, tm, d), lambda b, i: (b, i, 0)),
                  pl.BlockSpec((1, 3, d), lambda b, i: (b, 0, 0))] + _ffn_weight_specs(d, f),
        out_specs=pl.BlockSpec((1, tm, d), lambda b, i: (b, i, 0)),
        out_shape=jax.ShapeDtypeStruct((nb, t, d), F32),
        compiler_params=pltpu.CompilerParams(dimension_semantics=("arbitrary", "arbitrary"),
                                             vmem_limit_bytes=VMEM_LIMIT),
        name="half_ffn",
    )(h, mod3, g.reshape(1, d), wa, wb, wd)


def _merge_ffn(h, attn_o, gla_o, gs, gt, m_gate, g_gla, wb0, wb1, wo, mod3, g, wa, wb, wd, g_final,
               *, tm=512, n_chunks=1):
    nb, t, d = h.shape
    f = wa.shape[1]
    tm = min(tm, t)
    tok = lambda w: pl.BlockSpec((1, tm, w), lambda b, i: (b, i, 0))
    return pl.pallas_call(
        functools.partial(_merge_ffn_kernel, n_chunks=n_chunks),
        grid=(nb, t // tm),
        in_specs=[tok(d), tok(ATTN_Q_W), tok(GLA_V_W), tok(GLA_V_W), tok(2 * d),
                  pl.BlockSpec((1, 1, d), lambda b, i: (b, 0, 0)),
                  _const_spec((1, GLA_DV)), _const_spec((ATTN_Q_W, d)), _const_spec((GLA_V_W, d)),
                  _const_spec((d, d)),
                  pl.BlockSpec((1, 3, d), lambda b, i: (b, 0, 0))] + _ffn_weight_specs(d, f)
                 + [_const_spec((1, d))],
        out_specs=tok(d),
        out_shape=jax.ShapeDtypeStruct((nb, t, d), F32),
        compiler_params=pltpu.CompilerParams(dimension_semantics=("arbitrary", "arbitrary"),
                                             vmem_limit_bytes=VMEM_LIMIT),
        name="merge_ffn_final",
    )(h, attn_o, gla_o, gs, gt, m_gate, g_gla.reshape(1, GLA_DV), wb0, wb1, wo,
      mod3, g.reshape(1, d), wa, wb, wd, g_final.reshape(1, d))


_C_Q = 0
_C_K = _C_Q + ATTN_Q_W
_C_V = _C_K + ATTN_KV_W
_C_GQ = _C_V + ATTN_KV_W
_C_GK = _C_GQ + GLA_K_W
_C_GV = _C_GK + GLA_K_W
_C_GS = _C_GV + GLA_V_W
_C_GT = _C_GS + GLA_V_W


def _proj_kernel(h_ref, mod_ref, g_ref, w_ref, gqk_ref, seg_ref, cos_ref, sin_ref, wdec_ref,
                 bdec_ref, bgate_ref,
                 qt_ref, k_ref, vt_ref, gq_ref, gk_ref, gv_ref, gvt_ref, laf_ref, lab_ref, gs_ref, gt_ref):
    d = h_ref.shape[2]
    c_low = _C_GT + 2 * d
    x = h_ref[0]
    shift, scale = mod_ref[0, 0:1, :], mod_ref[0, 1:2, :]
    n = ((_rms(x) * g_ref[...]) * (1.0 + scale) + shift).astype(BF16)
    tm = x.shape[0]

    qk = _dot(n, w_ref[:, _C_Q:_C_V])
    low = _dot(n, w_ref[:, c_low:c_low + LANES])
    gt = _dot(n, w_ref[:, _C_GT:c_low]) + bgate_ref[...]

    seg = seg_ref[...]
    ms = []
    for j in range((ATTN_Q_W + ATTN_KV_W) // LANES):
        sq = qk[:, j * LANES:(j + 1) * LANES]
        hi, lo = _split2(sq * sq)
        ms.append(_dot(jnp.concatenate([hi, lo], axis=-1), seg))
    ms = jnp.concatenate(ms, axis=-1)
    gs = _dot(n, w_ref[:, _C_GS:_C_GT])

    l_hi = low.astype(BF16).astype(F32)
    l_lo = low - l_hi
    packed = l_hi + pltpu.roll(l_lo, 2 * GLA_RANK, 1) + pltpu.roll(l_hi, 4 * GLA_RANK, 1)
    z = _dot(packed.astype(BF16), wdec_ref[...]) + bdec_ref[...]
    gv = _dot(n, w_ref[:, _C_GV:_C_GS])
    vt = _dot(n, w_ref[:, _C_V:_C_GQ]).T
    gqk = _dot(n, w_ref[:, _C_GQ:_C_GV])

    qk = qk * lax.rsqrt(ms + EPS) * gqk_ref[...]
    lane = lax.broadcasted_iota(jnp.int32, (tm, LANES), 1)
    first = (lane % HEAD_DIM) < (HEAD_DIM // 2)
    cos, sin = cos_ref[...], sin_ref[...]
    rot = []
    for j in range((ATTN_Q_W + ATTN_KV_W) // LANES):
        xs = qk[:, j * LANES:(j + 1) * LANES]
        other = jnp.where(first, pltpu.roll(xs, LANES - HEAD_DIM // 2, 1), pltpu.roll(xs, HEAD_DIM // 2, 1))
        rot.append(xs * cos + other * sin)
    q_scale = HEAD_DIM ** -0.5 * float(np.log2(np.e))
    for j in range(ATTN_Q_W // LANES):
        qt_ref[0, j * LANES:(j + 1) * LANES, :] = (rot[j] * q_scale).T.astype(BF16)
    k_rot = rot[ATTN_Q_W // LANES]
    ones_lane = jnp.where(lane == HEAD_DIM, 1.0, 0.0)
    for g in range(ATTN_KV_HEADS):
        k_head = k_rot if g == 0 else pltpu.roll(k_rot, LANES - g * HEAD_DIM, 1)
        k_ref[0, g] = jnp.where(lane < HEAD_DIM, k_head, ones_lane).astype(BF16)

    for g in range(ATTN_KV_HEADS):
        vt_ref[0, g] = vt[g * HEAD_DIM:(g + 1) * HEAD_DIM].astype(BF16)

    gq_ref[0] = (gqk[:, :GLA_K_W] * (GLA_DK ** -0.5)).astype(BF16)
    gk_ref[0] = gqk[:, GLA_K_W:].astype(BF16)
    gv_ref[0] = gv.astype(BF16)
    gvt_ref[0] = gv.T.astype(BF16)

    la = (jnp.minimum(z, 0.0) - jnp.log(1.0 + jnp.exp(-jnp.abs(z)))) * (1.0 / GLA_GATE_NORM)
    laf_ref[0] = la[:, :GLA_K_W]
    lab_ref[0] = la[:, GLA_K_W:]

    gs_ref[0] = (gs * _sigmoid(gs)).astype(BF16)
    gt_ref[0] = _sigmoid(gt).astype(BF16)


def _in_proj(h, mod2, g, w_r, gqk, seg, cos_t, sin_t, w_dec, b_dec, b_gate, *, tm=512):
    nb, t, d = h.shape
    tm = min(tm, t)
    wp = w_r.shape[1]
    tok = lambda w: pl.BlockSpec((1, tm, w), lambda b, i: (b, i, 0))
    out_shape = [
        jax.ShapeDtypeStruct((nb, ATTN_Q_W, t), BF16),
        jax.ShapeDtypeStruct((nb, ATTN_KV_HEADS, t, LANES), BF16),
        jax.ShapeDtypeStruct((nb, ATTN_KV_HEADS, HEAD_DIM, t), BF16),
        jax.ShapeDtypeStruct((nb, t, GLA_K_W), BF16),
        jax.ShapeDtypeStruct((nb, t, GLA_K_W), BF16),
        jax.ShapeDtypeStruct((nb, t, GLA_V_W), BF16),
        jax.ShapeDtypeStruct((nb, GLA_V_W, t), BF16),
        jax.ShapeDtypeStruct((nb, t, GLA_K_W), F32),
        jax.ShapeDtypeStruct((nb, t, GLA_K_W), F32),
        jax.ShapeDtypeStruct((nb, t, GLA_V_W), BF16),
        jax.ShapeDtypeStruct((nb, t, 2 * d), BF16),
    ]
    out_specs = [
        pl.BlockSpec((1, ATTN_Q_W, tm), lambda b, i: (b, 0, i)),
        pl.BlockSpec((1, ATTN_KV_HEADS, tm, LANES), lambda b, i: (b, 0, i, 0)),
        pl.BlockSpec((1, ATTN_KV_HEADS, HEAD_DIM, tm), lambda b, i: (b, 0, 0, i)),
        tok(GLA_K_W), tok(GLA_K_W), tok(GLA_V_W),
        pl.BlockSpec((1, GLA_V_W, tm), lambda b, i: (b, 0, i)),
        tok(GLA_K_W), tok(GLA_K_W), tok(GLA_V_W), tok(2 * d),
    ]
    in_specs = [
        pl.BlockSpec((1, tm, d), lambda b, i: (b, i, 0)),
        pl.BlockSpec((1, 2, d), lambda b, i: (b, 0, 0)),
        _const_spec((1, d)), _const_spec((d, wp)), _const_spec((1, ATTN_Q_W + ATTN_KV_W)),
        _const_spec((2 * LANES, LANES)),
        pl.BlockSpec((tm, LANES), lambda b, i: (i, 0)),
        pl.BlockSpec((tm, LANES), lambda b, i: (i, 0)),
        _const_spec((LANES, 2 * GLA_K_W)),
        _const_spec((1, 2 * GLA_K_W)), _const_spec((1, 2 * d)),
    ]
    return pl.pallas_call(
        _proj_kernel,
        grid=(nb, t // tm),
        in_specs=in_specs,
        out_specs=out_specs,
        out_shape=out_shape,
        compiler_params=pltpu.CompilerParams(dimension_semantics=("arbitrary", "arbitrary"),
                                             vmem_limit_bytes=VMEM_LIMIT),
        name="in_proj",
    )(h, mod2, g.reshape(1, d), w_r, gqk, seg, cos_t, sin_t, w_dec, b_dec, b_gate)


ATTN_GROUPS = 2
ATTN_ROW_BLOCK = 256
ATTN_KC = 1024
ATTN_UNROLL = 3
ATTN_BLOCKS_PER_ITER = 33
ATTN_MAX_SHIFT = 50.0


def _attn_plan(t, tc):
    if tc % (2 * LANES) == 0:
        kc = max(c for c in range(LANES, ATTN_KC + 1, LANES) if t % c == 0)
        edge = tc // 2
        chunks = [(t, edge)] + [(i * kc, kc) for i in range(t // kc)] + [(t + edge, edge)]
        return chunks, (1, len(chunks) - 1)
    s_len = t + tc
    kc = max(c for c in range(LANES, ATTN_KC + 1, LANES) if s_len % c == 0)
    chunks = [(i * kc, kc) for i in range(s_len // kc)]
    return chunks, (0, len(chunks))


def _sublane_partial_sums(p):
    return p.astype(F32).reshape(p.shape[0] // 8, 8, p.shape[1]).sum(axis=0)


def _attn_kernel(par_ref, qt_ref, k_ref, vt_ref, o_ref, qs_ref, *group_refs, chunks, run):
    tq = qt_ref.shape[2]
    s_len = k_ref.shape[2]
    gw = Q_GROUP * tq // ATTN_GROUPS
    per = len(group_refs) // ATTN_GROUPS
    m_refs, cmax_refs, l_refs, acc_refs, s_refs = (
        [group_refs[g * per + j] for g in range(ATTN_GROUPS)] for j in range(per))
    for h in range(Q_GROUP):
        qs_ref[0:HEAD_DIM, h * tq:(h + 1) * tq] = qt_ref[0, h * HEAD_DIM:(h + 1) * HEAD_DIM, :]
    extra = lax.broadcasted_iota(jnp.int32, (LANES - HEAD_DIM, Q_GROUP * tq), 0)
    qs_ref[HEAD_DIM:LANES, :] = jnp.where(extra == 0, -par_ref[1], 0.0).astype(BF16)
    for g in range(ATTN_GROUPS):
        acc_refs[g][...] = jnp.zeros(acc_refs[g].shape, F32)
        l_refs[g][...] = jnp.zeros(l_refs[g].shape, F32)

    def rows(off, j, rb):
        start = off + j * rb
        return pl.ds(start if isinstance(start, int) else pl.multiple_of(start, LANES), rb)

    def bounded_scores():
        rb = ATTN_ROW_BLOCK if s_len % ATTN_ROW_BLOCK == 0 else LANES
        n_blocks = s_len // rb
        per_iter = max(d for d in range(1, ATTN_BLOCKS_PER_ITER + 1) if n_blocks % d == 0)

        def body(it, carry):
            sums = [None] * ATTN_GROUPS
            dens = [None] * ATTN_GROUPS
            pending = None

            def finish(unit):
                blk, g, p = unit
                part = _dot(vt_ref[0, 0, :, blk], p)
                sums[g] = part if sums[g] is None else sums[g] + part
                den = _sublane_partial_sums(p)
                dens[g] = den if dens[g] is None else dens[g] + den

            for j in range(per_iter):
                blk = rows(it * (per_iter * rb), j, rb)
                for g in range(ATTN_GROUPS):
                    s = _dot(k_ref[0, 0, blk, :], qs_ref[:, g * gw:(g + 1) * gw])
                    if pending is not None:
                        finish(pending)
                    pending = (blk, g, jnp.exp2(s).astype(BF16))
            finish(pending)
            for g in range(ATTN_GROUPS):
                acc_refs[g][...] += sums[g]
                l_refs[g][...] += jnp.sum(dens[g], axis=0, keepdims=True)
            return carry

        lax.fori_loop(0, n_blocks // per_iter, body, 0)

    def online_softmax():
        for g in range(ATTN_GROUPS):
            m_refs[g][...] = jnp.full(m_refs[g].shape, -jnp.inf, F32)

        def stage(score, apply):
            n_s = n_a = 0
            if score is not None:
                off_s, size_s, gs = score
                rb_s = min(ATTN_ROW_BLOCK, size_s)
                n_s = size_s // rb_s
            if apply is not None:
                off_a, size_a, ga = apply
                rb_a = min(ATTN_ROW_BLOCK, size_a)
                n_a = size_a // rb_a
                m_old = m_refs[ga][...]
                m_new = jnp.maximum(m_old, cmax_refs[ga][...])
                m_refs[ga][...] = m_new
            cmax = pv_sum = den = None
            for j in range(max(n_s, n_a)):
                if j < n_a:
                    p = jnp.exp2(s_refs[ga][j * rb_a:(j + 1) * rb_a, :] - m_new).astype(BF16)
                if j < n_s:
                    s = _dot(k_ref[0, 0, rows(off_s, j, rb_s), :], qs_ref[:, gs * gw:(gs + 1) * gw])
                    s_refs[gs][j * rb_s:(j + 1) * rb_s, :] = s
                    bmax = jnp.max(s, axis=0, keepdims=True)
                    cmax = bmax if cmax is None else jnp.maximum(cmax, bmax)
                if j < n_a:
                    part = _dot(vt_ref[0, 0, :, rows(off_a, j, rb_a)], p)
                    pv_sum = part if pv_sum is None else pv_sum + part
                    dj = _sublane_partial_sums(p)
                    den = dj if den is None else den + dj
            if score is not None:
                cmax_refs[gs][...] = cmax
            if apply is not None:
                alpha = jnp.exp2(m_old - m_new)
                acc_refs[ga][...] = alpha * acc_refs[ga][...] + pv_sum
                l_refs[ga][...] = alpha * l_refs[ga][...] + jnp.sum(den, axis=0, keepdims=True)

        last = ATTN_GROUPS - 1

        def chunk_stages(cur, prev):
            stage(cur + (0,), None if prev is None else prev + (last,))
            for g in range(1, ATTN_GROUPS):
                stage(cur + (g,), cur + (g - 1,))

        a, b = run
        trips = b - a - 1
        unroll = ATTN_UNROLL if trips >= ATTN_UNROLL else 1
        first_loop = a + 1 + trips % unroll
        for i in range(first_loop):
            chunk_stages(chunks[i], chunks[i - 1] if i else None)
        if b > first_loop:
            off0, kc = chunks[first_loop]

            def body(i, carry):
                for u in range(unroll):
                    off = off0 + (i * unroll + u) * kc
                    chunk_stages((off, kc), (off - kc, kc))
                return carry

            lax.fori_loop(0, (b - first_loop) // unroll, body, 0)
        for i in range(b, len(chunks)):
            chunk_stages(chunks[i], chunks[i - 1])
        stage(None, chunks[-1] + (last,))

    pl.when(par_ref[0] > 0.0)(bounded_scores)
    pl.when(par_ref[0] <= 0.0)(online_softmax)

    acc = jnp.concatenate([r[...] for r in acc_refs], axis=1)
    ot = acc / jnp.concatenate([r[...] for r in l_refs], axis=1)
    for j in range(Q_GROUP // 2):
        pair = jnp.concatenate([ot[:, (2 * j) * tq:(2 * j + 1) * tq], ot[:, (2 * j + 1) * tq:(2 * j + 2) * tq]],
                               axis=0)
        o_ref[0, :, j * LANES:(j + 1) * LANES] = pair.T.astype(o_ref.dtype)


def _attention(par, qt, k, vt, t_ctx, *, tq=512):
    nb, _, t = qt.shape
    s_len = k.shape[2]
    tq = min(tq, t)
    chunks, run = _attn_plan(t, t_ctx)
    kc = max(size for _, size in chunks)
    gw = Q_GROUP * tq // ATTN_GROUPS
    return pl.pallas_call(
        functools.partial(_attn_kernel, chunks=tuple(chunks), run=run),
        grid=(nb, ATTN_KV_HEADS, t // tq),
        in_specs=[
            pl.BlockSpec(memory_space=pltpu.SMEM),
            pl.BlockSpec((1, Q_GROUP * HEAD_DIM, tq), lambda b, g, i: (b, g, i)),
            pl.BlockSpec((1, 1, s_len, LANES), lambda b, g, i: (b, g, 0, 0)),
            pl.BlockSpec((1, 1, HEAD_DIM, s_len), lambda b, g, i: (b, g, 0, 0)),
        ],
        out_specs=pl.BlockSpec((1, tq, Q_GROUP * HEAD_DIM), lambda b, g, i: (b, i, g)),
        out_shape=jax.ShapeDtypeStruct((nb, t, ATTN_Q_W), BF16),
        scratch_shapes=[pltpu.VMEM((LANES, Q_GROUP * tq), BF16)] + ATTN_GROUPS * [
            pltpu.VMEM((1, gw), F32),
            pltpu.VMEM((1, gw), F32),
            pltpu.VMEM((1, gw), F32),
            pltpu.VMEM((HEAD_DIM, gw), F32),
            pltpu.VMEM((kc, gw), F32)],
        compiler_params=pltpu.CompilerParams(dimension_semantics=("arbitrary", "arbitrary", "arbitrary"),
                                             vmem_limit_bytes=VMEM_LIMIT),
        name="flash_attention",
    )(par, qt, k, vt)


def _gla_constants():
    c = GLA_CHUNK
    i = np.arange(c)[:, None]
    t = np.arange(c)[None, :]
    fwd = [t > i, t <= i]
    bwd = [t < i, t >= i]
    s = c // 2
    while s >= 1:
        mid = (i // (2 * s)) * (2 * s) + s
        second = (i % (2 * s)) >= s
        fwd.append(np.where(second, (t >= mid) & (t <= i), (t > i) & (t < mid)))
        bwd.append(np.where(second, (t >= mid) & (t < i), (t >= i) & (t < mid)))
        s //= 2
    to = lambda blocks: jnp.asarray(np.concatenate(blocks, axis=0).astype(np.float32), dtype=BF16)
    return to(fwd), to(bwd)


_GLA_LEVELS = int(np.log2(GLA_CHUNK))


def _cum(mat, la):
    hi, lo = _split2(la)
    r = _dot(mat, jnp.concatenate([hi, lo], axis=-1))
    return r[:, :LANES] + r[:, LANES:]


def _gla_kernel(q_ref, k_ref, v_ref, vt_ref, laf_ref, lab_ref, kc_ref, vtc_ref, lafc_ref, labc_ref,
                mf_ref, mb_ref, o_ref, st_ref, dec_ref, *, cpt):
    c = GLA_CHUNK
    n_lat = k_ref.shape[1] // c
    n_ctx = kc_ref.shape[1] // c
    n_all = n_ctx + n_lat
    tile = pl.program_id(2)
    lane = lax.broadcasted_iota(jnp.int32, (c, LANES), 1)
    head_masks = [lane < GLA_DK, lane >= GLA_DK]
    lane2 = lax.broadcasted_iota(jnp.int32, (c, 2 * LANES), 1) % LANES
    pair_masks = [lane2 < GLA_DK, lane2 >= GLA_DK]

    def increments(k_r, vt_r, laf_r, lab_r, n0, slot0, count):
        offs = [pl.multiple_of((n0 + i) * c, c) for i in range(count)]
        rfs = [_cum(mf_ref[0:2 * c, :], laf_r[0, pl.ds(off, c), :]) for off in offs]
        rbs = [_cum(mb_ref[0:2 * c, :], lab_r[0, pl.ds(off, c), :]) for off in offs]
        for i, (off, rf, rb) in enumerate(zip(offs, rfs, rbs)):
            k = k_r[0, pl.ds(off, c), :].astype(F32)
            kfb = jnp.concatenate([k * jnp.exp(rf[0:c]),
                                   k * jnp.exp(rb[0:c])], axis=-1)
            kk = jnp.concatenate([jnp.where(pair_masks[0], kfb, 0.0), jnp.where(pair_masks[1], kfb, 0.0)],
                                 axis=0).astype(BF16)
            vt2 = jnp.concatenate([vt_r[0, 0:GLA_DV, pl.ds(off, c)], vt_r[0, GLA_DV:2 * GLA_DV, pl.ds(off, c)]],
                                  axis=-1)
            st_ref[slot0 + i] = _dot(vt2, kk)
            dec_ref[slot0 + i, 0:1, :] = jnp.exp(rf[2 * c - 1:2 * c, :])
            dec_ref[slot0 + i, 1:2, :] = jnp.exp(rb[c:c + 1, :])

    def group_size(n):
        return max(g for g in (4, 2, 1) if n % g == 0)

    @pl.when(tile == 0)
    def _():
        gc, gl = group_size(n_ctx), group_size(n_lat)

        def ctx_body(i, carry):
            increments(kc_ref, vtc_ref, lafc_ref, labc_ref, i * gc, i * gc, gc)
            return carry
        lax.fori_loop(0, n_ctx // gc, ctx_body, 0)

        def lat_body(i, carry):
            increments(k_ref, vt_ref, laf_ref, lab_ref, i * gl, n_ctx + i * gl, gl)
            return carry
        lax.fori_loop(0, n_lat // gl, lat_body, 0)

        def fwd_body(s, st):
            inc = st_ref[s, :, 0:LANES]
            st_ref[s, :, 0:LANES] = st
            return st * dec_ref[s, 0:1, :] + inc
        lax.fori_loop(0, n_all, fwd_body, jnp.zeros((GLA_DV, LANES), F32))

        def bwd_body(j, st, base, count):
            s = base + count - 1 - j
            inc = st_ref[s, :, LANES:2 * LANES]
            st_ref[s, :, LANES:2 * LANES] = st
            return st * dec_ref[s, 1:2, :] + inc
        st = lax.fori_loop(0, n_ctx, functools.partial(bwd_body, base=0, count=n_ctx),
                           jnp.zeros((GLA_DV, LANES), F32))
        lax.fori_loop(0, n_lat, functools.partial(bwd_body, base=n_ctx, count=n_lat), st)

    xor2 = lax.broadcasted_iota(jnp.int32, (c, 2 * c), 0) ^ (lax.broadcasted_iota(jnp.int32, (c, 2 * c), 1) & (c - 1))
    row_l = lax.broadcasted_iota(jnp.int32, (c, LANES), 0)
    lane2v = lax.broadcasted_iota(jnp.int32, (c, 2 * GLA_DV), 1)
    lane2s = lax.broadcasted_iota(jnp.int32, (GLA_DV, 2 * LANES), 1) % LANES

    def stack_heads(x):
        return jnp.concatenate([jnp.where(head_masks[0], x, 0.0), jnp.where(head_masks[1], x, 0.0)], axis=0)

    go = group_size(cpt)

    def out_body(i, carry):
        idx = range(go)
        ns = [tile * cpt + i * go + u for u in idx]
        offs = [pl.multiple_of(n * c, c) for n in ns]
        locs = [pl.multiple_of((i * go + u) * c, c) for u in idx]
        rfs = [_cum(mf_ref[c:, :], laf_ref[0, pl.ds(off, c), :]) for off in offs]
        rbs = [_cum(mb_ref[c:, :], lab_ref[0, pl.ds(off, c), :]) for off in offs]
        qs = [q_ref[0, pl.ds(loc, c), :].astype(F32) for loc in locs]
        ks = [k_ref[0, pl.ds(off, c), :].astype(F32) for off in offs]
        a = [2.0 * _dot_nt(qs[u].astype(BF16), stack_heads(ks[u]).astype(BF16)) for u in idx]
        for lvl in range(_GLA_LEVELS):
            sh = _GLA_LEVELS - 1 - lvl
            second = ((row_l >> sh) & 1) == 1
            for u in idx:
                ef = jnp.exp(rfs[u][(1 + lvl) * c:(2 + lvl) * c])
                eb = jnp.exp(rbs[u][(1 + lvl) * c:(2 + lvl) * c])
                ql = (qs[u] * jnp.where(second, ef, eb)).astype(BF16)
                kl = stack_heads(ks[u] * jnp.where(second, eb, ef)).astype(BF16)
                a[u] = jnp.where((xor2 >> sh) == 1, _dot_nt(ql, kl), a[u])
        for u in idx:
            v = v_ref[0, pl.ds(locs[u], c), :]
            v_bd = jnp.concatenate([jnp.where(lane2v < GLA_DV, v, jnp.zeros_like(v)),
                                    jnp.where(lane2v >= GLA_DV, v, jnp.zeros_like(v))], axis=0)
            q_inter = jnp.concatenate([qs[u] * jnp.exp(rfs[u][0:c]), qs[u] * jnp.exp(rbs[u][0:c])],
                                      axis=-1).astype(BF16)
            states = st_ref[n_ctx + ns[u]]
            st2 = jnp.concatenate([jnp.where(lane2s < GLA_DK, states, 0.0),
                                   jnp.where(lane2s >= GLA_DK, states, 0.0)], axis=0).astype(BF16)
            o_ref[0, pl.ds(locs[u], c), :] = _dot(a[u].astype(BF16), v_bd) + _dot_nt(q_inter, st2)
        return carry

    lax.fori_loop(0, cpt // go, out_body, 0)


def _gla(gq, gk, gv, gvt, laf, lab, gk_c, gvt_c, laf_c, lab_c, mf, mb, *, tile=1024):
    nb, t, _ = gq.shape
    tc = gk_c.shape[1] // nb
    tile = min(tile, t)
    c = GLA_CHUNK
    n_all = (t + tc) // c
    pair_k = 2 * GLA_DK
    pair_v = 2 * GLA_DV
    return pl.pallas_call(
        functools.partial(_gla_kernel, cpt=tile // c),
        grid=(nb, GLA_HEADS // 2, t // tile),
        in_specs=[
            pl.BlockSpec((1, tile, pair_k), lambda b, p, i: (b, i, p)),
            pl.BlockSpec((1, t, pair_k), lambda b, p, i: (b, 0, p)),
            pl.BlockSpec((1, tile, pair_v), lambda b, p, i: (b, i, p)),
            pl.BlockSpec((1, pair_v, t), lambda b, p, i: (b, p, 0)),
            pl.BlockSpec((1, t, pair_k), lambda b, p, i: (b, 0, p)),
            pl.BlockSpec((1, t, pair_k), lambda b, p, i: (b, 0, p)),
            pl.BlockSpec((1, tc, pair_k), lambda b, p, i: (0, b, p)),
            pl.BlockSpec((1, pair_v, tc), lambda b, p, i: (0, p, b)),
            pl.BlockSpec((1, tc, pair_k), lambda b, p, i: (0, b, p)),
            pl.BlockSpec((1, tc, pair_k), lambda b, p, i: (0, b, p)),
            _const_spec(tuple(mf.shape)), _const_spec(tuple(mb.shape)),
        ],
        out_specs=pl.BlockSpec((1, tile, pair_v), lambda b, p, i: (b, i, p)),
        out_shape=jax.ShapeDtypeStruct((nb, t, GLA_V_W), F32),
        scratch_shapes=[pltpu.VMEM((n_all, GLA_DV, 2 * LANES), F32),
                        pltpu.VMEM((n_all, 8, LANES), F32)],
        compiler_params=pltpu.CompilerParams(dimension_semantics=("arbitrary", "arbitrary", "arbitrary"),
                                             vmem_limit_bytes=VMEM_LIMIT),
        name="gla",
    )(gq, gk, gv, gvt, laf, lab, gk_c, gvt_c, laf_c, lab_c, mf, mb)


def _rope_tables(t):
    rows = t // GRID_W
    row = jnp.repeat(jnp.arange(rows, dtype=F32), GRID_W)
    col = jnp.tile(jnp.arange(GRID_W, dtype=F32), rows)
    freqs = ROPE_THETA ** (-jnp.arange(0, ROPE_AXIS_DIM, 2, dtype=F32) / ROPE_AXIS_DIM)
    ang = jnp.concatenate([row[:, None] * freqs, col[:, None] * freqs], axis=-1)
    cos, sin = jnp.cos(ang), jnp.sin(ang)
    reps = LANES // HEAD_DIM
    return (jnp.tile(jnp.concatenate([cos, cos], axis=-1), (1, reps)),
            jnp.tile(jnp.concatenate([-sin, sin], axis=-1), (1, reps)))


def kernel(x, c, ctx, c_ctx, w_mod, b_mod, g_norm, w_ffn_up, w_ffn_down, w_in, g_q, g_k,
           w_decay, b_decay, g_gla, w_branch, b_gate, w_out, g_final):
    nb, t, d = x.shape
    tc = ctx.shape[1]
    f = w_ffn_down.shape[2]
    assert w_mod.shape[0] == 1, "single layer"
    assert t % GLA_CHUNK == 0 and tc % GLA_CHUNK == 0 and t % GRID_W == 0

    rows = -(-(nb + 1) // 8) * 8
    c_rows = jnp.zeros((rows, d), F32).at[:nb].set(c).at[nb].set(c_ctx)
    m = _modulation(c_rows, w_mod[0], b_mod[0]).reshape(rows, N_MOD, d)
    m_lat, m_ctx = m[:nb], m[nb:nb + 1]

    wa = [w_ffn_up[0, i, :, :f].astype(BF16) for i in range(2)]
    wb = [w_ffn_up[0, i, :, f:].astype(BF16) for i in range(2)]
    wd = [w_ffn_down[0, i].astype(BF16) for i in range(2)]
    c_low = _C_GT
    w_low = w_in[0][:, c_low:c_low + 2 * GLA_RANK]
    w_r = jnp.concatenate([w_in[0][:, :c_low], w_in[0][:, c_low + 2 * GLA_RANK:], w_low,
                           jnp.zeros((d, LANES - 2 * GLA_RANK), F32)], axis=-1).astype(BF16)
    w_bd = jnp.zeros((2 * GLA_RANK, 2 * GLA_K_W), F32)
    w_bd = w_bd.at[:GLA_RANK, :GLA_K_W].set(w_decay[0, 0]).at[GLA_RANK:, GLA_K_W:].set(w_decay[0, 1])
    w_bd_hi = w_bd.astype(BF16)
    w_bd_lo = (w_bd - w_bd_hi.astype(F32)).astype(BF16)
    w_dec = jnp.concatenate([w_bd_hi, w_bd_hi, w_bd_lo, jnp.zeros_like(w_bd_hi)], axis=0)
    b_dec = b_decay[0].reshape(1, 2 * GLA_K_W)
    gqk = jnp.concatenate([jnp.tile(g_q[0], ATTN_HEADS), jnp.tile(g_k[0], ATTN_KV_HEADS)]).reshape(1, -1)
    lane = np.arange(LANES)
    seg = jnp.asarray(np.tile((lane[:, None] // HEAD_DIM == lane[None, :] // HEAD_DIM) / HEAD_DIM, (2, 1)), dtype=BF16)
    cos_t, sin_t = _rope_tables(t)
    ones_t, zeros_t = jnp.ones((nb * tc, LANES), F32), jnp.zeros((nb * tc, LANES), F32)
    mf, mb = _gla_constants()

    proj = functools.partial(_in_proj, g=g_norm[0, 1], w_r=w_r, gqk=gqk, seg=seg, w_dec=w_dec,
                             b_dec=b_dec, b_gate=b_gate[0].reshape(1, 2 * d))

    hc = _half_ffn(ctx.reshape(1, nb * tc, d), m_ctx[:, 0:3], g_norm[0, 0], wa[0], wb[0], wd[0])
    pc = proj(hc, m_ctx[:, 3:5], cos_t=ones_t, sin_t=zeros_t)
    _, k_c, vt_c, _, gk_c, _, gvt_c, laf_c, lab_c, _, _ = pc

    h1 = _half_ffn(x, m_lat[:, 0:3], g_norm[0, 0], wa[0], wb[0], wd[0])
    qt, k, vt, gq, gk, gv, gvt, laf, lab, gs, gt = proj(h1, m_lat[:, 3:5], cos_t=cos_t, sin_t=sin_t)
    k_all = jnp.concatenate(
        [k, k_c.reshape(ATTN_KV_HEADS, nb, tc, LANES).transpose(1, 0, 2, 3)], axis=2)
    vt_all = jnp.concatenate(
        [vt, vt_c.reshape(ATTN_KV_HEADS, HEAD_DIM, nb, tc).transpose(2, 0, 1, 3)], axis=3)
    bound = jnp.ceil(1.01 * HEAD_DIM ** 0.5 * float(np.log2(np.e)) * jnp.max(jnp.abs(g_q[0])) * jnp.max(jnp.abs(g_k[0])))
    bounded = bound <= ATTN_MAX_SHIFT
    par = jnp.stack([bounded.astype(F32), jnp.where(bounded, bound, 0.0)])
    attn_o = _attention(par, qt, k_all, vt_all, tc)
    gla_o = _gla(gq, gk, gv, gvt, laf, lab, gk_c, gvt_c, laf_c, lab_c, mf, mb)
    return _merge_ffn(h1, attn_o, gla_o, gs, gt, m_lat[:, 5:6], g_gla[0],
                      w_branch[0, 0].astype(BF16), w_branch[0, 1].astype(BF16), w_out[0].astype(BF16),
                      m_lat[:, 6:9], g_norm[0, 2], wa[1], wb[1], wd[1], g_final)
```

```python
import functools

import numpy as np
import jax
import jax.numpy as jnp
from jax import lax
from jax.experimental import pallas as pl
from jax.experimental.pallas import tpu as pltpu

F32 = jnp.float32
BF16 = jnp.bfloat16

EPS = 1e-6
GRID_W = 64
N_MOD = 9
ATTN_HEADS = 8
ATTN_KV_HEADS = 2
HEAD_DIM = 64
ROPE_AXIS_DIM = HEAD_DIM // 2
ROPE_THETA = 10000.0
GLA_HEADS = 4
GLA_DK = 64
GLA_DV = 128
GLA_RANK = 16
GLA_GATE_NORM = 16.0
ATTN_Q_W = ATTN_HEADS * HEAD_DIM
ATTN_KV_W = ATTN_KV_HEADS * HEAD_DIM
GLA_K_W = GLA_HEADS * GLA_DK
GLA_V_W = GLA_HEADS * GLA_DV
Q_GROUP = ATTN_HEADS // ATTN_KV_HEADS

LANES = 128
V7X_VMEM_BYTES = 64 * 1024 * 1024
VMEM_LIMIT = V7X_VMEM_BYTES * 7 // 8
TOKEN_TILE = 512
MOD_COL_TILE = 1024
ATTN_Q_TILE = 512
GLA_TILE = 1024
GLA_CHUNK = 128


def _dot(a, b):
    return jnp.dot(a, b, preferred_element_type=F32)


def _dot_nt(a, b):
    return lax.dot_general(a, b, (((1,), (1,)), ((), ())), preferred_element_type=F32)


def _sigmoid(x):
    return 1.0 / (1.0 + jnp.exp(-x))


def _split2(x):
    hi = x.astype(BF16)
    lo = (x - hi.astype(F32)).astype(BF16)
    return hi, lo


def _rms(x):
    return x * lax.rsqrt(jnp.mean(x * x, axis=-1, keepdims=True) + EPS)


def _const_spec(shape):
    nd = len(shape)
    return pl.BlockSpec(shape, lambda *_: (0,) * nd, pipeline_mode=pl.Buffered(1))


def _mod_kernel(c_ref, w_ref, b_ref, o_ref):
    c = c_ref[...]
    s_hi, s_lo = _split2(c * _sigmoid(c))
    w_hi, w_lo = _split2(w_ref[...])
    o_ref[...] = _dot(s_hi, w_hi) + _dot(s_hi, w_lo) + _dot(s_lo, w_hi) + b_ref[...]


def _modulation(c_rows, w_mod, b_mod):
    rows, d = c_rows.shape
    n = w_mod.shape[1]
    tn = MOD_COL_TILE
    return pl.pallas_call(
        _mod_kernel,
        grid=(n // tn,),
        in_specs=[pl.BlockSpec((rows, d), lambda j: (0, 0)),
                  pl.BlockSpec((d, tn), lambda j: (0, j)),
                  pl.BlockSpec((1, tn), lambda j: (0, j))],
        out_specs=pl.BlockSpec((rows, tn), lambda j: (0, j)),
        out_shape=jax.ShapeDtypeStruct((rows, n), F32),
        compiler_params=pltpu.CompilerParams(dimension_semantics=("arbitrary",),
                                             vmem_limit_bytes=VMEM_LIMIT),
        name="modulation",
    )(c_rows, w_mod, b_mod.reshape(1, n))


def _ffn_math(x, mod_ref, g_ref, wa_ref, wb_ref, wd_ref):
    shift, scale, gate = mod_ref[0, 0:1, :], mod_ref[0, 1:2, :], mod_ref[0, 2:3, :]
    n = ((_rms(x) * g_ref[...]) * (1.0 + scale) + shift).astype(BF16)
    a = _dot(n, wa_ref[...])
    b = _dot(n, wb_ref[...])
    act = ((a * _sigmoid(a)) * b).astype(BF16)
    return x + 0.5 * gate * _dot(act, wd_ref[...])


def _ffn_kernel(h_ref, mod_ref, g_ref, wa_ref, wb_ref, wd_ref, o_ref):
    o_ref[0] = _ffn_math(h_ref[0], mod_ref, g_ref, wa_ref, wb_ref, wd_ref)


def _merge_ffn_kernel(h_ref, ao_ref, go_ref, gs_ref, gt_ref, mg_ref, gg_ref, wb0_ref, wb1_ref, wo_ref,
                      mod_ref, g_ref, wa_ref, wb_ref, wd_ref, gf_ref, o_ref):
    d = h_ref.shape[2]
    go = go_ref[0]
    normed = [_rms(go[:, h * GLA_DV:(h + 1) * GLA_DV]) * gg_ref[...] for h in range(GLA_HEADS)]
    gn = (jnp.concatenate(normed, axis=-1) * gs_ref[0].astype(F32)).astype(BF16)
    y_attn = _dot(ao_ref[0], wb0_ref[...])
    y_gla = _dot(gn, wb1_ref[...])
    gt = gt_ref[0].astype(F32)
    z = (gt[:, :d] * y_attn + gt[:, d:] * y_gla).astype(BF16)
    h2 = h_ref[0] + mg_ref[0] * _dot(z, wo_ref[...])
    out = _ffn_math(h2, mod_ref, g_ref, wa_ref, wb_ref, wd_ref)
    o_ref[0] = _rms(out) * gf_ref[...]


def _ffn_weight_specs(d, f):
    return [_const_spec((1, d)), _const_spec((d, f)), _const_spec((d, f)), _const_spec((f, d))]


def _half_ffn(h, mod3, g, wa, wb, wd, *, tm=TOKEN_TILE):
    nb, t, d = h.shape
    f = wa.shape[1]
    tm = min(tm, t)
    return pl.pallas_call(
        _ffn_kernel,
        grid=(nb, t // tm),
        in_specs=[pl.BlockSpec((1, tm, d), lambda b, i: (b, i, 0)),
                  pl.BlockSpec((1, 3, d), lambda b, i: (b, 0, 0))] + _ffn_weight_specs(d, f),
        out_specs=pl.BlockSpec((1, tm, d), lambda b, i: (b, i, 0)),
        out_shape=jax.ShapeDtypeStruct((nb, t, d), F32),
        compiler_params=pltpu.CompilerParams(dimension_semantics=("arbitrary", "arbitrary"),
                                             vmem_limit_bytes=VMEM_LIMIT),
        name="half_ffn",
    )(h, mod3, g.reshape(1, d), wa, wb, wd)


def _merge_ffn(h, attn_o, gla_o, gs, gt, m_gate, g_gla, wb0, wb1, wo, mod3, g, wa, wb, wd, g_final,
               *, tm=TOKEN_TILE):
    nb, t, d = h.shape
    f = wa.shape[1]
    tm = min(tm, t)
    tok = lambda w: pl.BlockSpec((1, tm, w), lambda b, i: (b, i, 0))
    return pl.pallas_call(
        _merge_ffn_kernel,
        grid=(nb, t // tm),
        in_specs=[tok(d), tok(ATTN_Q_W), tok(GLA_V_W), tok(GLA_V_W), tok(2 * d),
                  pl.BlockSpec((1, 1, d), lambda b, i: (b, 0, 0)),
                  _const_spec((1, GLA_DV)), _const_spec((ATTN_Q_W, d)), _const_spec((GLA_V_W, d)),
                  _const_spec((d, d)),
                  pl.BlockSpec((1, 3, d), lambda b, i: (b, 0, 0))] + _ffn_weight_specs(d, f)
                 + [_const_spec((1, d))],
        out_specs=tok(d),
        out_shape=jax.ShapeDtypeStruct((nb, t, d), F32),
        compiler_params=pltpu.CompilerParams(dimension_semantics=("arbitrary", "arbitrary"),
                                             vmem_limit_bytes=VMEM_LIMIT),
        name="merge_ffn_final",
    )(h, attn_o, gla_o, gs, gt, m_gate, g_gla.reshape(1, GLA_DV), wb0, wb1, wo,
      mod3, g.reshape(1, d), wa, wb, wd, g_final.reshape(1, d))


_C_Q = 0
_C_K = _C_Q + ATTN_Q_W
_C_V = _C_K + ATTN_KV_W
_C_GQ = _C_V + ATTN_KV_W
_C_GK = _C_GQ + GLA_K_W
_C_GV = _C_GK + GLA_K_W
_C_GS = _C_GV + GLA_V_W
_C_GT = _C_GS + GLA_V_W


def _proj_kernel(h_ref, mod_ref, g_ref, w_ref, gqk_ref, seg_ref, cos_ref, sin_ref, wdec_ref,
                 bdec_ref, bgate_ref,
                 qt_ref, k_ref, vt_ref, gq_ref, gk_ref, gv_ref, gvt_ref, laf_ref, lab_ref, gs_ref, gt_ref):
    d = h_ref.shape[2]
    c_low = _C_GT + 2 * d
    x = h_ref[0]
    shift, scale = mod_ref[0, 0:1, :], mod_ref[0, 1:2, :]
    n = ((_rms(x) * g_ref[...]) * (1.0 + scale) + shift).astype(BF16)
    tm = x.shape[0]

    qk = _dot(n, w_ref[:, _C_Q:_C_V])
    low = _dot(n, w_ref[:, c_low:c_low + LANES])
    gt = _dot(n, w_ref[:, _C_GT:c_low]) + bgate_ref[...]

    seg = seg_ref[...]
    ms = []
    for j in range((ATTN_Q_W + ATTN_KV_W) // LANES):
        sq = qk[:, j * LANES:(j + 1) * LANES]
        hi, lo = _split2(sq * sq)
        ms.append(_dot(jnp.concatenate([hi, lo], axis=-1), seg))
    ms = jnp.concatenate(ms, axis=-1)
    gs = _dot(n, w_ref[:, _C_GS:_C_GT])

    l_hi = low.astype(BF16).astype(F32)
    l_lo = low - l_hi
    packed = l_hi + pltpu.roll(l_lo, 2 * GLA_RANK, 1) + pltpu.roll(l_hi, 4 * GLA_RANK, 1)
    z = _dot(packed.astype(BF16), wdec_ref[...]) + bdec_ref[...]
    gv = _dot(n, w_ref[:, _C_GV:_C_GS])
    vt = _dot(n, w_ref[:, _C_V:_C_GQ]).T
    gqk = _dot(n, w_ref[:, _C_GQ:_C_GV])

    qk = qk * lax.rsqrt(ms + EPS) * gqk_ref[...]
    lane = lax.broadcasted_iota(jnp.int32, (tm, LANES), 1)
    first = (lane % HEAD_DIM) < (HEAD_DIM // 2)
    cos, sin = cos_ref[...], sin_ref[...]
    rot = []
    for j in range((ATTN_Q_W + ATTN_KV_W) // LANES):
        xs = qk[:, j * LANES:(j + 1) * LANES]
        other = jnp.where(first, pltpu.roll(xs, LANES - HEAD_DIM // 2, 1), pltpu.roll(xs, HEAD_DIM // 2, 1))
        rot.append(xs * cos + other * sin)
    q_scale = HEAD_DIM ** -0.5 * float(np.log2(np.e))
    for j in range(ATTN_Q_W // LANES):
        qt_ref[0, j * LANES:(j + 1) * LANES, :] = (rot[j] * q_scale).T.astype(BF16)
    k_rot = rot[ATTN_Q_W // LANES]
    ones_lane = jnp.where(lane == HEAD_DIM, 1.0, 0.0)
    for g in range(ATTN_KV_HEADS):
        k_head = k_rot if g == 0 else pltpu.roll(k_rot, LANES - g * HEAD_DIM, 1)
        k_ref[0, g] = jnp.where(lane < HEAD_DIM, k_head, ones_lane).astype(BF16)

    for g in range(ATTN_KV_HEADS):
        vt_ref[0, g] = vt[g * HEAD_DIM:(g + 1) * HEAD_DIM].astype(BF16)

    gq_ref[0] = (gqk[:, :GLA_K_W] * (GLA_DK ** -0.5)).astype(BF16)
    gk_ref[0] = gqk[:, GLA_K_W:].astype(BF16)
    gv_ref[0] = gv.astype(BF16)
    gvt_ref[0] = gv.T.astype(BF16)

    la = (jnp.minimum(z, 0.0) - jnp.log(1.0 + jnp.exp(-jnp.abs(z)))) * (1.0 / GLA_GATE_NORM)
    laf_ref[0] = la[:, :GLA_K_W]
    lab_ref[0] = la[:, GLA_K_W:]

    gs_ref[0] = (gs * _sigmoid(gs)).astype(BF16)
    gt_ref[0] = _sigmoid(gt).astype(BF16)


def _in_proj(h, mod2, g, w_r, gqk, seg, cos_t, sin_t, w_dec, b_dec, b_gate, *, tm=TOKEN_TILE):
    nb, t, d = h.shape
    tm = min(tm, t)
    wp = w_r.shape[1]
    tok = lambda w: pl.BlockSpec((1, tm, w), lambda b, i: (b, i, 0))
    out_shape = [
        jax.ShapeDtypeStruct((nb, ATTN_Q_W, t), BF16),
        jax.ShapeDtypeStruct((nb, ATTN_KV_HEADS, t, LANES), BF16),
        jax.ShapeDtypeStruct((nb, ATTN_KV_HEADS, HEAD_DIM, t), BF16),
        jax.ShapeDtypeStruct((nb, t, GLA_K_W), BF16),
        jax.ShapeDtypeStruct((nb, t, GLA_K_W), BF16),
        jax.ShapeDtypeStruct((nb, t, GLA_V_W), BF16),
        jax.ShapeDtypeStruct((nb, GLA_V_W, t), BF16),
        jax.ShapeDtypeStruct((nb, t, GLA_K_W), F32),
        jax.ShapeDtypeStruct((nb, t, GLA_K_W), F32),
        jax.ShapeDtypeStruct((nb, t, GLA_V_W), BF16),
        jax.ShapeDtypeStruct((nb, t, 2 * d), BF16),
    ]
    out_specs = [
        pl.BlockSpec((1, ATTN_Q_W, tm), lambda b, i: (b, 0, i)),
        pl.BlockSpec((1, ATTN_KV_HEADS, tm, LANES), lambda b, i: (b, 0, i, 0)),
        pl.BlockSpec((1, ATTN_KV_HEADS, HEAD_DIM, tm), lambda b, i: (b, 0, 0, i)),
        tok(GLA_K_W), tok(GLA_K_W), tok(GLA_V_W),
        pl.BlockSpec((1, GLA_V_W, tm), lambda b, i: (b, 0, i)),
        tok(GLA_K_W), tok(GLA_K_W), tok(GLA_V_W), tok(2 * d),
    ]
    in_specs = [
        pl.BlockSpec((1, tm, d), lambda b, i: (b, i, 0)),
        pl.BlockSpec((1, 2, d), lambda b, i: (b, 0, 0)),
        _const_spec((1, d)), _const_spec((d, wp)), _const_spec((1, ATTN_Q_W + ATTN_KV_W)),
        _const_spec((2 * LANES, LANES)),
        pl.BlockSpec((tm, LANES), lambda b, i: (i, 0)),
        pl.BlockSpec((tm, LANES), lambda b, i: (i, 0)),
        _const_spec((LANES, 2 * GLA_K_W)),
        _const_spec((1, 2 * GLA_K_W)), _const_spec((1, 2 * d)),
    ]
    return pl.pallas_call(
        _proj_kernel,
        grid=(nb, t // tm),
        in_specs=in_specs,
        out_specs=out_specs,
        out_shape=out_shape,
        compiler_params=pltpu.CompilerParams(dimension_semantics=("arbitrary", "arbitrary"),
                                             vmem_limit_bytes=VMEM_LIMIT),
        name="in_proj",
    )(h, mod2, g.reshape(1, d), w_r, gqk, seg, cos_t, sin_t, w_dec, b_dec, b_gate)


ATTN_GROUPS = 2
ATTN_ROW_BLOCK = 256
ATTN_FAST_ROW_BLOCK = 256
ATTN_KC = 1024
ATTN_UNROLL = 3
ATTN_BLOCKS_PER_ITER = 33
ATTN_MAX_SHIFT = 50.0


def _attn_plan(t, tc):
    if tc % (2 * LANES) == 0:
        kc = max(c for c in range(LANES, ATTN_KC + 1, LANES) if t % c == 0)
        edge = tc // 2
        chunks = [(t, edge)] + [(i * kc, kc) for i in range(t // kc)] + [(t + edge, edge)]
        return chunks, (1, len(chunks) - 1)
    s_len = t + tc
    kc = max(c for c in range(LANES, ATTN_KC + 1, LANES) if s_len % c == 0)
    chunks = [(i * kc, kc) for i in range(s_len // kc)]
    return chunks, (0, len(chunks))


def _sublane_partial_sums(p):
    return p.astype(F32).reshape(p.shape[0] // 8, 8, p.shape[1]).sum(axis=0)


def _attn_kernel(par_ref, qt_ref, k_ref, vt_ref, o_ref, qs_ref, *group_refs, chunks, run):
    tq = qt_ref.shape[2]
    s_len = k_ref.shape[2]
    gw = Q_GROUP * tq // ATTN_GROUPS
    per = len(group_refs) // ATTN_GROUPS
    m_refs, cmax_refs, l_refs, acc_refs, s_refs = (
        [group_refs[g * per + j] for g in range(ATTN_GROUPS)] for j in range(per))
    for h in range(Q_GROUP):
        qs_ref[0:HEAD_DIM, h * tq:(h + 1) * tq] = qt_ref[0, h * HEAD_DIM:(h + 1) * HEAD_DIM, :]
    extra = lax.broadcasted_iota(jnp.int32, (LANES - HEAD_DIM, Q_GROUP * tq), 0)
    qs_ref[HEAD_DIM:LANES, :] = jnp.where(extra == 0, -par_ref[1], 0.0).astype(BF16)
    for g in range(ATTN_GROUPS):
        acc_refs[g][...] = jnp.zeros(acc_refs[g].shape, F32)
        l_refs[g][...] = jnp.zeros(l_refs[g].shape, F32)

    def rows(off, j, rb):
        start = off + j * rb
        return pl.ds(start if isinstance(start, int) else pl.multiple_of(start, LANES), rb)

    def bounded_scores():
        rb = ATTN_FAST_ROW_BLOCK if s_len % ATTN_FAST_ROW_BLOCK == 0 else LANES
        n_blocks = s_len // rb
        per_iter = max(d for d in range(1, ATTN_BLOCKS_PER_ITER + 1) if n_blocks % d == 0)

        def body(it, carry):
            sums = [None] * ATTN_GROUPS
            dens = [None] * ATTN_GROUPS
            pending = None

            def finish(unit):
                blk, g, p = unit
                part = _dot(vt_ref[0, 0, :, blk], p)
                sums[g] = part if sums[g] is None else sums[g] + part
                den = _sublane_partial_sums(p)
                dens[g] = den if dens[g] is None else dens[g] + den

            for j in range(per_iter):
                blk = rows(it * (per_iter * rb), j, rb)
                for g in range(ATTN_GROUPS):
                    s = _dot(k_ref[0, 0, blk, :], qs_ref[:, g * gw:(g + 1) * gw])
                    if pending is not None:
                        finish(pending)
                    pending = (blk, g, jnp.exp2(s).astype(BF16))
            finish(pending)
            for g in range(ATTN_GROUPS):
                acc_refs[g][...] += sums[g]
                l_refs[g][...] += jnp.sum(dens[g], axis=0, keepdims=True)
            return carry

        lax.fori_loop(0, n_blocks // per_iter, body, 0)

    def online_softmax():
        for g in range(ATTN_GROUPS):
            m_refs[g][...] = jnp.full(m_refs[g].shape, -jnp.inf, F32)

        def stage(score, apply):
            n_s = n_a = 0
            if score is not None:
                off_s, size_s, gs = score
                rb_s = min(ATTN_ROW_BLOCK, size_s)
                n_s = size_s // rb_s
            if apply is not None:
                off_a, size_a, ga = apply
                rb_a = min(ATTN_ROW_BLOCK, size_a)
                n_a = size_a // rb_a
                m_old = m_refs[ga][...]
                m_new = jnp.maximum(m_old, cmax_refs[ga][...])
                m_refs[ga][...] = m_new
            cmax = pv_sum = den = None
            for j in range(max(n_s, n_a)):
                if j < n_a:
                    p = jnp.exp2(s_refs[ga][j * rb_a:(j + 1) * rb_a, :] - m_new).astype(BF16)
                if j < n_s:
                    s = _dot(k_ref[0, 0, rows(off_s, j, rb_s), :], qs_ref[:, gs * gw:(gs + 1) * gw])
                    s_refs[gs][j * rb_s:(j + 1) * rb_s, :] = s
                    bmax = jnp.max(s, axis=0, keepdims=True)
                    cmax = bmax if cmax is None else jnp.maximum(cmax, bmax)
                if j < n_a:
                    part = _dot(vt_ref[0, 0, :, rows(off_a, j, rb_a)], p)
                    pv_sum = part if pv_sum is None else pv_sum + part
                    dj = _sublane_partial_sums(p)
                    den = dj if den is None else den + dj
            if score is not None:
                cmax_refs[gs][...] = cmax
            if apply is not None:
                alpha = jnp.exp2(m_old - m_new)
                acc_refs[ga][...] = alpha * acc_refs[ga][...] + pv_sum
                l_refs[ga][...] = alpha * l_refs[ga][...] + jnp.sum(den, axis=0, keepdims=True)

        last = ATTN_GROUPS - 1

        def chunk_stages(cur, prev):
            stage(cur + (0,), None if prev is None else prev + (last,))
            for g in range(1, ATTN_GROUPS):
                stage(cur + (g,), cur + (g - 1,))

        a, b = run
        trips = b - a - 1
        unroll = ATTN_UNROLL if trips >= ATTN_UNROLL else 1
        first_loop = a + 1 + trips % unroll
        for i in range(first_loop):
            chunk_stages(chunks[i], chunks[i - 1] if i else None)
        if b > first_loop:
            off0, kc = chunks[first_loop]

            def body(i, carry):
                for u in range(unroll):
                    off = off0 + (i * unroll + u) * kc
                    chunk_stages((off, kc), (off - kc, kc))
                return carry

            lax.fori_loop(0, (b - first_loop) // unroll, body, 0)
        for i in range(b, len(chunks)):
            chunk_stages(chunks[i], chunks[i - 1])
        stage(None, chunks[-1] + (last,))

    pl.when(par_ref[0] > 0.0)(bounded_scores)
    pl.when(par_ref[0] <= 0.0)(online_softmax)

    acc = jnp.concatenate([r[...] for r in acc_refs], axis=1)
    ot = acc / jnp.concatenate([r[...] for r in l_refs], axis=1)
    for j in range(Q_GROUP // 2):
        pair = jnp.concatenate([ot[:, (2 * j) * tq:(2 * j + 1) * tq], ot[:, (2 * j + 1) * tq:(2 * j + 2) * tq]],
                               axis=0)
        o_ref[0, :, j * LANES:(j + 1) * LANES] = pair.T.astype(o_ref.dtype)


def _attention(par, qt, k, vt, t_ctx, *, tq=ATTN_Q_TILE):
    nb, _, t = qt.shape
    s_len = k.shape[2]
    tq = min(tq, t)
    chunks, run = _attn_plan(t, t_ctx)
    kc = max(size for _, size in chunks)
    gw = Q_GROUP * tq // ATTN_GROUPS
    return pl.pallas_call(
        functools.partial(_attn_kernel, chunks=tuple(chunks), run=run),
        grid=(nb, ATTN_KV_HEADS, t // tq),
        in_specs=[
            pl.BlockSpec(memory_space=pltpu.SMEM),
            pl.BlockSpec((1, Q_GROUP * HEAD_DIM, tq), lambda b, g, i: (b, g, i)),
            pl.BlockSpec((1, 1, s_len, LANES), lambda b, g, i: (b, g, 0, 0)),
            pl.BlockSpec((1, 1, HEAD_DIM, s_len), lambda b, g, i: (b, g, 0, 0)),
        ],
        out_specs=pl.BlockSpec((1, tq, Q_GROUP * HEAD_DIM), lambda b, g, i: (b, i, g)),
        out_shape=jax.ShapeDtypeStruct((nb, t, ATTN_Q_W), BF16),
        scratch_shapes=[pltpu.VMEM((LANES, Q_GROUP * tq), BF16)] + ATTN_GROUPS * [
            pltpu.VMEM((1, gw), F32),
            pltpu.VMEM((1, gw), F32),
            pltpu.VMEM((1, gw), F32),
            pltpu.VMEM((HEAD_DIM, gw), F32),
            pltpu.VMEM((kc, gw), F32)],
        compiler_params=pltpu.CompilerParams(dimension_semantics=("arbitrary", "arbitrary", "arbitrary"),
                                             vmem_limit_bytes=VMEM_LIMIT),
        name="flash_attention",
    )(par, qt, k, vt)


def _gla_constants():
    c = GLA_CHUNK
    i = np.arange(c)[:, None]
    t = np.arange(c)[None, :]
    fwd = [t > i, t <= i]
    bwd = [t < i, t >= i]
    s = c // 2
    while s >= 1:
        mid = (i // (2 * s)) * (2 * s) + s
        second = (i % (2 * s)) >= s
        fwd.append(np.where(second, (t >= mid) & (t <= i), (t > i) & (t < mid)))
        bwd.append(np.where(second, (t >= mid) & (t < i), (t >= i) & (t < mid)))
        s //= 2
    to = lambda blocks: jnp.asarray(np.concatenate(blocks, axis=0).astype(np.float32), dtype=BF16)
    return to(fwd), to(bwd)


_GLA_LEVELS = int(np.log2(GLA_CHUNK))


def _cum(mat, la):
    hi, lo = _split2(la)
    r = _dot(mat, jnp.concatenate([hi, lo], axis=-1))
    return r[:, :LANES] + r[:, LANES:]


def _gla_kernel(q_ref, k_ref, v_ref, vt_ref, laf_ref, lab_ref, kc_ref, vtc_ref, lafc_ref, labc_ref,
                mf_ref, mb_ref, o_ref, st_ref, dec_ref, *, cpt):
    c = GLA_CHUNK
    n_lat = k_ref.shape[1] // c
    n_ctx = kc_ref.shape[1] // c
    n_all = n_ctx + n_lat
    tile = pl.program_id(2)
    lane = lax.broadcasted_iota(jnp.int32, (c, LANES), 1)
    head_masks = [lane < GLA_DK, lane >= GLA_DK]
    lane2 = lax.broadcasted_iota(jnp.int32, (c, 2 * LANES), 1) % LANES
    pair_masks = [lane2 < GLA_DK, lane2 >= GLA_DK]

    def increments(k_r, vt_r, laf_r, lab_r, n0, slot0, count):
        offs = [pl.multiple_of((n0 + i) * c, c) for i in range(count)]
        rfs = [_cum(mf_ref[0:2 * c, :], laf_r[0, pl.ds(off, c), :]) for off in offs]
        rbs = [_cum(mb_ref[0:2 * c, :], lab_r[0, pl.ds(off, c), :]) for off in offs]
        for i, (off, rf, rb) in enumerate(zip(offs, rfs, rbs)):
            k = k_r[0, pl.ds(off, c), :].astype(F32)
            kfb = jnp.concatenate([k * jnp.exp(rf[0:c]),
                                   k * jnp.exp(rb[0:c])], axis=-1)
            kk = jnp.concatenate([jnp.where(pair_masks[0], kfb, 0.0), jnp.where(pair_masks[1], kfb, 0.0)],
                                 axis=0).astype(BF16)
            vt2 = jnp.concatenate([vt_r[0, 0:GLA_DV, pl.ds(off, c)], vt_r[0, GLA_DV:2 * GLA_DV, pl.ds(off, c)]],
                                  axis=-1)
            st_ref[slot0 + i] = _dot(vt2, kk)
            dec_ref[slot0 + i, 0:1, :] = jnp.exp(rf[2 * c - 1:2 * c, :])
            dec_ref[slot0 + i, 1:2, :] = jnp.exp(rb[c:c + 1, :])

    def group_size(n):
        return max(g for g in (4, 2, 1) if n % g == 0)

    @pl.when(tile == 0)
    def _():
        gc, gl = group_size(n_ctx), group_size(n_lat)

        def ctx_body(i, carry):
            increments(kc_ref, vtc_ref, lafc_ref, labc_ref, i * gc, i * gc, gc)
            return carry
        lax.fori_loop(0, n_ctx // gc, ctx_body, 0)

        def lat_body(i, carry):
            increments(k_ref, vt_ref, laf_ref, lab_ref, i * gl, n_ctx + i * gl, gl)
            return carry
        lax.fori_loop(0, n_lat // gl, lat_body, 0)

        def fwd_body(s, st):
            inc = st_ref[s, :, 0:LANES]
            st_ref[s, :, 0:LANES] = st
            return st * dec_ref[s, 0:1, :] + inc
        lax.fori_loop(0, n_all, fwd_body, jnp.zeros((GLA_DV, LANES), F32))

        def bwd_body(j, st, base, count):
            s = base + count - 1 - j
            inc = st_ref[s, :, LANES:2 * LANES]
            st_ref[s, :, LANES:2 * LANES] = st
            return st * dec_ref[s, 1:2, :] + inc
        st = lax.fori_loop(0, n_ctx, functools.partial(bwd_body, base=0, count=n_ctx),
                           jnp.zeros((GLA_DV, LANES), F32))
        lax.fori_loop(0, n_lat, functools.partial(bwd_body, base=n_ctx, count=n_lat), st)

    xor2 = lax.broadcasted_iota(jnp.int32, (c, 2 * c), 0) ^ (lax.broadcasted_iota(jnp.int32, (c, 2 * c), 1) & (c - 1))
    row_l = lax.broadcasted_iota(jnp.int32, (c, LANES), 0)
    lane2v = lax.broadcasted_iota(jnp.int32, (c, 2 * GLA_DV), 1)
    lane2s = lax.broadcasted_iota(jnp.int32, (GLA_DV, 2 * LANES), 1) % LANES

    def stack_heads(x):
        return jnp.concatenate([jnp.where(head_masks[0], x, 0.0), jnp.where(head_masks[1], x, 0.0)], axis=0)

    go = group_size(cpt)

    def out_body(i, carry):
        idx = range(go)
        ns = [tile * cpt + i * go + u for u in idx]
        offs = [pl.multiple_of(n * c, c) for n in ns]
        locs = [pl.multiple_of((i * go + u) * c, c) for u in idx]
        rfs = [_cum(mf_ref[c:, :], laf_ref[0, pl.ds(off, c), :]) for off in offs]
        rbs = [_cum(mb_ref[c:, :], lab_ref[0, pl.ds(off, c), :]) for off in offs]
        qs = [q_ref[0, pl.ds(loc, c), :].astype(F32) for loc in locs]
        ks = [k_ref[0, pl.ds(off, c), :].astype(F32) for off in offs]
        a = [2.0 * _dot_nt(qs[u].astype(BF16), stack_heads(ks[u]).astype(BF16)) for u in idx]
        for lvl in range(_GLA_LEVELS):
            sh = _GLA_LEVELS - 1 - lvl
            second = ((row_l >> sh) & 1) == 1
            for u in idx:
                ef = jnp.exp(rfs[u][(1 + lvl) * c:(2 + lvl) * c])
                eb = jnp.exp(rbs[u][(1 + lvl) * c:(2 + lvl) * c])
                ql = (qs[u] * jnp.where(second, ef, eb)).astype(BF16)
                kl = stack_heads(ks[u] * jnp.where(second, eb, ef)).astype(BF16)
                a[u] = jnp.where((xor2 >> sh) == 1, _dot_nt(ql, kl), a[u])
        for u in idx:
            v = v_ref[0, pl.ds(locs[u], c), :]
            v_bd = jnp.concatenate([jnp.where(lane2v < GLA_DV, v, jnp.zeros_like(v)),
                                    jnp.where(lane2v >= GLA_DV, v, jnp.zeros_like(v))], axis=0)
            q_inter = jnp.concatenate([qs[u] * jnp.exp(rfs[u][0:c]), qs[u] * jnp.exp(rbs[u][0:c])],
                                      axis=-1).astype(BF16)
            states = st_ref[n_ctx + ns[u]]
            st2 = jnp.concatenate([jnp.where(lane2s < GLA_DK, states, 0.0),
                                   jnp.where(lane2s >= GLA_DK, states, 0.0)], axis=0).astype(BF16)
            o_ref[0, pl.ds(locs[u], c), :] = _dot(a[u].astype(BF16), v_bd) + _dot_nt(q_inter, st2)
        return carry

    lax.fori_loop(0, cpt // go, out_body, 0)


def _gla(gq, gk, gv, gvt, laf, lab, gk_c, gvt_c, laf_c, lab_c, mf, mb, *, tile=GLA_TILE):
    nb, t, _ = gq.shape
    tc = gk_c.shape[1] // nb
    tile = min(tile, t)
    c = GLA_CHUNK
    n_all = (t + tc) // c
    pair_k = 2 * GLA_DK
    pair_v = 2 * GLA_DV
    return pl.pallas_call(
        functools.partial(_gla_kernel, cpt=tile // c),
        grid=(nb, GLA_HEADS // 2, t // tile),
        in_specs=[
            pl.BlockSpec((1, tile, pair_k), lambda b, p, i: (b, i, p)),
            pl.BlockSpec((1, t, pair_k), lambda b, p, i: (b, 0, p)),
            pl.BlockSpec((1, tile, pair_v), lambda b, p, i: (b, i, p)),
            pl.BlockSpec((1, pair_v, t), lambda b, p, i: (b, p, 0)),
            pl.BlockSpec((1, t, pair_k), lambda b, p, i: (b, 0, p)),
            pl.BlockSpec((1, t, pair_k), lambda b, p, i: (b, 0, p)),
            pl.BlockSpec((1, tc, pair_k), lambda b, p, i: (0, b, p)),
            pl.BlockSpec((1, pair_v, tc), lambda b, p, i: (0, p, b)),
            pl.BlockSpec((1, tc, pair_k), lambda b, p, i: (0, b, p)),
            pl.BlockSpec((1, tc, pair_k), lambda b, p, i: (0, b, p)),
            _const_spec(tuple(mf.shape)), _const_spec(tuple(mb.shape)),
        ],
        out_specs=pl.BlockSpec((1, tile, pair_v), lambda b, p, i: (b, i, p)),
        out_shape=jax.ShapeDtypeStruct((nb, t, GLA_V_W), F32),
        scratch_shapes=[pltpu.VMEM((n_all, GLA_DV, 2 * LANES), F32),
                        pltpu.VMEM((n_all, 8, LANES), F32)],
        compiler_params=pltpu.CompilerParams(dimension_semantics=("arbitrary", "arbitrary", "arbitrary"),
                                             vmem_limit_bytes=VMEM_LIMIT),
        name="gla",
    )(gq, gk, gv, gvt, laf, lab, gk_c, gvt_c, laf_c, lab_c, mf, mb)


def _rope_tables(t):
    rows = t // GRID_W
    row = jnp.repeat(jnp.arange(rows, dtype=F32), GRID_W)
    col = jnp.tile(jnp.arange(GRID_W, dtype=F32), rows)
    freqs = ROPE_THETA ** (-jnp.arange(0, ROPE_AXIS_DIM, 2, dtype=F32) / ROPE_AXIS_DIM)
    ang = jnp.concatenate([row[:, None] * freqs, col[:, None] * freqs], axis=-1)
    cos, sin = jnp.cos(ang), jnp.sin(ang)
    reps = LANES // HEAD_DIM
    return (jnp.tile(jnp.concatenate([cos, cos], axis=-1), (1, reps)),
            jnp.tile(jnp.concatenate([-sin, sin], axis=-1), (1, reps)))


def kernel(x, c, ctx, c_ctx, w_mod, b_mod, g_norm, w_ffn_up, w_ffn_down, w_in, g_q, g_k,
           w_decay, b_decay, g_gla, w_branch, b_gate, w_out, g_final):
    nb, t, d = x.shape
    tc = ctx.shape[1]
    f = w_ffn_down.shape[2]
    assert w_mod.shape[0] == 1, "single layer"
    assert t % GLA_CHUNK == 0 and tc % GLA_CHUNK == 0 and t % GRID_W == 0

    rows = -(-(nb + 1) // 8) * 8
    c_rows = jnp.zeros((rows, d), F32).at[:nb].set(c).at[nb].set(c_ctx)
    m = _modulation(c_rows, w_mod[0], b_mod[0]).reshape(rows, N_MOD, d)
    m_lat, m_ctx = m[:nb], m[nb:nb + 1]

    wa = [w_ffn_up[0, i, :, :f].astype(BF16) for i in range(2)]
    wb = [w_ffn_up[0, i, :, f:].astype(BF16) for i in range(2)]
    wd = [w_ffn_down[0, i].astype(BF16) for i in range(2)]
    c_low = _C_GT
    w_low = w_in[0][:, c_low:c_low + 2 * GLA_RANK]
    w_r = jnp.concatenate([w_in[0][:, :c_low], w_in[0][:, c_low + 2 * GLA_RANK:], w_low,
                           jnp.zeros((d, LANES - 2 * GLA_RANK), F32)], axis=-1).astype(BF16)
    w_bd = jnp.zeros((2 * GLA_RANK, 2 * GLA_K_W), F32)
    w_bd = w_bd.at[:GLA_RANK, :GLA_K_W].set(w_decay[0, 0]).at[GLA_RANK:, GLA_K_W:].set(w_decay[0, 1])
    w_bd_hi = w_bd.astype(BF16)
    w_bd_lo = (w_bd - w_bd_hi.astype(F32)).astype(BF16)
    w_dec = jnp.concatenate([w_bd_hi, w_bd_hi, w_bd_lo, jnp.zeros_like(w_bd_hi)], axis=0)
    b_dec = b_decay[0].reshape(1, 2 * GLA_K_W)
    gqk = jnp.concatenate([jnp.tile(g_q[0], ATTN_HEADS), jnp.tile(g_k[0], ATTN_KV_HEADS)]).reshape(1, -1)
    lane = np.arange(LANES)
    seg = jnp.asarray(np.tile((lane[:, None] // HEAD_DIM == lane[None, :] // HEAD_DIM) / HEAD_DIM, (2, 1)), dtype=BF16)
    cos_t, sin_t = _rope_tables(t)
    ones_t, zeros_t = jnp.ones((nb * tc, LANES), F32), jnp.zeros((nb * tc, LANES), F32)
    mf, mb = _gla_constants()

    proj = functools.partial(_in_proj, g=g_norm[0, 1], w_r=w_r, gqk=gqk, seg=seg, w_dec=w_dec,
                             b_dec=b_dec, b_gate=b_gate[0].reshape(1, 2 * d))

    hc = _half_ffn(ctx.reshape(1, nb * tc, d), m_ctx[:, 0:3], g_norm[0, 0], wa[0], wb[0], wd[0])
    pc = proj(hc, m_ctx[:, 3:5], cos_t=ones_t, sin_t=zeros_t)
    _, k_c, vt_c, _, gk_c, _, gvt_c, laf_c, lab_c, _, _ = pc

    h1 = _half_ffn(x, m_lat[:, 0:3], g_norm[0, 0], wa[0], wb[0], wd[0])
    qt, k, vt, gq, gk, gv, gvt, laf, lab, gs, gt = proj(h1, m_lat[:, 3:5], cos_t=cos_t, sin_t=sin_t)
    k_all = jnp.concatenate(
        [k, k_c.reshape(ATTN_KV_HEADS, nb, tc, LANES).transpose(1, 0, 2, 3)], axis=2)
    vt_all = jnp.concatenate(
        [vt, vt_c.reshape(ATTN_KV_HEADS, HEAD_DIM, nb, tc).transpose(2, 0, 1, 3)], axis=3)
    bound = jnp.ceil(1.01 * HEAD_DIM ** 0.5 * float(np.log2(np.e)) * jnp.max(jnp.abs(g_q[0])) * jnp.max(jnp.abs(g_k[0])))
    bounded = bound <= ATTN_MAX_SHIFT
    par = jnp.stack([bounded.astype(F32), jnp.where(bounded, bound, 0.0)])
    attn_o = _attention(par, qt, k_all, vt_all, tc)
    gla_o = _gla(gq, gk, gv, gvt, laf, lab, gk_c, gvt_c, laf_c, lab_c, mf, mb)
    return _merge_ffn(h1, attn_o, gla_o, gs, gt, m_lat[:, 5:6], g_gla[0],
                      w_branch[0, 0].astype(BF16), w_branch[0, 1].astype(BF16), w_out[0].astype(BF16),
                      m_lat[:, 6:9], g_norm[0, 2], wa[1], wb[1], wd[1], g_final)
```

```python
import functools

import numpy as np
import jax
import jax.numpy as jnp
from jax import lax
from jax.experimental import pallas as pl
from jax.experimental.pallas import tpu as pltpu

F32 = jnp.float32
BF16 = jnp.bfloat16

EPS = 1e-6
GRID_W = 64
N_MOD = 9
ATTN_HEADS = 8
ATTN_KV_HEADS = 2
HEAD_DIM = 64
ROPE_AXIS_DIM = HEAD_DIM // 2
ROPE_THETA = 10000.0
GLA_HEADS = 4
GLA_DK = 64
GLA_DV = 128
GLA_RANK = 16
GLA_GATE_NORM = 16.0
ATTN_Q_W = ATTN_HEADS * HEAD_DIM
ATTN_KV_W = ATTN_KV_HEADS * HEAD_DIM
GLA_K_W = GLA_HEADS * GLA_DK
GLA_V_W = GLA_HEADS * GLA_DV
Q_GROUP = ATTN_HEADS // ATTN_KV_HEADS

LANES = 128
V7X_VMEM_BYTES = 64 * 1024 * 1024
VMEM_LIMIT = V7X_VMEM_BYTES * 7 // 8
TOKEN_TILE = 512
MOD_COL_TILE = 1024
ATTN_Q_TILE = 512
GLA_TILE = 1024
GLA_CHUNK = 128


def _dot(a, b):
    return jnp.dot(a, b, preferred_element_type=F32)


def _dot_nt(a, b):
    return lax.dot_general(a, b, (((1,), (1,)), ((), ())), preferred_element_type=F32)


def _sigmoid(x):
    return 1.0 / (1.0 + jnp.exp(-x))


def _split2(x):
    hi = x.astype(BF16)
    lo = (x - hi.astype(F32)).astype(BF16)
    return hi, lo


def _rms(x):
    return x * lax.rsqrt(jnp.mean(x * x, axis=-1, keepdims=True) + EPS)


def _const_spec(shape):
    nd = len(shape)
    return pl.BlockSpec(shape, lambda *_: (0,) * nd, pipeline_mode=pl.Buffered(1))


def _mod_kernel(c_ref, w_ref, b_ref, o_ref):
    c = c_ref[...]
    s_hi, s_lo = _split2(c * _sigmoid(c))
    w_hi, w_lo = _split2(w_ref[...])
    o_ref[...] = _dot(s_hi, w_hi) + _dot(s_hi, w_lo) + _dot(s_lo, w_hi) + b_ref[...]


def _modulation(c_rows, w_mod, b_mod):
    rows, d = c_rows.shape
    n = w_mod.shape[1]
    tn = MOD_COL_TILE
    return pl.pallas_call(
        _mod_kernel,
        grid=(n // tn,),
        in_specs=[pl.BlockSpec((rows, d), lambda j: (0, 0)),
                  pl.BlockSpec((d, tn), lambda j: (0, j)),
                  pl.BlockSpec((1, tn), lambda j: (0, j))],
        out_specs=pl.BlockSpec((rows, tn), lambda j: (0, j)),
        out_shape=jax.ShapeDtypeStruct((rows, n), F32),
        compiler_params=pltpu.CompilerParams(dimension_semantics=("arbitrary",),
                                             vmem_limit_bytes=VMEM_LIMIT),
        name="modulation",
    )(c_rows, w_mod, b_mod.reshape(1, n))


def _ffn_math(x, mod_ref, g_ref, wa_ref, wb_ref, wd_ref):
    shift, scale, gate = mod_ref[0, 0:1, :], mod_ref[0, 1:2, :], mod_ref[0, 2:3, :]
    n = ((_rms(x) * g_ref[...]) * (1.0 + scale) + shift).astype(BF16)
    a = _dot(n, wa_ref[...])
    b = _dot(n, wb_ref[...])
    act = ((a * _sigmoid(a)) * b).astype(BF16)
    return x + 0.5 * gate * _dot(act, wd_ref[...])


def _ffn_kernel(h_ref, mod_ref, g_ref, wa_ref, wb_ref, wd_ref, o_ref):
    o_ref[0] = _ffn_math(h_ref[0], mod_ref, g_ref, wa_ref, wb_ref, wd_ref)


def _merge_ffn_kernel(h_ref, ao_ref, go_ref, gs_ref, gt_ref, mg_ref, gg_ref, wb0_ref, wb1_ref, wo_ref,
                      mod_ref, g_ref, wa_ref, wb_ref, wd_ref, gf_ref, o_ref):
    d = h_ref.shape[2]
    go = go_ref[0]
    normed = [_rms(go[:, h * GLA_DV:(h + 1) * GLA_DV]) * gg_ref[...] for h in range(GLA_HEADS)]
    gn = (jnp.concatenate(normed, axis=-1) * gs_ref[0].astype(F32)).astype(BF16)
    y_attn = _dot(ao_ref[0], wb0_ref[...])
    y_gla = _dot(gn, wb1_ref[...])
    gt = gt_ref[0].astype(F32)
    z = (gt[:, :d] * y_attn + gt[:, d:] * y_gla).astype(BF16)
    h2 = h_ref[0] + mg_ref[0] * _dot(z, wo_ref[...])
    out = _ffn_math(h2, mod_ref, g_ref, wa_ref, wb_ref, wd_ref)
    o_ref[0] = _rms(out) * gf_ref[...]


def _ffn_weight_specs(d, f):
    return [_const_spec((1, d)), _const_spec((d, f)), _const_spec((d, f)), _const_spec((f, d))]


def _half_ffn(h, mod3, g, wa, wb, wd, *, tm=TOKEN_TILE):
    nb, t, d = h.shape
    f = wa.shape[1]
    tm = min(tm, t)
    return pl.pallas_call(
        _ffn_kernel,
        grid=(nb, t // tm),
        in_specs=[pl.BlockSpec((1, tm, d), lambda b, i: (b, i, 0)),
                  pl.BlockSpec((1, 3, d), lambda b, i: (b, 0, 0))] + _ffn_weight_specs(d, f),
        out_specs=pl.BlockSpec((1, tm, d), lambda b, i: (b, i, 0)),
        out_shape=jax.ShapeDtypeStruct((nb, t, d), F32),
        compiler_params=pltpu.CompilerParams(dimension_semantics=("arbitrary", "arbitrary"),
                                             vmem_limit_bytes=VMEM_LIMIT),
        name="half_ffn",
    )(h, mod3, g.reshape(1, d), wa, wb, wd)


def _merge_ffn(h, attn_o, gla_o, gs, gt, m_gate, g_gla, wb0, wb1, wo, mod3, g, wa, wb, wd, g_final,
               *, tm=TOKEN_TILE):
    nb, t, d = h.shape
    f = wa.shape[1]
    tm = min(tm, t)
    tok = lambda w: pl.BlockSpec((1, tm, w), lambda b, i: (b, i, 0))
    return pl.pallas_call(
        _merge_ffn_kernel,
        grid=(nb, t // tm),
        in_specs=[tok(d), tok(ATTN_Q_W), tok(GLA_V_W), tok(GLA_V_W), tok(2 * d),
                  pl.BlockSpec((1, 1, d), lambda b, i: (b, 0, 0)),
                  _const_spec((1, GLA_DV)), _const_spec((ATTN_Q_W, d)), _const_spec((GLA_V_W, d)),
                  _const_spec((d, d)),
                  pl.BlockSpec((1, 3, d), lambda b, i: (b, 0, 0))] + _ffn_weight_specs(d, f)
                 + [_const_spec((1, d))],
        out_specs=tok(d),
        out_shape=jax.ShapeDtypeStruct((nb, t, d), F32),
        compiler_params=pltpu.CompilerParams(dimension_semantics=("arbitrary", "arbitrary"),
                                             vmem_limit_bytes=VMEM_LIMIT),
        name="merge_ffn_final",
    )(h, attn_o, gla_o, gs, gt, m_gate, g_gla.reshape(1, GLA_DV), wb0, wb1, wo,
      mod3, g.reshape(1, d), wa, wb, wd, g_final.reshape(1, d))


_C_Q = 0
_C_K = _C_Q + ATTN_Q_W
_C_V = _C_K + ATTN_KV_W
_C_GQ = _C_V + ATTN_KV_W
_C_GK = _C_GQ + GLA_K_W
_C_GV = _C_GK + GLA_K_W
_C_GS = _C_GV + GLA_V_W
_C_GT = _C_GS + GLA_V_W


def _proj_kernel(h_ref, mod_ref, g_ref, w_ref, gqk_ref, seg_ref, cos_ref, sin_ref, wdec_ref,
                 bdec_ref, bgate_ref,
                 qt_ref, k_ref, vt_ref, gq_ref, gk_ref, gv_ref, gvt_ref, laf_ref, lab_ref, gs_ref, gt_ref):
    d = h_ref.shape[2]
    c_low = _C_GT + 2 * d
    x = h_ref[0]
    shift, scale = mod_ref[0, 0:1, :], mod_ref[0, 1:2, :]
    n = ((_rms(x) * g_ref[...]) * (1.0 + scale) + shift).astype(BF16)
    tm = x.shape[0]

    qkv = _dot(n, w_ref[:, _C_Q:_C_GQ])
    qk = qkv[:, :_C_V]
    low = _dot(n, w_ref[:, c_low:c_low + LANES])
    gt = _dot(n, w_ref[:, _C_GT:c_low]) + bgate_ref[...]

    seg = seg_ref[...]
    ms = []
    for j in range((ATTN_Q_W + ATTN_KV_W) // LANES):
        sq = qk[:, j * LANES:(j + 1) * LANES]
        hi, lo = _split2(sq * sq)
        ms.append(_dot(jnp.concatenate([hi, lo], axis=-1), seg))
    ms = jnp.concatenate(ms, axis=-1)
    gs = _dot(n, w_ref[:, _C_GS:_C_GT])

    l_hi = low.astype(BF16).astype(F32)
    l_lo = low - l_hi
    packed = l_hi + pltpu.roll(l_lo, 2 * GLA_RANK, 1) + pltpu.roll(l_hi, 4 * GLA_RANK, 1)
    z = _dot(packed.astype(BF16), wdec_ref[...]) + bdec_ref[...]
    gv = _dot(n, w_ref[:, _C_GV:_C_GS])
    vt = qkv[:, _C_V:_C_GQ].T
    gqk = _dot(n, w_ref[:, _C_GQ:_C_GV])

    qk = qk * lax.rsqrt(ms + EPS) * gqk_ref[...]
    lane = lax.broadcasted_iota(jnp.int32, (tm, LANES), 1)
    first = (lane % HEAD_DIM) < (HEAD_DIM // 2)
    cos, sin = cos_ref[...], sin_ref[...]
    rot = []
    for j in range((ATTN_Q_W + ATTN_KV_W) // LANES):
        xs = qk[:, j * LANES:(j + 1) * LANES]
        other = jnp.where(first, pltpu.roll(xs, LANES - HEAD_DIM // 2, 1), pltpu.roll(xs, HEAD_DIM // 2, 1))
        rot.append(xs * cos + other * sin)
    q_scale = HEAD_DIM ** -0.5 * float(np.log2(np.e))
    for j in range(ATTN_Q_W // LANES):
        qt_ref[0, j * LANES:(j + 1) * LANES, :] = (rot[j] * q_scale).T.astype(BF16)
    k_rot = rot[ATTN_Q_W // LANES]
    ones_lane = jnp.where(lane == HEAD_DIM, 1.0, 0.0)
    for g in range(ATTN_KV_HEADS):
        k_head = k_rot if g == 0 else pltpu.roll(k_rot, LANES - g * HEAD_DIM, 1)
        k_ref[0, g] = jnp.where(lane < HEAD_DIM, k_head, ones_lane).astype(BF16)

    for g in range(ATTN_KV_HEADS):
        vt_ref[0, g] = vt[g * HEAD_DIM:(g + 1) * HEAD_DIM].astype(BF16)

    gq_ref[0] = (gqk[:, :GLA_K_W] * (GLA_DK ** -0.5)).astype(BF16)
    gk_ref[0] = gqk[:, GLA_K_W:].astype(BF16)
    gv_ref[0] = gv.astype(BF16)
    gvt_ref[0] = gv.T.astype(BF16)

    la = (jnp.minimum(z, 0.0) - jnp.log(1.0 + jnp.exp(-jnp.abs(z)))) * (1.0 / GLA_GATE_NORM)
    laf_ref[0] = la[:, :GLA_K_W]
    lab_ref[0] = la[:, GLA_K_W:]

    gs_ref[0] = (gs * _sigmoid(gs)).astype(BF16)
    gt_ref[0] = _sigmoid(gt).astype(BF16)


def _in_proj(h, mod2, g, w_r, gqk, seg, cos_t, sin_t, w_dec, b_dec, b_gate, *, tm=TOKEN_TILE):
    nb, t, d = h.shape
    tm = min(tm, t)
    wp = w_r.shape[1]
    tok = lambda w: pl.BlockSpec((1, tm, w), lambda b, i: (b, i, 0))
    out_shape = [
        jax.ShapeDtypeStruct((nb, ATTN_Q_W, t), BF16),
        jax.ShapeDtypeStruct((nb, ATTN_KV_HEADS, t, LANES), BF16),
        jax.ShapeDtypeStruct((nb, ATTN_KV_HEADS, HEAD_DIM, t), BF16),
        jax.ShapeDtypeStruct((nb, t, GLA_K_W), BF16),
        jax.ShapeDtypeStruct((nb, t, GLA_K_W), BF16),
        jax.ShapeDtypeStruct((nb, t, GLA_V_W), BF16),
        jax.ShapeDtypeStruct((nb, GLA_V_W, t), BF16),
        jax.ShapeDtypeStruct((nb, t, GLA_K_W), F32),
        jax.ShapeDtypeStruct((nb, t, GLA_K_W), F32),
        jax.ShapeDtypeStruct((nb, t, GLA_V_W), BF16),
        jax.ShapeDtypeStruct((nb, t, 2 * d), BF16),
    ]
    out_specs = [
        pl.BlockSpec((1, ATTN_Q_W, tm), lambda b, i: (b, 0, i)),
        pl.BlockSpec((1, ATTN_KV_HEADS, tm, LANES), lambda b, i: (b, 0, i, 0)),
        pl.BlockSpec((1, ATTN_KV_HEADS, HEAD_DIM, tm), lambda b, i: (b, 0, 0, i)),
        tok(GLA_K_W), tok(GLA_K_W), tok(GLA_V_W),
        pl.BlockSpec((1, GLA_V_W, tm), lambda b, i: (b, 0, i)),
        tok(GLA_K_W), tok(GLA_K_W), tok(GLA_V_W), tok(2 * d),
    ]
    in_specs = [
        pl.BlockSpec((1, tm, d), lambda b, i: (b, i, 0)),
        pl.BlockSpec((1, 2, d), lambda b, i: (b, 0, 0)),
        _const_spec((1, d)), _const_spec((d, wp)), _const_spec((1, ATTN_Q_W + ATTN_KV_W)),
        _const_spec((2 * LANES, LANES)),
        pl.BlockSpec((tm, LANES), lambda b, i: (i, 0)),
        pl.BlockSpec((tm, LANES), lambda b, i: (i, 0)),
        _const_spec((LANES, 2 * GLA_K_W)),
        _const_spec((1, 2 * GLA_K_W)), _const_spec((1, 2 * d)),
    ]
    return pl.pallas_call(
        _proj_kernel,
        grid=(nb, t // tm),
        in_specs=in_specs,
        out_specs=out_specs,
        out_shape=out_shape,
        compiler_params=pltpu.CompilerParams(dimension_semantics=("arbitrary", "arbitrary"),
                                             vmem_limit_bytes=VMEM_LIMIT),
        name="in_proj",
    )(h, mod2, g.reshape(1, d), w_r, gqk, seg, cos_t, sin_t, w_dec, b_dec, b_gate)


ATTN_GROUPS = 2
ATTN_ROW_BLOCK = 256
ATTN_FAST_ROW_BLOCK = 256
ATTN_KC = 1024
ATTN_UNROLL = 3
ATTN_BLOCKS_PER_ITER = 33
ATTN_MAX_SHIFT = 50.0


def _attn_plan(t, tc):
    if tc % (2 * LANES) == 0:
        kc = max(c for c in range(LANES, ATTN_KC + 1, LANES) if t % c == 0)
        edge = tc // 2
        chunks = [(t, edge)] + [(i * kc, kc) for i in range(t // kc)] + [(t + edge, edge)]
        return chunks, (1, len(chunks) - 1)
    s_len = t + tc
    kc = max(c for c in range(LANES, ATTN_KC + 1, LANES) if s_len % c == 0)
    chunks = [(i * kc, kc) for i in range(s_len // kc)]
    return chunks, (0, len(chunks))


def _sublane_partial_sums(p):
    return p.astype(F32).reshape(p.shape[0] // 8, 8, p.shape[1]).sum(axis=0)


def _attn_kernel(par_ref, qt_ref, k_ref, vt_ref, o_ref, qs_ref, *group_refs, chunks, run):
    tq = qt_ref.shape[2]
    s_len = k_ref.shape[2]
    gw = Q_GROUP * tq // ATTN_GROUPS
    per = len(group_refs) // ATTN_GROUPS
    m_refs, cmax_refs, l_refs, acc_refs, s_refs = (
        [group_refs[g * per + j] for g in range(ATTN_GROUPS)] for j in range(per))
    for h in range(Q_GROUP):
        qs_ref[0:HEAD_DIM, h * tq:(h + 1) * tq] = qt_ref[0, h * HEAD_DIM:(h + 1) * HEAD_DIM, :]
    extra = lax.broadcasted_iota(jnp.int32, (LANES - HEAD_DIM, Q_GROUP * tq), 0)
    qs_ref[HEAD_DIM:LANES, :] = jnp.where(extra == 0, -par_ref[1], 0.0).astype(BF16)
    for g in range(ATTN_GROUPS):
        acc_refs[g][...] = jnp.zeros(acc_refs[g].shape, F32)
        l_refs[g][...] = jnp.zeros(l_refs[g].shape, F32)

    def rows(off, j, rb):
        start = off + j * rb
        return pl.ds(start if isinstance(start, int) else pl.multiple_of(start, LANES), rb)

    def bounded_scores():
        rb = ATTN_FAST_ROW_BLOCK if s_len % ATTN_FAST_ROW_BLOCK == 0 else LANES
        n_blocks = s_len // rb
        per_iter = max(d for d in range(1, ATTN_BLOCKS_PER_ITER + 1) if n_blocks % d == 0)

        def body(it, carry):
            sums = [None] * ATTN_GROUPS
            dens = [None] * ATTN_GROUPS
            pending = None

            def finish(unit):
                blk, g, p = unit
                part = _dot(vt_ref[0, 0, :, blk], p)
                sums[g] = part if sums[g] is None else sums[g] + part
                den = _sublane_partial_sums(p)
                dens[g] = den if dens[g] is None else dens[g] + den

            for j in range(per_iter):
                blk = rows(it * (per_iter * rb), j, rb)
                for g in range(ATTN_GROUPS):
                    s = _dot(k_ref[0, 0, blk, :], qs_ref[:, g * gw:(g + 1) * gw])
                    if pending is not None:
                        finish(pending)
                    pending = (blk, g, jnp.exp2(s).astype(BF16))
            finish(pending)
            for g in range(ATTN_GROUPS):
                acc_refs[g][...] += sums[g]
                l_refs[g][...] += jnp.sum(dens[g], axis=0, keepdims=True)
            return carry

        lax.fori_loop(0, n_blocks // per_iter, body, 0)

    def online_softmax():
        for g in range(ATTN_GROUPS):
            m_refs[g][...] = jnp.full(m_refs[g].shape, -jnp.inf, F32)

        def stage(score, apply):
            n_s = n_a = 0
            if score is not None:
                off_s, size_s, gs = score
                rb_s = min(ATTN_ROW_BLOCK, size_s)
                n_s = size_s // rb_s
            if apply is not None:
                off_a, size_a, ga = apply
                rb_a = min(ATTN_ROW_BLOCK, size_a)
                n_a = size_a // rb_a
                m_old = m_refs[ga][...]
                m_new = jnp.maximum(m_old, cmax_refs[ga][...])
                m_refs[ga][...] = m_new
            cmax = pv_sum = den = None
            for j in range(max(n_s, n_a)):
                if j < n_a:
                    p = jnp.exp2(s_refs[ga][j * rb_a:(j + 1) * rb_a, :] - m_new).astype(BF16)
                if j < n_s:
                    s = _dot(k_ref[0, 0, rows(off_s, j, rb_s), :], qs_ref[:, gs * gw:(gs + 1) * gw])
                    s_refs[gs][j * rb_s:(j + 1) * rb_s, :] = s
                    bmax = jnp.max(s, axis=0, keepdims=True)
                    cmax = bmax if cmax is None else jnp.maximum(cmax, bmax)
                if j < n_a:
                    part = _dot(vt_ref[0, 0, :, rows(off_a, j, rb_a)], p)
                    pv_sum = part if pv_sum is None else pv_sum + part
                    dj = _sublane_partial_sums(p)
                    den = dj if den is None else den + dj
            if score is not None:
                cmax_refs[gs][...] = cmax
            if apply is not None:
                alpha = jnp.exp2(m_old - m_new)
                acc_refs[ga][...] = alpha * acc_refs[ga][...] + pv_sum
                l_refs[ga][...] = alpha * l_refs[ga][...] + jnp.sum(den, axis=0, keepdims=True)

        last = ATTN_GROUPS - 1

        def chunk_stages(cur, prev):
            stage(cur + (0,), None if prev is None else prev + (last,))
            for g in range(1, ATTN_GROUPS):
                stage(cur + (g,), cur + (g - 1,))

        a, b = run
        trips = b - a - 1
        unroll = ATTN_UNROLL if trips >= ATTN_UNROLL else 1
        first_loop = a + 1 + trips % unroll
        for i in range(first_loop):
            chunk_stages(chunks[i], chunks[i - 1] if i else None)
        if b > first_loop:
            off0, kc = chunks[first_loop]

            def body(i, carry):
                for u in range(unroll):
                    off = off0 + (i * unroll + u) * kc
                    chunk_stages((off, kc), (off - kc, kc))
                return carry

            lax.fori_loop(0, (b - first_loop) // unroll, body, 0)
        for i in range(b, len(chunks)):
            chunk_stages(chunks[i], chunks[i - 1])
        stage(None, chunks[-1] + (last,))

    pl.when(par_ref[0] > 0.0)(bounded_scores)
    pl.when(par_ref[0] <= 0.0)(online_softmax)

    acc = jnp.concatenate([r[...] for r in acc_refs], axis=1)
    ot = acc / jnp.concatenate([r[...] for r in l_refs], axis=1)
    for j in range(Q_GROUP // 2):
        pair = jnp.concatenate([ot[:, (2 * j) * tq:(2 * j + 1) * tq], ot[:, (2 * j + 1) * tq:(2 * j + 2) * tq]],
                               axis=0)
        o_ref[0, :, j * LANES:(j + 1) * LANES] = pair.T.astype(o_ref.dtype)


def _attention(par, qt, k, vt, t_ctx, *, tq=ATTN_Q_TILE):
    nb, _, t = qt.shape
    s_len = k.shape[2]
    tq = min(tq, t)
    chunks, run = _attn_plan(t, t_ctx)
    kc = max(size for _, size in chunks)
    gw = Q_GROUP * tq // ATTN_GROUPS
    return pl.pallas_call(
        functools.partial(_attn_kernel, chunks=tuple(chunks), run=run),
        grid=(nb, ATTN_KV_HEADS, t // tq),
        in_specs=[
            pl.BlockSpec(memory_space=pltpu.SMEM),
            pl.BlockSpec((1, Q_GROUP * HEAD_DIM, tq), lambda b, g, i: (b, g, i)),
            pl.BlockSpec((1, 1, s_len, LANES), lambda b, g, i: (b, g, 0, 0)),
            pl.BlockSpec((1, 1, HEAD_DIM, s_len), lambda b, g, i: (b, g, 0, 0)),
        ],
        out_specs=pl.BlockSpec((1, tq, Q_GROUP * HEAD_DIM), lambda b, g, i: (b, i, g)),
        out_shape=jax.ShapeDtypeStruct((nb, t, ATTN_Q_W), BF16),
        scratch_shapes=[pltpu.VMEM((LANES, Q_GROUP * tq), BF16)] + ATTN_GROUPS * [
            pltpu.VMEM((1, gw), F32),
            pltpu.VMEM((1, gw), F32),
            pltpu.VMEM((1, gw), F32),
            pltpu.VMEM((HEAD_DIM, gw), F32),
            pltpu.VMEM((kc, gw), F32)],
        compiler_params=pltpu.CompilerParams(dimension_semantics=("arbitrary", "arbitrary", "arbitrary"),
                                             vmem_limit_bytes=VMEM_LIMIT),
        name="flash_attention",
    )(par, qt, k, vt)


def _gla_constants():
    c = GLA_CHUNK
    i = np.arange(c)[:, None]
    t = np.arange(c)[None, :]
    fwd = [t > i, t <= i]
    bwd = [t < i, t >= i]
    s = c // 2
    while s >= 2:
        mid = (i // (2 * s)) * (2 * s) + s
        second = (i % (2 * s)) >= s
        fwd.append(np.where(second, (t >= mid) & (t <= i), (t > i) & (t < mid)))
        bwd.append(np.where(second, (t >= mid) & (t < i), (t >= i) & (t < mid)))
        s //= 2
    to = lambda blocks: jnp.asarray(np.concatenate(blocks, axis=0).astype(np.float32), dtype=BF16)
    return to(fwd), to(bwd)


_GLA_LEVELS = int(np.log2(GLA_CHUNK))


def _cum(mat, la):
    hi, lo = _split2(la)
    r = _dot(mat, jnp.concatenate([hi, lo], axis=-1))
    return r[:, :LANES] + r[:, LANES:]


def _gla_kernel(q_ref, k_ref, v_ref, vt_ref, laf_ref, lab_ref, kc_ref, vtc_ref, lafc_ref, labc_ref,
                mf_ref, mb_ref, o_ref, st_ref, dec_ref, *, cpt):
    c = GLA_CHUNK
    n_lat = k_ref.shape[1] // c
    n_ctx = kc_ref.shape[1] // c
    n_all = n_ctx + n_lat
    tile = pl.program_id(2)
    lane = lax.broadcasted_iota(jnp.int32, (c, LANES), 1)
    head_masks = [lane < GLA_DK, lane >= GLA_DK]
    lane2 = lax.broadcasted_iota(jnp.int32, (c, 2 * LANES), 1) % LANES
    pair_masks = [lane2 < GLA_DK, lane2 >= GLA_DK]

    def increments(k_r, vt_r, laf_r, lab_r, n0, slot0, count):
        offs = [pl.multiple_of((n0 + i) * c, c) for i in range(count)]
        rfs = [_cum(mf_ref[0:2 * c, :], laf_r[0, pl.ds(off, c), :]) for off in offs]
        rbs = [_cum(mb_ref[0:2 * c, :], lab_r[0, pl.ds(off, c), :]) for off in offs]
        for i, (off, rf, rb) in enumerate(zip(offs, rfs, rbs)):
            k = k_r[0, pl.ds(off, c), :].astype(F32)
            kfb = jnp.concatenate([k * jnp.exp(rf[0:c]),
                                   k * jnp.exp(rb[0:c])], axis=-1)
            kk = jnp.concatenate([jnp.where(pair_masks[0], kfb, 0.0), jnp.where(pair_masks[1], kfb, 0.0)],
                                 axis=0).astype(BF16)
            vt2 = jnp.concatenate([vt_r[0, 0:GLA_DV, pl.ds(off, c)], vt_r[0, GLA_DV:2 * GLA_DV, pl.ds(off, c)]],
                                  axis=-1)
            st_ref[slot0 + i] = _dot(vt2, kk)
            dec_ref[slot0 + i, 0:1, :] = jnp.exp(rf[2 * c - 1:2 * c, :])
            dec_ref[slot0 + i, 1:2, :] = jnp.exp(rb[c:c + 1, :])

    def group_size(n):
        return max(g for g in (4, 2, 1) if n % g == 0)

    @pl.when(tile == 0)
    def _():
        gc, gl = group_size(n_ctx), group_size(n_lat)

        def ctx_body(i, carry):
            increments(kc_ref, vtc_ref, lafc_ref, labc_ref, i * gc, i * gc, gc)
            return carry
        lax.fori_loop(0, n_ctx // gc, ctx_body, 0)

        def lat_body(i, carry):
            increments(k_ref, vt_ref, laf_ref, lab_ref, i * gl, n_ctx + i * gl, gl)
            return carry
        lax.fori_loop(0, n_lat // gl, lat_body, 0)

        def fwd_body(s, st):
            inc = st_ref[s, :, 0:LANES]
            st_ref[s, :, 0:LANES] = st
            return st * dec_ref[s, 0:1, :] + inc
        lax.fori_loop(0, n_all, fwd_body, jnp.zeros((GLA_DV, LANES), F32))

        def bwd_body(j, st, base, count):
            s = base + count - 1 - j
            inc = st_ref[s, :, LANES:2 * LANES]
            st_ref[s, :, LANES:2 * LANES] = st
            return st * dec_ref[s, 1:2, :] + inc
        st = lax.fori_loop(0, n_ctx, functools.partial(bwd_body, base=0, count=n_ctx),
                           jnp.zeros((GLA_DV, LANES), F32))
        lax.fori_loop(0, n_lat, functools.partial(bwd_body, base=n_ctx, count=n_lat), st)

    xor2 = lax.broadcasted_iota(jnp.int32, (c, 2 * c), 0) ^ (lax.broadcasted_iota(jnp.int32, (c, 2 * c), 1) & (c - 1))
    row_l = lax.broadcasted_iota(jnp.int32, (c, LANES), 0)
    lane2v = lax.broadcasted_iota(jnp.int32, (c, 2 * GLA_DV), 1)
    lane2s = lax.broadcasted_iota(jnp.int32, (GLA_DV, 2 * LANES), 1) % LANES

    def stack_heads(x):
        return jnp.concatenate([jnp.where(head_masks[0], x, 0.0), jnp.where(head_masks[1], x, 0.0)], axis=0)

    go = group_size(cpt)

    def out_body(i, carry):
        idx = range(go)
        ns = [tile * cpt + i * go + u for u in idx]
        offs = [pl.multiple_of(n * c, c) for n in ns]
        locs = [pl.multiple_of((i * go + u) * c, c) for u in idx]
        lafs = [laf_ref[0, pl.ds(off, c), :] for off in offs]
        labs = [lab_ref[0, pl.ds(off, c), :] for off in offs]
        rfs = [_cum(mf_ref[c:, :], la) for la in lafs]
        rbs = [_cum(mb_ref[c:, :], la) for la in labs]
        qs = [q_ref[0, pl.ds(loc, c), :].astype(F32) for loc in locs]
        ks = [k_ref[0, pl.ds(off, c), :].astype(F32) for off in offs]
        a = [2.0 * _dot_nt(qs[u].astype(BF16), stack_heads(ks[u]).astype(BF16)) for u in idx]
        for lvl in range(_GLA_LEVELS):
            sh = _GLA_LEVELS - 1 - lvl
            second = ((row_l >> sh) & 1) == 1
            for u in idx:
                if sh > 0:
                    ef = jnp.exp(rfs[u][(1 + lvl) * c:(2 + lvl) * c])
                    eb = jnp.exp(rbs[u][(1 + lvl) * c:(2 + lvl) * c])
                    ql = (qs[u] * jnp.where(second, ef, eb)).astype(BF16)
                    kl = stack_heads(ks[u] * jnp.where(second, eb, ef)).astype(BF16)
                else:
                    ql = (qs[u] * jnp.exp(jnp.where(second, lafs[u], labs[u]))).astype(BF16)
                    kl = stack_heads(ks[u]).astype(BF16)
                a[u] = jnp.where((xor2 >> sh) == 1, _dot_nt(ql, kl), a[u])
        for u in idx:
            v = v_ref[0, pl.ds(locs[u], c), :]
            v_bd = jnp.concatenate([jnp.where(lane2v < GLA_DV, v, jnp.zeros_like(v)),
                                    jnp.where(lane2v >= GLA_DV, v, jnp.zeros_like(v))], axis=0)
            q_inter = jnp.concatenate([qs[u] * jnp.exp(rfs[u][0:c]), qs[u] * jnp.exp(rbs[u][0:c])],
                                      axis=-1).astype(BF16)
            states = st_ref[n_ctx + ns[u]]
            st2 = jnp.concatenate([jnp.where(lane2s < GLA_DK, states, 0.0),
                                   jnp.where(lane2s >= GLA_DK, states, 0.0)], axis=0).astype(BF16)
            o_ref[0, pl.ds(locs[u], c), :] = _dot(a[u].astype(BF16), v_bd) + _dot_nt(q_inter, st2)
        return carry

    lax.fori_loop(0, cpt // go, out_body, 0)


def _gla(gq, gk, gv, gvt, laf, lab, gk_c, gvt_c, laf_c, lab_c, mf, mb, *, tile=GLA_TILE):
    nb, t, _ = gq.shape
    tc = gk_c.shape[1] // nb
    tile = min(tile, t)
    c = GLA_CHUNK
    n_all = (t + tc) // c
    pair_k = 2 * GLA_DK
    pair_v = 2 * GLA_DV
    return pl.pallas_call(
        functools.partial(_gla_kernel, cpt=tile // c),
        grid=(nb, GLA_HEADS // 2, t // tile),
        in_specs=[
            pl.BlockSpec((1, tile, pair_k), lambda b, p, i: (b, i, p)),
            pl.BlockSpec((1, t, pair_k), lambda b, p, i: (b, 0, p)),
            pl.BlockSpec((1, tile, pair_v), lambda b, p, i: (b, i, p)),
            pl.BlockSpec((1, pair_v, t), lambda b, p, i: (b, p, 0)),
            pl.BlockSpec((1, t, pair_k), lambda b, p, i: (b, 0, p)),
            pl.BlockSpec((1, t, pair_k), lambda b, p, i: (b, 0, p)),
            pl.BlockSpec((1, tc, pair_k), lambda b, p, i: (0, b, p)),
            pl.BlockSpec((1, pair_v, tc), lambda b, p, i: (0, p, b)),
            pl.BlockSpec((1, tc, pair_k), lambda b, p, i: (0, b, p)),
            pl.BlockSpec((1, tc, pair_k), lambda b, p, i: (0, b, p)),
            _const_spec(tuple(mf.shape)), _const_spec(tuple(mb.shape)),
        ],
        out_specs=pl.BlockSpec((1, tile, pair_v), lambda b, p, i: (b, i, p)),
        out_shape=jax.ShapeDtypeStruct((nb, t, GLA_V_W), F32),
        scratch_shapes=[pltpu.VMEM((n_all, GLA_DV, 2 * LANES), F32),
                        pltpu.VMEM((n_all, 8, LANES), F32)],
        compiler_params=pltpu.CompilerParams(dimension_semantics=("arbitrary", "arbitrary", "arbitrary"),
                                             vmem_limit_bytes=VMEM_LIMIT),
        name="gla",
    )(gq, gk, gv, gvt, laf, lab, gk_c, gvt_c, laf_c, lab_c, mf, mb)


def _rope_tables(t):
    rows = t // GRID_W
    row = jnp.repeat(jnp.arange(rows, dtype=F32), GRID_W)
    col = jnp.tile(jnp.arange(GRID_W, dtype=F32), rows)
    freqs = ROPE_THETA ** (-jnp.arange(0, ROPE_AXIS_DIM, 2, dtype=F32) / ROPE_AXIS_DIM)
    ang = jnp.concatenate([row[:, None] * freqs, col[:, None] * freqs], axis=-1)
    cos, sin = jnp.cos(ang), jnp.sin(ang)
    reps = LANES // HEAD_DIM
    return (jnp.tile(jnp.concatenate([cos, cos], axis=-1), (1, reps)),
            jnp.tile(jnp.concatenate([-sin, sin], axis=-1), (1, reps)))


def kernel(x, c, ctx, c_ctx, w_mod, b_mod, g_norm, w_ffn_up, w_ffn_down, w_in, g_q, g_k,
           w_decay, b_decay, g_gla, w_branch, b_gate, w_out, g_final):
    nb, t, d = x.shape
    tc = ctx.shape[1]
    f = w_ffn_down.shape[2]
    assert w_mod.shape[0] == 1, "single layer"
    assert t % GLA_CHUNK == 0 and tc % GLA_CHUNK == 0 and t % GRID_W == 0

    rows = -(-(nb + 1) // 8) * 8
    c_rows = jnp.zeros((rows, d), F32).at[:nb].set(c).at[nb].set(c_ctx)
    m = _modulation(c_rows, w_mod[0], b_mod[0]).reshape(rows, N_MOD, d)
    m_lat, m_ctx = m[:nb], m[nb:nb + 1]

    wa = [w_ffn_up[0, i, :, :f].astype(BF16) for i in range(2)]
    wb = [w_ffn_up[0, i, :, f:].astype(BF16) for i in range(2)]
    wd = [w_ffn_down[0, i].astype(BF16) for i in range(2)]
    c_low = _C_GT
    w_low = w_in[0][:, c_low:c_low + 2 * GLA_RANK]
    w_r = jnp.concatenate([w_in[0][:, :c_low], w_in[0][:, c_low + 2 * GLA_RANK:], w_low,
                           jnp.zeros((d, LANES - 2 * GLA_RANK), F32)], axis=-1).astype(BF16)
    w_bd = jnp.zeros((2 * GLA_RANK, 2 * GLA_K_W), F32)
    w_bd = w_bd.at[:GLA_RANK, :GLA_K_W].set(w_decay[0, 0]).at[GLA_RANK:, GLA_K_W:].set(w_decay[0, 1])
    w_bd_hi = w_bd.astype(BF16)
    w_bd_lo = (w_bd - w_bd_hi.astype(F32)).astype(BF16)
    w_dec = jnp.concatenate([w_bd_hi, w_bd_hi, w_bd_lo, jnp.zeros_like(w_bd_hi)], axis=0)
    b_dec = b_decay[0].reshape(1, 2 * GLA_K_W)
    gqk = jnp.concatenate([jnp.tile(g_q[0], ATTN_HEADS), jnp.tile(g_k[0], ATTN_KV_HEADS)]).reshape(1, -1)
    lane = np.arange(LANES)
    seg = jnp.asarray(np.tile((lane[:, None] // HEAD_DIM == lane[None, :] // HEAD_DIM) / HEAD_DIM, (2, 1)), dtype=BF16)
    cos_t, sin_t = _rope_tables(t)
    ones_t, zeros_t = jnp.ones((nb * tc, LANES), F32), jnp.zeros((nb * tc, LANES), F32)
    mf, mb = _gla_constants()

    proj = functools.partial(_in_proj, g=g_norm[0, 1], w_r=w_r, gqk=gqk, seg=seg, w_dec=w_dec,
                             b_dec=b_dec, b_gate=b_gate[0].reshape(1, 2 * d))

    hc = _half_ffn(ctx.reshape(1, nb * tc, d), m_ctx[:, 0:3], g_norm[0, 0], wa[0], wb[0], wd[0])
    pc = proj(hc, m_ctx[:, 3:5], cos_t=ones_t, sin_t=zeros_t)
    _, k_c, vt_c, _, gk_c, _, gvt_c, laf_c, lab_c, _, _ = pc

    h1 = _half_ffn(x, m_lat[:, 0:3], g_norm[0, 0], wa[0], wb[0], wd[0])
    qt, k, vt, gq, gk, gv, gvt, laf, lab, gs, gt = proj(h1, m_lat[:, 3:5], cos_t=cos_t, sin_t=sin_t)
    k_all = jnp.concatenate(
        [k, k_c.reshape(ATTN_KV_HEADS, nb, tc, LANES).transpose(1, 0, 2, 3)], axis=2)
    vt_all = jnp.concatenate(
        [vt, vt_c.reshape(ATTN_KV_HEADS, HEAD_DIM, nb, tc).transpose(2, 0, 1, 3)], axis=3)
    bound = jnp.ceil(1.01 * HEAD_DIM ** 0.5 * float(np.log2(np.e)) * jnp.max(jnp.abs(g_q[0])) * jnp.max(jnp.abs(g_k[0])))
    bounded = bound <= ATTN_MAX_SHIFT
    par = jnp.stack([bounded.astype(F32), jnp.where(bounded, bound, 0.0)])
    attn_o = _attention(par, qt, k_all, vt_all, tc)
    gla_o = _gla(gq, gk, gv, gvt, laf, lab, gk_c, gvt_c, laf_c, lab_c, mf, mb)
    return _merge_ffn(h1, attn_o, gla_o, gs, gt, m_lat[:, 5:6], g_gla[0],
                      w_branch[0, 0].astype(BF16), w_branch[0, 1].astype(BF16), w_out[0].astype(BF16),
                      m_lat[:, 6:9], g_norm[0, 2], wa[1], wb[1], wd[1], g_final)
```

```python
import functools

import numpy as np
import jax
import jax.numpy as jnp
from jax import lax
from jax.experimental import pallas as pl
from jax.experimental.pallas import tpu as pltpu

F32 = jnp.float32
BF16 = jnp.bfloat16

EPS = 1e-6
GRID_W = 64
N_MOD = 9
ATTN_HEADS = 8
ATTN_KV_HEADS = 2
HEAD_DIM = 64
ROPE_AXIS_DIM = HEAD_DIM // 2
ROPE_THETA = 10000.0
GLA_HEADS = 4
GLA_DK = 64
GLA_DV = 128
GLA_RANK = 16
GLA_GATE_NORM = 16.0
ATTN_Q_W = ATTN_HEADS * HEAD_DIM
ATTN_KV_W = ATTN_KV_HEADS * HEAD_DIM
GLA_K_W = GLA_HEADS * GLA_DK
GLA_V_W = GLA_HEADS * GLA_DV
Q_GROUP = ATTN_HEADS // ATTN_KV_HEADS

LANES = 128
SUBLANES = 8
V7X_VMEM_BYTES = 64 * 1024 * 1024
VMEM_LIMIT = V7X_VMEM_BYTES * 7 // 8
TOKEN_TILE = 512
PROJ_TOKEN_TILE = 1024
MOD_COL_TILE = 1024
ATTN_Q_TILE = 512
GLA_TILE = 1024
GLA_CHUNK = 128


def _dot(a, b):
    return jnp.dot(a, b, preferred_element_type=F32)


def _dot_nt(a, b):
    return lax.dot_general(a, b, (((1,), (1,)), ((), ())), preferred_element_type=F32)


def _sigmoid(x):
    return 0.5 + 0.5 * jnp.tanh(0.5 * x)


def _split2(x):
    hi = x.astype(BF16)
    lo = (x - hi.astype(F32)).astype(BF16)
    return hi, lo


def _rms(x):
    return x * lax.rsqrt(jnp.mean(x * x, axis=-1, keepdims=True) + EPS)


def _const_spec(shape):
    nd = len(shape)
    return pl.BlockSpec(shape, lambda *_: (0,) * nd, pipeline_mode=pl.Buffered(1))


def _mod_kernel(c_ref, w_ref, b_ref, o_ref):
    c = c_ref[...]
    s_hi, s_lo = _split2(c * _sigmoid(c))
    w_hi, w_lo = _split2(w_ref[...])
    o_ref[...] = _dot(s_hi, w_hi) + _dot(s_hi, w_lo) + _dot(s_lo, w_hi) + b_ref[...]


def _modulation(c_rows, w_mod, b_mod):
    rows, d = c_rows.shape
    n = w_mod.shape[1]
    tn = MOD_COL_TILE
    return pl.pallas_call(
        _mod_kernel,
        grid=(n // tn,),
        in_specs=[pl.BlockSpec((rows, d), lambda j: (0, 0)),
                  pl.BlockSpec((d, tn), lambda j: (0, j)),
                  pl.BlockSpec((1, tn), lambda j: (0, j))],
        out_specs=pl.BlockSpec((rows, tn), lambda j: (0, j)),
        out_shape=jax.ShapeDtypeStruct((rows, n), F32),
        compiler_params=pltpu.CompilerParams(dimension_semantics=("arbitrary",),
                                             vmem_limit_bytes=VMEM_LIMIT),
        name="modulation",
    )(c_rows, w_mod, b_mod.reshape(1, n))


def _ffn_math(x, mod_ref, g_ref, wa_ref, wb_ref, wd_ref):
    shift, scale, gate = mod_ref[0, 0:1, :], mod_ref[0, 1:2, :], mod_ref[0, 2:3, :]
    n = ((_rms(x) * g_ref[...]) * (1.0 + scale) + shift).astype(BF16)
    a = _dot(n, wa_ref[...])
    b = _dot(n, wb_ref[...])
    act = ((a * _sigmoid(a)) * b).astype(BF16)
    return x + 0.5 * gate * _dot(act, wd_ref[...])


def _ffn_kernel(h_ref, mod_ref, g_ref, wa_ref, wb_ref, wd_ref, o_ref):
    o_ref[0] = _ffn_math(h_ref[0], mod_ref, g_ref, wa_ref, wb_ref, wd_ref)


def _merge_ffn_kernel(h_ref, ao_ref, go_ref, gs_ref, gt_ref, mg_ref, gg_ref, wb0_ref, wb1_ref, wo_ref,
                      mod_ref, g_ref, wa_ref, wb_ref, wd_ref, gf_ref, o_ref):
    d = h_ref.shape[2]
    go = go_ref[0]
    normed = [_rms(go[:, h * GLA_DV:(h + 1) * GLA_DV]) * gg_ref[...] for h in range(GLA_HEADS)]
    gn = (jnp.concatenate(normed, axis=-1) * gs_ref[0].astype(F32)).astype(BF16)
    y_attn = _dot(ao_ref[0], wb0_ref[...])
    y_gla = _dot(gn, wb1_ref[...])
    gt = gt_ref[0].astype(F32)
    z = (gt[:, :d] * y_attn + gt[:, d:] * y_gla).astype(BF16)
    h2 = h_ref[0] + mg_ref[0] * _dot(z, wo_ref[...])
    out = _ffn_math(h2, mod_ref, g_ref, wa_ref, wb_ref, wd_ref)
    o_ref[0] = _rms(out) * gf_ref[...]


def _ffn_weight_specs(d, f):
    return [_const_spec((1, d)), _const_spec((d, f)), _const_spec((d, f)), _const_spec((f, d))]


def _half_ffn(h, mod3, g, wa, wb, wd, *, tm=TOKEN_TILE):
    nb, t, d = h.shape
    f = wa.shape[1]
    tm = min(tm, t)
    return pl.pallas_call(
        _ffn_kernel,
        grid=(nb, t // tm),
        in_specs=[pl.BlockSpec((1, tm, d), lambda b, i: (b, i, 0)),
                  pl.BlockSpec((1, 3, d), lambda b, i: (b, 0, 0))] + _ffn_weight_specs(d, f),
        out_specs=pl.BlockSpec((1, tm, d), lambda b, i: (b, i, 0)),
        out_shape=jax.ShapeDtypeStruct((nb, t, d), F32),
        compiler_params=pltpu.CompilerParams(dimension_semantics=("arbitrary", "arbitrary"),
                                             vmem_limit_bytes=VMEM_LIMIT),
        name="half_ffn",
    )(h, mod3, g.reshape(1, d), wa, wb, wd)


def _merge_ffn(h, attn_o, gla_o, gs, gt, m_gate, g_gla, wb0, wb1, wo, mod3, g, wa, wb, wd, g_final,
               *, tm=TOKEN_TILE):
    nb, t, d = h.shape
    f = wa.shape[1]
    tm = min(tm, t)
    tok = lambda w: pl.BlockSpec((1, tm, w), lambda b, i: (b, i, 0))
    return pl.pallas_call(
        _merge_ffn_kernel,
        grid=(nb, t // tm),
        in_specs=[tok(d), tok(ATTN_Q_W), tok(GLA_V_W), tok(GLA_V_W), tok(2 * d),
                  pl.BlockSpec((1, 1, d), lambda b, i: (b, 0, 0)),
                  _const_spec((1, GLA_DV)), _const_spec((ATTN_Q_W, d)), _const_spec((GLA_V_W, d)),
                  _const_spec((d, d)),
                  pl.BlockSpec((1, 3, d), lambda b, i: (b, 0, 0))] + _ffn_weight_specs(d, f)
                 + [_const_spec((1, d))],
        out_specs=tok(d),
        out_shape=jax.ShapeDtypeStruct((nb, t, d), F32),
        compiler_params=pltpu.CompilerParams(dimension_semantics=("arbitrary", "arbitrary"),
                                             vmem_limit_bytes=VMEM_LIMIT),
        name="merge_ffn_final",
    )(h, attn_o, gla_o, gs, gt, m_gate, g_gla.reshape(1, GLA_DV), wb0, wb1, wo,
      mod3, g.reshape(1, d), wa, wb, wd, g_final.reshape(1, d))


_C_Q = 0
_C_K = _C_Q + ATTN_Q_W
_C_V = _C_K + ATTN_KV_W
_C_GQ = _C_V + ATTN_KV_W
_C_GK = _C_GQ + GLA_K_W
_C_GV = _C_GK + GLA_K_W
_C_GS = _C_GV + GLA_V_W
_C_GT = _C_GS + GLA_V_W


def _proj_kernel(h_ref, mod_ref, g_ref, w_ref, gqk_ref, seg_ref, cos_ref, sin_ref, wdec_ref,
                 bdec_ref, bgate_ref,
                 qt_ref, k_ref, vt_ref, gq_ref, gk_ref, gv_ref, gvt_ref, laf_ref, lab_ref, gs_ref, gt_ref):
    d = h_ref.shape[2]
    c_low = _C_GT + 2 * d
    x = h_ref[0]
    shift, scale = mod_ref[0, 0:1, :], mod_ref[0, 1:2, :]
    n = ((_rms(x) * g_ref[...]) * (1.0 + scale) + shift).astype(BF16)
    tm = x.shape[0]

    qkv = _dot(n, w_ref[:, _C_Q:_C_GQ])
    qk = qkv[:, :_C_V]
    low = _dot(n, w_ref[:, c_low:c_low + LANES])
    gt = _dot(n, w_ref[:, _C_GT:c_low]) + bgate_ref[...]

    seg = seg_ref[...]
    ms = []
    for j in range((ATTN_Q_W + ATTN_KV_W) // LANES):
        sq = qk[:, j * LANES:(j + 1) * LANES]
        hi, lo = _split2(sq * sq)
        ms.append(_dot(jnp.concatenate([hi, lo], axis=-1), seg))
    ms = jnp.concatenate(ms, axis=-1)
    gs = _dot(n, w_ref[:, _C_GS:_C_GT])

    l_hi = low.astype(BF16).astype(F32)
    l_lo = low - l_hi
    packed = l_hi + pltpu.roll(l_lo, 2 * GLA_RANK, 1) + pltpu.roll(l_hi, 4 * GLA_RANK, 1)
    z = _dot(packed.astype(BF16), wdec_ref[...]) + bdec_ref[...]
    gv = _dot(n, w_ref[:, _C_GV:_C_GS])
    vt = qkv[:, _C_V:_C_GQ].T
    gqk = _dot(n, w_ref[:, _C_GQ:_C_GV])

    qk = qk * lax.rsqrt(ms + EPS) * gqk_ref[...]
    lane = lax.broadcasted_iota(jnp.int32, (tm, LANES), 1)
    first = (lane % HEAD_DIM) < (HEAD_DIM // 2)
    cos, sin = cos_ref[...], sin_ref[...]
    rot = []
    for j in range((ATTN_Q_W + ATTN_KV_W) // LANES):
        xs = qk[:, j * LANES:(j + 1) * LANES]
        other = jnp.where(first, pltpu.roll(xs, LANES - HEAD_DIM // 2, 1), pltpu.roll(xs, HEAD_DIM // 2, 1))
        rot.append(xs * cos + other * sin)
    q_scale = HEAD_DIM ** -0.5 * float(np.log2(np.e))
    for j in range(ATTN_Q_W // LANES):
        qt_ref[0, j * LANES:(j + 1) * LANES, :] = (rot[j] * q_scale).T.astype(BF16)
    k_rot = rot[ATTN_Q_W // LANES]
    ones_lane = jnp.where(lane == HEAD_DIM, 1.0, 0.0)
    for g in range(ATTN_KV_HEADS):
        k_head = k_rot if g == 0 else pltpu.roll(k_rot, LANES - g * HEAD_DIM, 1)
        k_ref[0, g] = jnp.where(lane < HEAD_DIM, k_head, ones_lane).astype(BF16)

    for g in range(ATTN_KV_HEADS):
        vt_ref[0, g] = vt[g * HEAD_DIM:(g + 1) * HEAD_DIM].astype(BF16)

    gq_ref[0] = (gqk[:, :GLA_K_W] * (GLA_DK ** -0.5)).astype(BF16)
    gk_ref[0] = gqk[:, GLA_K_W:].astype(BF16)
    gv_ref[0] = gv.astype(BF16)
    gvt_ref[0] = gv.T.astype(BF16)

    la = (jnp.minimum(z, 0.0) - jnp.log(1.0 + jnp.exp(-jnp.abs(z)))) * (1.0 / GLA_GATE_NORM)
    laf_ref[0] = la[:, :GLA_K_W]
    lab_ref[0] = la[:, GLA_K_W:]

    gs_ref[0] = (gs * _sigmoid(gs)).astype(BF16)
    gt_ref[0] = _sigmoid(gt).astype(BF16)


def _in_proj(h, mod2, g, w_r, gqk, seg, cos_t, sin_t, w_dec, b_dec, b_gate, *, tm=PROJ_TOKEN_TILE):
    nb, t, d = h.shape
    tm = min(tm, t)
    wp = w_r.shape[1]
    tok = lambda w: pl.BlockSpec((1, tm, w), lambda b, i: (b, i, 0))
    out_shape = [
        jax.ShapeDtypeStruct((nb, ATTN_Q_W, t), BF16),
        jax.ShapeDtypeStruct((nb, ATTN_KV_HEADS, t, LANES), BF16),
        jax.ShapeDtypeStruct((nb, ATTN_KV_HEADS, HEAD_DIM, t), BF16),
        jax.ShapeDtypeStruct((nb, t, GLA_K_W), BF16),
        jax.ShapeDtypeStruct((nb, t, GLA_K_W), BF16),
        jax.ShapeDtypeStruct((nb, t, GLA_V_W), BF16),
        jax.ShapeDtypeStruct((nb, GLA_V_W, t), BF16),
        jax.ShapeDtypeStruct((nb, t, GLA_K_W), F32),
        jax.ShapeDtypeStruct((nb, t, GLA_K_W), F32),
        jax.ShapeDtypeStruct((nb, t, GLA_V_W), BF16),
        jax.ShapeDtypeStruct((nb, t, 2 * d), BF16),
    ]
    out_specs = [
        pl.BlockSpec((1, ATTN_Q_W, tm), lambda b, i: (b, 0, i)),
        pl.BlockSpec((1, ATTN_KV_HEADS, tm, LANES), lambda b, i: (b, 0, i, 0)),
        pl.BlockSpec((1, ATTN_KV_HEADS, HEAD_DIM, tm), lambda b, i: (b, 0, 0, i)),
        tok(GLA_K_W), tok(GLA_K_W), tok(GLA_V_W),
        pl.BlockSpec((1, GLA_V_W, tm), lambda b, i: (b, 0, i)),
        tok(GLA_K_W), tok(GLA_K_W), tok(GLA_V_W), tok(2 * d),
    ]
    in_specs = [
        pl.BlockSpec((1, tm, d), lambda b, i: (b, i, 0)),
        pl.BlockSpec((1, 2, d), lambda b, i: (b, 0, 0)),
        _const_spec((1, d)), _const_spec((d, wp)), _const_spec((1, ATTN_Q_W + ATTN_KV_W)),
        _const_spec((2 * LANES, LANES)),
        pl.BlockSpec((tm, LANES), lambda b, i: (i, 0)),
        pl.BlockSpec((tm, LANES), lambda b, i: (i, 0)),
        _const_spec((LANES, 2 * GLA_K_W)),
        _const_spec((1, 2 * GLA_K_W)), _const_spec((1, 2 * d)),
    ]
    return pl.pallas_call(
        _proj_kernel,
        grid=(nb, t // tm),
        in_specs=in_specs,
        out_specs=out_specs,
        out_shape=out_shape,
        compiler_params=pltpu.CompilerParams(dimension_semantics=("arbitrary", "arbitrary"),
                                             vmem_limit_bytes=VMEM_LIMIT),
        name="in_proj",
    )(h, mod2, g.reshape(1, d), w_r, gqk, seg, cos_t, sin_t, w_dec, b_dec, b_gate)


ATTN_GROUPS = 2
ATTN_ROW_BLOCK = 256
ATTN_FAST_ROW_BLOCK = 256
ATTN_KC = 1024
ATTN_UNROLL = 3
ATTN_BLOCKS_PER_ITER = 33
ATTN_MAX_SHIFT = 50.0


def _attn_plan(t, tc):
    if tc % (2 * LANES) == 0:
        kc = max(c for c in range(LANES, ATTN_KC + 1, LANES) if t % c == 0)
        edge = tc // 2
        chunks = [(t, edge)] + [(i * kc, kc) for i in range(t // kc)] + [(t + edge, edge)]
        return chunks, (1, len(chunks) - 1)
    s_len = t + tc
    kc = max(c for c in range(LANES, ATTN_KC + 1, LANES) if s_len % c == 0)
    chunks = [(i * kc, kc) for i in range(s_len // kc)]
    return chunks, (0, len(chunks))


def _sublane_partial_sums(p):
    return p.astype(F32).reshape(p.shape[0] // SUBLANES, SUBLANES, p.shape[1]).sum(axis=0)


def _attn_kernel(par_ref, qt_ref, k_ref, vt_ref, o_ref, qs_ref, *group_refs, chunks, run):
    tq = qt_ref.shape[2]
    s_len = k_ref.shape[2]
    gw = Q_GROUP * tq // ATTN_GROUPS
    per = len(group_refs) // ATTN_GROUPS
    m_refs, cmax_refs, l_refs, acc_refs, s_refs = (
        [group_refs[g * per + j] for g in range(ATTN_GROUPS)] for j in range(per))
    for h in range(Q_GROUP):
        qs_ref[0:HEAD_DIM, h * tq:(h + 1) * tq] = qt_ref[0, h * HEAD_DIM:(h + 1) * HEAD_DIM, :]
    extra = lax.broadcasted_iota(jnp.int32, (LANES - HEAD_DIM, Q_GROUP * tq), 0)
    qs_ref[HEAD_DIM:LANES, :] = jnp.where(extra == 0, -par_ref[1], 0.0).astype(BF16)
    for g in range(ATTN_GROUPS):
        acc_refs[g][...] = jnp.zeros(acc_refs[g].shape, F32)
        l_refs[g][...] = jnp.zeros(l_refs[g].shape, F32)

    def rows(off, j, rb):
        start = off + j * rb
        return pl.ds(start if isinstance(start, int) else pl.multiple_of(start, LANES), rb)

    def bounded_scores():
        rb = ATTN_FAST_ROW_BLOCK if s_len % ATTN_FAST_ROW_BLOCK == 0 else LANES
        n_blocks = s_len // rb
        per_iter = max(d for d in range(1, ATTN_BLOCKS_PER_ITER + 1) if n_blocks % d == 0)

        def body(it, carry):
            sums = [None] * ATTN_GROUPS
            dens = [None] * ATTN_GROUPS
            pending = None

            def finish(unit):
                blk, g, p = unit
                part = _dot(vt_ref[0, 0, :, blk], p)
                sums[g] = part if sums[g] is None else sums[g] + part
                den = _sublane_partial_sums(p)
                dens[g] = den if dens[g] is None else dens[g] + den

            for j in range(per_iter):
                blk = rows(it * (per_iter * rb), j, rb)
                for g in range(ATTN_GROUPS):
                    s = _dot(k_ref[0, 0, blk, :], qs_ref[:, g * gw:(g + 1) * gw])
                    if pending is not None:
                        finish(pending)
                    pending = (blk, g, jnp.exp2(s).astype(BF16))
            finish(pending)
            for g in range(ATTN_GROUPS):
                acc_refs[g][...] += sums[g]
                l_refs[g][...] += jnp.sum(dens[g], axis=0, keepdims=True)
            return carry

        lax.fori_loop(0, n_blocks // per_iter, body, 0)

    def online_softmax():
        for g in range(ATTN_GROUPS):
            m_refs[g][...] = jnp.full(m_refs[g].shape, -jnp.inf, F32)

        def stage(score, apply):
            n_s = n_a = 0
            if score is not None:
                off_s, size_s, gs = score
                rb_s = min(ATTN_ROW_BLOCK, size_s)
                n_s = size_s // rb_s
            if apply is not None:
                off_a, size_a, ga = apply
                rb_a = min(ATTN_ROW_BLOCK, size_a)
                n_a = size_a // rb_a
                m_old = m_refs[ga][...]
                m_new = jnp.maximum(m_old, cmax_refs[ga][...])
                m_refs[ga][...] = m_new
            cmax = pv_sum = den = None
            for j in range(max(n_s, n_a)):
                if j < n_a:
                    p = jnp.exp2(s_refs[ga][j * rb_a:(j + 1) * rb_a, :] - m_new).astype(BF16)
                if j < n_s:
                    s = _dot(k_ref[0, 0, rows(off_s, j, rb_s), :], qs_ref[:, gs * gw:(gs + 1) * gw])
                    s_refs[gs][j * rb_s:(j + 1) * rb_s, :] = s
                    bmax = jnp.max(s, axis=0, keepdims=True)
                    cmax = bmax if cmax is None else jnp.maximum(cmax, bmax)
                if j < n_a:
                    part = _dot(vt_ref[0, 0, :, rows(off_a, j, rb_a)], p)
                    pv_sum = part if pv_sum is None else pv_sum + part
                    dj = _sublane_partial_sums(p)
                    den = dj if den is None else den + dj
            if score is not None:
                cmax_refs[gs][...] = cmax
            if apply is not None:
                alpha = jnp.exp2(m_old - m_new)
                acc_refs[ga][...] = alpha * acc_refs[ga][...] + pv_sum
                l_refs[ga][...] = alpha * l_refs[ga][...] + jnp.sum(den, axis=0, keepdims=True)

        last = ATTN_GROUPS - 1

        def chunk_stages(cur, prev):
            stage(cur + (0,), None if prev is None else prev + (last,))
            for g in range(1, ATTN_GROUPS):
                stage(cur + (g,), cur + (g - 1,))

        a, b = run
        trips = b - a - 1
        unroll = ATTN_UNROLL if trips >= ATTN_UNROLL else 1
        first_loop = a + 1 + trips % unroll
        for i in range(first_loop):
            chunk_stages(chunks[i], chunks[i - 1] if i else None)
        if b > first_loop:
            off0, kc = chunks[first_loop]

            def body(i, carry):
                for u in range(unroll):
                    off = off0 + (i * unroll + u) * kc
                    chunk_stages((off, kc), (off - kc, kc))
                return carry

            lax.fori_loop(0, (b - first_loop) // unroll, body, 0)
        for i in range(b, len(chunks)):
            chunk_stages(chunks[i], chunks[i - 1])
        stage(None, chunks[-1] + (last,))

    pl.when(par_ref[0] > 0.0)(bounded_scores)
    pl.when(par_ref[0] <= 0.0)(online_softmax)

    acc = jnp.concatenate([r[...] for r in acc_refs], axis=1)
    ot = acc / jnp.concatenate([r[...] for r in l_refs], axis=1)
    for j in range(Q_GROUP // 2):
        pair = jnp.concatenate([ot[:, (2 * j) * tq:(2 * j + 1) * tq], ot[:, (2 * j + 1) * tq:(2 * j + 2) * tq]],
                               axis=0)
        o_ref[0, :, j * LANES:(j + 1) * LANES] = pair.T.astype(o_ref.dtype)


def _attention(par, qt, k, vt, t_ctx, *, tq=ATTN_Q_TILE):
    nb, _, t = qt.shape
    s_len = k.shape[2]
    tq = min(tq, t)
    chunks, run = _attn_plan(t, t_ctx)
    kc = max(size for _, size in chunks)
    gw = Q_GROUP * tq // ATTN_GROUPS
    return pl.pallas_call(
        functools.partial(_attn_kernel, chunks=tuple(chunks), run=run),
        grid=(nb, ATTN_KV_HEADS, t // tq),
        in_specs=[
            pl.BlockSpec(memory_space=pltpu.SMEM),
            pl.BlockSpec((1, Q_GROUP * HEAD_DIM, tq), lambda b, g, i: (b, g, i)),
            pl.BlockSpec((1, 1, s_len, LANES), lambda b, g, i: (b, g, 0, 0)),
            pl.BlockSpec((1, 1, HEAD_DIM, s_len), lambda b, g, i: (b, g, 0, 0)),
        ],
        out_specs=pl.BlockSpec((1, tq, Q_GROUP * HEAD_DIM), lambda b, g, i: (b, i, g)),
        out_shape=jax.ShapeDtypeStruct((nb, t, ATTN_Q_W), BF16),
        scratch_shapes=[pltpu.VMEM((LANES, Q_GROUP * tq), BF16)] + ATTN_GROUPS * [
            pltpu.VMEM((1, gw), F32),
            pltpu.VMEM((1, gw), F32),
            pltpu.VMEM((1, gw), F32),
            pltpu.VMEM((HEAD_DIM, gw), F32),
            pltpu.VMEM((kc, gw), F32)],
        compiler_params=pltpu.CompilerParams(dimension_semantics=("arbitrary", "arbitrary", "arbitrary"),
                                             vmem_limit_bytes=VMEM_LIMIT),
        name="flash_attention",
    )(par, qt, k, vt)


def _gla_constants():
    c = GLA_CHUNK
    i = np.arange(c)[:, None]
    t = np.arange(c)[None, :]
    fwd = [t > i, t <= i]
    bwd = [t < i, t >= i]
    s = c // 2
    while s >= 2:
        mid = (i // (2 * s)) * (2 * s) + s
        second = (i % (2 * s)) >= s
        fwd.append(np.where(second, (t >= mid) & (t <= i), (t > i) & (t < mid)))
        bwd.append(np.where(second, (t >= mid) & (t < i), (t >= i) & (t < mid)))
        s //= 2
    to = lambda blocks: jnp.asarray(np.concatenate(blocks, axis=0).astype(np.float32), dtype=BF16)
    return to(fwd), to(bwd)


_GLA_LEVELS = int(np.log2(GLA_CHUNK))


def _cum(mat, la):
    hi, lo = _split2(la)
    r = _dot(mat, jnp.concatenate([hi, lo], axis=-1))
    return r[:, :LANES] + r[:, LANES:]


def _gla_kernel(q_ref, k_ref, v_ref, vt_ref, laf_ref, lab_ref, kc_ref, vtc_ref, lafc_ref, labc_ref,
                mf_ref, mb_ref, o_ref, st_ref, dec_ref, *, cpt):
    c = GLA_CHUNK
    n_lat = k_ref.shape[1] // c
    n_ctx = kc_ref.shape[1] // c
    n_all = n_ctx + n_lat
    tile = pl.program_id(2)
    lane = lax.broadcasted_iota(jnp.int32, (c, LANES), 1)
    head_masks = [lane < GLA_DK, lane >= GLA_DK]
    lane2 = lax.broadcasted_iota(jnp.int32, (c, 2 * LANES), 1) % LANES
    pair_masks = [lane2 < GLA_DK, lane2 >= GLA_DK]

    def increments(k_r, vt_r, laf_r, lab_r, n0, slot0, count):
        offs = [pl.multiple_of((n0 + i) * c, c) for i in range(count)]
        rfs = [_cum(mf_ref[0:2 * c, :], laf_r[0, pl.ds(off, c), :]) for off in offs]
        rbs = [_cum(mb_ref[0:2 * c, :], lab_r[0, pl.ds(off, c), :]) for off in offs]
        for i, (off, rf, rb) in enumerate(zip(offs, rfs, rbs)):
            k = k_r[0, pl.ds(off, c), :].astype(F32)
            kfb = jnp.concatenate([k * jnp.exp(rf[0:c]),
                                   k * jnp.exp(rb[0:c])], axis=-1)
            kk = jnp.concatenate([jnp.where(pair_masks[0], kfb, 0.0), jnp.where(pair_masks[1], kfb, 0.0)],
                                 axis=0).astype(BF16)
            vt2 = jnp.concatenate([vt_r[0, 0:GLA_DV, pl.ds(off, c)], vt_r[0, GLA_DV:2 * GLA_DV, pl.ds(off, c)]],
                                  axis=-1)
            st_ref[slot0 + i] = _dot(vt2, kk)
            dec_ref[slot0 + i, 0:1, :] = jnp.exp(rf[2 * c - 1:2 * c, :])
            dec_ref[slot0 + i, 1:2, :] = jnp.exp(rb[c:c + 1, :])

    def group_size(n):
        return max(g for g in (4, 2, 1) if n % g == 0)

    @pl.when(tile == 0)
    def _():
        gc, gl = group_size(n_ctx), group_size(n_lat)

        def ctx_body(i, carry):
            increments(kc_ref, vtc_ref, lafc_ref, labc_ref, i * gc, i * gc, gc)
            return carry
        lax.fori_loop(0, n_ctx // gc, ctx_body, 0)

        def lat_body(i, carry):
            increments(k_ref, vt_ref, laf_ref, lab_ref, i * gl, n_ctx + i * gl, gl)
            return carry
        lax.fori_loop(0, n_lat // gl, lat_body, 0)

        def fwd_body(s, st):
            inc = st_ref[s, :, 0:LANES]
            st_ref[s, :, 0:LANES] = st
            return st * dec_ref[s, 0:1, :] + inc
        lax.fori_loop(0, n_all, fwd_body, jnp.zeros((GLA_DV, LANES), F32))

        def bwd_body(j, st, base, count):
            s = base + count - 1 - j
            inc = st_ref[s, :, LANES:2 * LANES]
            st_ref[s, :, LANES:2 * LANES] = st
            return st * dec_ref[s, 1:2, :] + inc
        st = lax.fori_loop(0, n_ctx, functools.partial(bwd_body, base=0, count=n_ctx),
                           jnp.zeros((GLA_DV, LANES), F32))
        lax.fori_loop(0, n_lat, functools.partial(bwd_body, base=n_ctx, count=n_lat), st)

    xor2 = lax.broadcasted_iota(jnp.int32, (c, 2 * c), 0) ^ (lax.broadcasted_iota(jnp.int32, (c, 2 * c), 1) & (c - 1))
    row_l = lax.broadcasted_iota(jnp.int32, (c, LANES), 0)
    lane2v = lax.broadcasted_iota(jnp.int32, (c, 2 * GLA_DV), 1)
    lane2s = lax.broadcasted_iota(jnp.int32, (GLA_DV, 2 * LANES), 1) % LANES

    def stack_heads(x):
        return jnp.concatenate([jnp.where(head_masks[0], x, 0.0), jnp.where(head_masks[1], x, 0.0)], axis=0)

    go = group_size(cpt)

    def out_body(i, carry):
        idx = range(go)
        ns = [tile * cpt + i * go + u for u in idx]
        offs = [pl.multiple_of(n * c, c) for n in ns]
        locs = [pl.multiple_of((i * go + u) * c, c) for u in idx]
        lafs = [laf_ref[0, pl.ds(off, c), :] for off in offs]
        labs = [lab_ref[0, pl.ds(off, c), :] for off in offs]
        rfs = [_cum(mf_ref[c:, :], la) for la in lafs]
        rbs = [_cum(mb_ref[c:, :], la) for la in labs]
        qs = [q_ref[0, pl.ds(loc, c), :].astype(F32) for loc in locs]
        ks = [k_ref[0, pl.ds(off, c), :].astype(F32) for off in offs]
        a = [2.0 * _dot_nt(qs[u].astype(BF16), stack_heads(ks[u]).astype(BF16)) for u in idx]
        for lvl in range(_GLA_LEVELS):
            sh = _GLA_LEVELS - 1 - lvl
            second = ((row_l >> sh) & 1) == 1
            for u in idx:
                if sh > 0:
                    ef = jnp.exp(rfs[u][(1 + lvl) * c:(2 + lvl) * c])
                    eb = jnp.exp(rbs[u][(1 + lvl) * c:(2 + lvl) * c])
                    ql = (qs[u] * jnp.where(second, ef, eb)).astype(BF16)
                    kl = stack_heads(ks[u] * jnp.where(second, eb, ef)).astype(BF16)
                else:
                    ql = (qs[u] * jnp.exp(jnp.where(second, lafs[u], labs[u]))).astype(BF16)
                    kl = stack_heads(ks[u]).astype(BF16)
                a[u] = jnp.where((xor2 >> sh) == 1, _dot_nt(ql, kl), a[u])
        for u in idx:
            v = v_ref[0, pl.ds(locs[u], c), :]
            v_bd = jnp.concatenate([jnp.where(lane2v < GLA_DV, v, jnp.zeros_like(v)),
                                    jnp.where(lane2v >= GLA_DV, v, jnp.zeros_like(v))], axis=0)
            q_inter = jnp.concatenate([qs[u] * jnp.exp(rfs[u][0:c]), qs[u] * jnp.exp(rbs[u][0:c])],
                                      axis=-1).astype(BF16)
            states = st_ref[n_ctx + ns[u]]
            st2 = jnp.concatenate([jnp.where(lane2s < GLA_DK, states, 0.0),
                                   jnp.where(lane2s >= GLA_DK, states, 0.0)], axis=0).astype(BF16)
            o_ref[0, pl.ds(locs[u], c), :] = _dot(a[u].astype(BF16), v_bd) + _dot_nt(q_inter, st2)
        return carry

    lax.fori_loop(0, cpt // go, out_body, 0)


def _gla(gq, gk, gv, gvt, laf, lab, gk_c, gvt_c, laf_c, lab_c, mf, mb, *, tile=GLA_TILE):
    nb, t, _ = gq.shape
    tc = gk_c.shape[1] // nb
    tile = min(tile, t)
    c = GLA_CHUNK
    n_all = (t + tc) // c
    pair_k = 2 * GLA_DK
    pair_v = 2 * GLA_DV
    return pl.pallas_call(
        functools.partial(_gla_kernel, cpt=tile // c),
        grid=(nb, GLA_HEADS // 2, t // tile),
        in_specs=[
            pl.BlockSpec((1, tile, pair_k), lambda b, p, i: (b, i, p)),
            pl.BlockSpec((1, t, pair_k), lambda b, p, i: (b, 0, p)),
            pl.BlockSpec((1, tile, pair_v), lambda b, p, i: (b, i, p)),
            pl.BlockSpec((1, pair_v, t), lambda b, p, i: (b, p, 0)),
            pl.BlockSpec((1, t, pair_k), lambda b, p, i: (b, 0, p)),
            pl.BlockSpec((1, t, pair_k), lambda b, p, i: (b, 0, p)),
            pl.BlockSpec((1, tc, pair_k), lambda b, p, i: (0, b, p)),
            pl.BlockSpec((1, pair_v, tc), lambda b, p, i: (0, p, b)),
            pl.BlockSpec((1, tc, pair_k), lambda b, p, i: (0, b, p)),
            pl.BlockSpec((1, tc, pair_k), lambda b, p, i: (0, b, p)),
            _const_spec(tuple(mf.shape)), _const_spec(tuple(mb.shape)),
        ],
        out_specs=pl.BlockSpec((1, tile, pair_v), lambda b, p, i: (b, i, p)),
        out_shape=jax.ShapeDtypeStruct((nb, t, GLA_V_W), F32),
        scratch_shapes=[pltpu.VMEM((n_all, GLA_DV, 2 * LANES), F32),
                        pltpu.VMEM((n_all, SUBLANES, LANES), F32)],
        compiler_params=pltpu.CompilerParams(dimension_semantics=("arbitrary", "arbitrary", "arbitrary"),
                                             vmem_limit_bytes=VMEM_LIMIT),
        name="gla",
    )(gq, gk, gv, gvt, laf, lab, gk_c, gvt_c, laf_c, lab_c, mf, mb)


def _rope_tables(t):
    rows = t // GRID_W
    row = jnp.repeat(jnp.arange(rows, dtype=F32), GRID_W)
    col = jnp.tile(jnp.arange(GRID_W, dtype=F32), rows)
    freqs = ROPE_THETA ** (-jnp.arange(0, ROPE_AXIS_DIM, 2, dtype=F32) / ROPE_AXIS_DIM)
    ang = jnp.concatenate([row[:, None] * freqs, col[:, None] * freqs], axis=-1)
    cos, sin = jnp.cos(ang), jnp.sin(ang)
    reps = LANES // HEAD_DIM
    return (jnp.tile(jnp.concatenate([cos, cos], axis=-1), (1, reps)),
            jnp.tile(jnp.concatenate([-sin, sin], axis=-1), (1, reps)))


def kernel(x, c, ctx, c_ctx, w_mod, b_mod, g_norm, w_ffn_up, w_ffn_down, w_in, g_q, g_k,
           w_decay, b_decay, g_gla, w_branch, b_gate, w_out, g_final):
    nb, t, d = x.shape
    tc = ctx.shape[1]
    f = w_ffn_down.shape[2]
    assert w_mod.shape[0] == 1, "single layer"
    assert t % GLA_CHUNK == 0 and tc % GLA_CHUNK == 0 and t % GRID_W == 0

    rows = -(-(nb + 1) // SUBLANES) * SUBLANES
    c_rows = jnp.zeros((rows, d), F32).at[:nb].set(c).at[nb].set(c_ctx)
    m = _modulation(c_rows, w_mod[0], b_mod[0]).reshape(rows, N_MOD, d)
    m_lat, m_ctx = m[:nb], m[nb:nb + 1]

    wa = [w_ffn_up[0, i, :, :f].astype(BF16) for i in range(2)]
    wb = [w_ffn_up[0, i, :, f:].astype(BF16) for i in range(2)]
    wd = [w_ffn_down[0, i].astype(BF16) for i in range(2)]
    c_low = _C_GT
    w_low = w_in[0][:, c_low:c_low + 2 * GLA_RANK]
    w_r = jnp.concatenate([w_in[0][:, :c_low], w_in[0][:, c_low + 2 * GLA_RANK:], w_low,
                           jnp.zeros((d, LANES - 2 * GLA_RANK), F32)], axis=-1).astype(BF16)
    w_bd = jnp.zeros((2 * GLA_RANK, 2 * GLA_K_W), F32)
    w_bd = w_bd.at[:GLA_RANK, :GLA_K_W].set(w_decay[0, 0]).at[GLA_RANK:, GLA_K_W:].set(w_decay[0, 1])
    w_bd_hi = w_bd.astype(BF16)
    w_bd_lo = (w_bd - w_bd_hi.astype(F32)).astype(BF16)
    w_dec = jnp.concatenate([w_bd_hi, w_bd_hi, w_bd_lo, jnp.zeros_like(w_bd_hi)], axis=0)
    b_dec = b_decay[0].reshape(1, 2 * GLA_K_W)
    gqk = jnp.concatenate([jnp.tile(g_q[0], ATTN_HEADS), jnp.tile(g_k[0], ATTN_KV_HEADS)]).reshape(1, -1)
    lane = np.arange(LANES)
    seg = jnp.asarray(np.tile((lane[:, None] // HEAD_DIM == lane[None, :] // HEAD_DIM) / HEAD_DIM, (2, 1)), dtype=BF16)
    cos_t, sin_t = _rope_tables(t)
    ones_t, zeros_t = jnp.ones((nb * tc, LANES), F32), jnp.zeros((nb * tc, LANES), F32)
    mf, mb = _gla_constants()

    proj = functools.partial(_in_proj, g=g_norm[0, 1], w_r=w_r, gqk=gqk, seg=seg, w_dec=w_dec,
                             b_dec=b_dec, b_gate=b_gate[0].reshape(1, 2 * d))

    hc = _half_ffn(ctx.reshape(1, nb * tc, d), m_ctx[:, 0:3], g_norm[0, 0], wa[0], wb[0], wd[0])
    pc = proj(hc, m_ctx[:, 3:5], cos_t=ones_t, sin_t=zeros_t)
    _, k_c, vt_c, _, gk_c, _, gvt_c, laf_c, lab_c, _, _ = pc

    h1 = _half_ffn(x, m_lat[:, 0:3], g_norm[0, 0], wa[0], wb[0], wd[0])
    qt, k, vt, gq, gk, gv, gvt, laf, lab, gs, gt = proj(h1, m_lat[:, 3:5], cos_t=cos_t, sin_t=sin_t)
    k_all = jnp.concatenate(
        [k, k_c.reshape(ATTN_KV_HEADS, nb, tc, LANES).transpose(1, 0, 2, 3)], axis=2)
    vt_all = jnp.concatenate(
        [vt, vt_c.reshape(ATTN_KV_HEADS, HEAD_DIM, nb, tc).transpose(2, 0, 1, 3)], axis=3)
    bound = jnp.ceil(1.01 * HEAD_DIM ** 0.5 * float(np.log2(np.e)) * jnp.max(jnp.abs(g_q[0])) * jnp.max(jnp.abs(g_k[0])))
    bounded = bound <= ATTN_MAX_SHIFT
    par = jnp.stack([bounded.astype(F32), jnp.where(bounded, bound, 0.0)])
    attn_o = _attention(par, qt, k_all, vt_all, tc)
    gla_o = _gla(gq, gk, gv, gvt, laf, lab, gk_c, gvt_c, laf_c, lab_c, mf, mb)
    return _merge_ffn(h1, attn_o, gla_o, gs, gt, m_lat[:, 5:6], g_gla[0],
                      w_branch[0, 0].astype(BF16), w_branch[0, 1].astype(BF16), w_out[0].astype(BF16),
                      m_lat[:, 6:9], g_norm[0, 2], wa[1], wb[1], wd[1], g_final)
```

```python
import functools

import numpy as np
import jax
import jax.numpy as jnp
from jax import lax
from jax.experimental import pallas as pl
from jax.experimental.pallas import tpu as pltpu

F32 = jnp.float32
BF16 = jnp.bfloat16

EPS = 1e-6
GRID_W = 64
N_MOD = 9
ATTN_HEADS = 8
ATTN_KV_HEADS = 2
HEAD_DIM = 64
ROPE_AXIS_DIM = HEAD_DIM // 2
ROPE_THETA = 10000.0
GLA_HEADS = 4
GLA_DK = 64
GLA_DV = 128
GLA_RANK = 16
GLA_GATE_NORM = 16.0
ATTN_Q_W = ATTN_HEADS * HEAD_DIM
ATTN_KV_W = ATTN_KV_HEADS * HEAD_DIM
GLA_K_W = GLA_HEADS * GLA_DK
GLA_V_W = GLA_HEADS * GLA_DV
Q_GROUP = ATTN_HEADS // ATTN_KV_HEADS

LANES = 128
SUBLANES = 8
V7X_VMEM_BYTES = 64 * 1024 * 1024
VMEM_LIMIT = V7X_VMEM_BYTES * 7 // 8
TOKEN_TILE = 512
PROJ_TOKEN_TILE = 1024
MOD_COL_TILE = 1024
ATTN_Q_TILE = 512
GLA_TILE = 4096
GLA_CHUNK = 128


def _dot(a, b):
    return jnp.dot(a, b, preferred_element_type=F32)


def _dot_nt(a, b):
    return lax.dot_general(a, b, (((1,), (1,)), ((), ())), preferred_element_type=F32)


def _sigmoid(x):
    return 0.5 + 0.5 * jnp.tanh(0.5 * x)


def _split2(x):
    hi = x.astype(BF16)
    lo = (x - hi.astype(F32)).astype(BF16)
    return hi, lo


def _rms(x):
    return x * lax.rsqrt(jnp.mean(x * x, axis=-1, keepdims=True) + EPS)


def _const_spec(shape):
    nd = len(shape)
    return pl.BlockSpec(shape, lambda *_: (0,) * nd, pipeline_mode=pl.Buffered(1))


def _mod_kernel(c_ref, w_ref, b_ref, o_ref):
    c = c_ref[...]
    s_hi, s_lo = _split2(c * _sigmoid(c))
    w_hi, w_lo = _split2(w_ref[...])
    o_ref[...] = _dot(s_hi, w_hi) + _dot(s_hi, w_lo) + _dot(s_lo, w_hi) + b_ref[...]


def _modulation(c_rows, w_mod, b_mod):
    rows, d = c_rows.shape
    n = w_mod.shape[1]
    tn = MOD_COL_TILE
    return pl.pallas_call(
        _mod_kernel,
        grid=(n // tn,),
        in_specs=[pl.BlockSpec((rows, d), lambda j: (0, 0)),
                  pl.BlockSpec((d, tn), lambda j: (0, j)),
                  pl.BlockSpec((1, tn), lambda j: (0, j))],
        out_specs=pl.BlockSpec((rows, tn), lambda j: (0, j)),
        out_shape=jax.ShapeDtypeStruct((rows, n), F32),
        compiler_params=pltpu.CompilerParams(dimension_semantics=("arbitrary",),
                                             vmem_limit_bytes=VMEM_LIMIT),
        name="modulation",
    )(c_rows, w_mod, b_mod.reshape(1, n))


def _ffn_math(x, mod_ref, g_ref, wa_ref, wb_ref, wd_ref):
    shift, scale, gate = mod_ref[0, 0:1, :], mod_ref[0, 1:2, :], mod_ref[0, 2:3, :]
    n = ((_rms(x) * g_ref[...]) * (1.0 + scale) + shift).astype(BF16)
    a = _dot(n, wa_ref[...])
    b = _dot(n, wb_ref[...])
    act = ((a * _sigmoid(a)) * b).astype(BF16)
    return x + 0.5 * gate * _dot(act, wd_ref[...])


def _ffn_kernel(h_ref, mod_ref, g_ref, wa_ref, wb_ref, wd_ref, o_ref):
    o_ref[0] = _ffn_math(h_ref[0], mod_ref, g_ref, wa_ref, wb_ref, wd_ref)


def _merge_ffn_kernel(h_ref, ao_ref, go_ref, gs_ref, gt_ref, mg_ref, gg_ref, wb0_ref, wb1_ref, wo_ref,
                      mod_ref, g_ref, wa_ref, wb_ref, wd_ref, gf_ref, o_ref):
    d = h_ref.shape[2]
    go = go_ref[0]
    normed = [_rms(go[:, h * GLA_DV:(h + 1) * GLA_DV]) * gg_ref[...] for h in range(GLA_HEADS)]
    gn = (jnp.concatenate(normed, axis=-1) * gs_ref[0].astype(F32)).astype(BF16)
    y_attn = _dot(ao_ref[0], wb0_ref[...])
    y_gla = _dot(gn, wb1_ref[...])
    gt = gt_ref[0].astype(F32)
    z = (gt[:, :d] * y_attn + gt[:, d:] * y_gla).astype(BF16)
    h2 = h_ref[0] + mg_ref[0] * _dot(z, wo_ref[...])
    out = _ffn_math(h2, mod_ref, g_ref, wa_ref, wb_ref, wd_ref)
    o_ref[0] = _rms(out) * gf_ref[...]


def _ffn_weight_specs(d, f):
    return [_const_spec((1, d)), _const_spec((d, f)), _const_spec((d, f)), _const_spec((f, d))]


def _half_ffn(h, mod3, g, wa, wb, wd, *, tm=TOKEN_TILE):
    nb, t, d = h.shape
    f = wa.shape[1]
    tm = min(tm, t)
    return pl.pallas_call(
        _ffn_kernel,
        grid=(nb, t // tm),
        in_specs=[pl.BlockSpec((1, tm, d), lambda b, i: (b, i, 0)),
                  pl.BlockSpec((1, 3, d), lambda b, i: (b, 0, 0))] + _ffn_weight_specs(d, f),
        out_specs=pl.BlockSpec((1, tm, d), lambda b, i: (b, i, 0)),
        out_shape=jax.ShapeDtypeStruct((nb, t, d), F32),
        compiler_params=pltpu.CompilerParams(dimension_semantics=("arbitrary", "arbitrary"),
                                             vmem_limit_bytes=VMEM_LIMIT),
        name="half_ffn",
    )(h, mod3, g.reshape(1, d), wa, wb, wd)


def _merge_ffn(h, attn_o, gla_o, gs, gt, m_gate, g_gla, wb0, wb1, wo, mod3, g, wa, wb, wd, g_final,
               *, tm=TOKEN_TILE):
    nb, t, d = h.shape
    f = wa.shape[1]
    tm = min(tm, t)
    tok = lambda w: pl.BlockSpec((1, tm, w), lambda b, i: (b, i, 0))
    return pl.pallas_call(
        _merge_ffn_kernel,
        grid=(nb, t // tm),
        in_specs=[tok(d), tok(ATTN_Q_W), tok(GLA_V_W), tok(GLA_V_W), tok(2 * d),
                  pl.BlockSpec((1, 1, d), lambda b, i: (b, 0, 0)),
                  _const_spec((1, GLA_DV)), _const_spec((ATTN_Q_W, d)), _const_spec((GLA_V_W, d)),
                  _const_spec((d, d)),
                  pl.BlockSpec((1, 3, d), lambda b, i: (b, 0, 0))] + _ffn_weight_specs(d, f)
                 + [_const_spec((1, d))],
        out_specs=tok(d),
        out_shape=jax.ShapeDtypeStruct((nb, t, d), F32),
        compiler_params=pltpu.CompilerParams(dimension_semantics=("arbitrary", "arbitrary"),
                                             vmem_limit_bytes=VMEM_LIMIT),
        name="merge_ffn_final",
    )(h, attn_o, gla_o, gs, gt, m_gate, g_gla.reshape(1, GLA_DV), wb0, wb1, wo,
      mod3, g.reshape(1, d), wa, wb, wd, g_final.reshape(1, d))


_C_Q = 0
_C_K = _C_Q + ATTN_Q_W
_C_V = _C_K + ATTN_KV_W
_C_GQ = _C_V + ATTN_KV_W
_C_GK = _C_GQ + GLA_K_W
_C_GV = _C_GK + GLA_K_W
_C_GS = _C_GV + GLA_V_W
_C_GT = _C_GS + GLA_V_W


def _proj_kernel(h_ref, mod_ref, g_ref, w_ref, gqk_ref, seg_ref, cos_ref, sin_ref, wdec_ref,
                 bdec_ref, bgate_ref,
                 qt_ref, k_ref, vt_ref, gq_ref, gk_ref, gv_ref, gvt_ref, laf_ref, lab_ref, gs_ref, gt_ref):
    d = h_ref.shape[2]
    c_low = _C_GT + 2 * d
    x = h_ref[0]
    shift, scale = mod_ref[0, 0:1, :], mod_ref[0, 1:2, :]
    n = ((_rms(x) * g_ref[...]) * (1.0 + scale) + shift).astype(BF16)
    tm = x.shape[0]

    qkv = _dot(n, w_ref[:, _C_Q:_C_GQ])
    qk = qkv[:, :_C_V]
    low = _dot(n, w_ref[:, c_low:c_low + LANES])
    gt = _dot(n, w_ref[:, _C_GT:c_low]) + bgate_ref[...]

    seg = seg_ref[...]
    ms = []
    for j in range((ATTN_Q_W + ATTN_KV_W) // LANES):
        sq = qk[:, j * LANES:(j + 1) * LANES]
        hi, lo = _split2(sq * sq)
        ms.append(_dot(jnp.concatenate([hi, lo], axis=-1), seg))
    ms = jnp.concatenate(ms, axis=-1)
    gs = _dot(n, w_ref[:, _C_GS:_C_GT])

    l_hi = low.astype(BF16).astype(F32)
    l_lo = low - l_hi
    packed = l_hi + pltpu.roll(l_lo, 2 * GLA_RANK, 1) + pltpu.roll(l_hi, 4 * GLA_RANK, 1)
    z = _dot(packed.astype(BF16), wdec_ref[...]) + bdec_ref[...]
    gv = _dot(n, w_ref[:, _C_GV:_C_GS])
    vt = qkv[:, _C_V:_C_GQ].T
    gqk = _dot(n, w_ref[:, _C_GQ:_C_GV])

    qk = qk * lax.rsqrt(ms + EPS) * gqk_ref[...]
    lane = lax.broadcasted_iota(jnp.int32, (tm, LANES), 1)
    first = (lane % HEAD_DIM) < (HEAD_DIM // 2)
    cos, sin = cos_ref[...], sin_ref[...]
    rot = []
    for j in range((ATTN_Q_W + ATTN_KV_W) // LANES):
        xs = qk[:, j * LANES:(j + 1) * LANES]
        other = jnp.where(first, pltpu.roll(xs, LANES - HEAD_DIM // 2, 1), pltpu.roll(xs, HEAD_DIM // 2, 1))
        rot.append(xs * cos + other * sin)
    q_scale = HEAD_DIM ** -0.5 * float(np.log2(np.e))
    for j in range(ATTN_Q_W // LANES):
        qt_ref[0, j * LANES:(j + 1) * LANES, :] = (rot[j] * q_scale).T.astype(BF16)
    k_rot = rot[ATTN_Q_W // LANES]
    ones_lane = jnp.where(lane == HEAD_DIM, 1.0, 0.0)
    for g in range(ATTN_KV_HEADS):
        k_head = k_rot if g == 0 else pltpu.roll(k_rot, LANES - g * HEAD_DIM, 1)
        k_ref[0, g] = jnp.where(lane < HEAD_DIM, k_head, ones_lane).astype(BF16)

    for g in range(ATTN_KV_HEADS):
        vt_ref[0, g] = vt[g * HEAD_DIM:(g + 1) * HEAD_DIM].astype(BF16)

    gq_ref[0] = (gqk[:, :GLA_K_W] * (GLA_DK ** -0.5)).astype(BF16)
    gk_ref[0] = gqk[:, GLA_K_W:].astype(BF16)
    gv_ref[0] = gv.astype(BF16)
    gvt_ref[0] = gv.T.astype(BF16)

    la = (jnp.minimum(z, 0.0) - jnp.log(1.0 + jnp.exp(-jnp.abs(z)))) * (1.0 / GLA_GATE_NORM)
    laf_ref[0] = la[:, :GLA_K_W]
    lab_ref[0] = la[:, GLA_K_W:]

    gs_ref[0] = (gs * _sigmoid(gs)).astype(BF16)
    gt_ref[0] = _sigmoid(gt).astype(BF16)


def _in_proj(h, mod2, g, w_r, gqk, seg, cos_t, sin_t, w_dec, b_dec, b_gate, *, tm=PROJ_TOKEN_TILE):
    nb, t, d = h.shape
    tm = min(tm, t)
    wp = w_r.shape[1]
    tok = lambda w: pl.BlockSpec((1, tm, w), lambda b, i: (b, i, 0))
    out_shape = [
        jax.ShapeDtypeStruct((nb, ATTN_Q_W, t), BF16),
        jax.ShapeDtypeStruct((nb, ATTN_KV_HEADS, t, LANES), BF16),
        jax.ShapeDtypeStruct((nb, ATTN_KV_HEADS, HEAD_DIM, t), BF16),
        jax.ShapeDtypeStruct((nb, t, GLA_K_W), BF16),
        jax.ShapeDtypeStruct((nb, t, GLA_K_W), BF16),
        jax.ShapeDtypeStruct((nb, t, GLA_V_W), BF16),
        jax.ShapeDtypeStruct((nb, GLA_V_W, t), BF16),
        jax.ShapeDtypeStruct((nb, t, GLA_K_W), F32),
        jax.ShapeDtypeStruct((nb, t, GLA_K_W), F32),
        jax.ShapeDtypeStruct((nb, t, GLA_V_W), BF16),
        jax.ShapeDtypeStruct((nb, t, 2 * d), BF16),
    ]
    out_specs = [
        pl.BlockSpec((1, ATTN_Q_W, tm), lambda b, i: (b, 0, i)),
        pl.BlockSpec((1, ATTN_KV_HEADS, tm, LANES), lambda b, i: (b, 0, i, 0)),
        pl.BlockSpec((1, ATTN_KV_HEADS, HEAD_DIM, tm), lambda b, i: (b, 0, 0, i)),
        tok(GLA_K_W), tok(GLA_K_W), tok(GLA_V_W),
        pl.BlockSpec((1, GLA_V_W, tm), lambda b, i: (b, 0, i)),
        tok(GLA_K_W), tok(GLA_K_W), tok(GLA_V_W), tok(2 * d),
    ]
    in_specs = [
        pl.BlockSpec((1, tm, d), lambda b, i: (b, i, 0)),
        pl.BlockSpec((1, 2, d), lambda b, i: (b, 0, 0)),
        _const_spec((1, d)), _const_spec((d, wp)), _const_spec((1, ATTN_Q_W + ATTN_KV_W)),
        _const_spec((2 * LANES, LANES)),
        pl.BlockSpec((tm, LANES), lambda b, i: (i, 0)),
        pl.BlockSpec((tm, LANES), lambda b, i: (i, 0)),
        _const_spec((LANES, 2 * GLA_K_W)),
        _const_spec((1, 2 * GLA_K_W)), _const_spec((1, 2 * d)),
    ]
    return pl.pallas_call(
        _proj_kernel,
        grid=(nb, t // tm),
        in_specs=in_specs,
        out_specs=out_specs,
        out_shape=out_shape,
        compiler_params=pltpu.CompilerParams(dimension_semantics=("arbitrary", "arbitrary"),
                                             vmem_limit_bytes=VMEM_LIMIT),
        name="in_proj",
    )(h, mod2, g.reshape(1, d), w_r, gqk, seg, cos_t, sin_t, w_dec, b_dec, b_gate)


ATTN_GROUPS = 2
ATTN_ROW_BLOCK = 256
ATTN_FAST_ROW_BLOCK = 256
ATTN_KC = 1024
ATTN_UNROLL = 3
ATTN_BLOCKS_PER_ITER = 33
ATTN_MAX_SHIFT = 50.0


def _attn_plan(t, tc):
    if tc % (2 * LANES) == 0:
        kc = max(c for c in range(LANES, ATTN_KC + 1, LANES) if t % c == 0)
        edge = tc // 2
        chunks = [(t, edge)] + [(i * kc, kc) for i in range(t // kc)] + [(t + edge, edge)]
        return chunks, (1, len(chunks) - 1)
    s_len = t + tc
    kc = max(c for c in range(LANES, ATTN_KC + 1, LANES) if s_len % c == 0)
    chunks = [(i * kc, kc) for i in range(s_len // kc)]
    return chunks, (0, len(chunks))


def _sublane_partial_sums(p):
    return p.astype(F32).reshape(p.shape[0] // SUBLANES, SUBLANES, p.shape[1]).sum(axis=0)


def _attn_kernel(par_ref, qt_ref, k_ref, vt_ref, o_ref, qs_ref, *group_refs, chunks, run):
    tq = qt_ref.shape[2]
    s_len = k_ref.shape[2]
    gw = Q_GROUP * tq // ATTN_GROUPS
    per = len(group_refs) // ATTN_GROUPS
    m_refs, cmax_refs, l_refs, acc_refs, s_refs = (
        [group_refs[g * per + j] for g in range(ATTN_GROUPS)] for j in range(per))
    for h in range(Q_GROUP):
        qs_ref[0:HEAD_DIM, h * tq:(h + 1) * tq] = qt_ref[0, h * HEAD_DIM:(h + 1) * HEAD_DIM, :]
    extra = lax.broadcasted_iota(jnp.int32, (LANES - HEAD_DIM, Q_GROUP * tq), 0)
    qs_ref[HEAD_DIM:LANES, :] = jnp.where(extra == 0, -par_ref[1], 0.0).astype(BF16)
    for g in range(ATTN_GROUPS):
        acc_refs[g][...] = jnp.zeros(acc_refs[g].shape, F32)
        l_refs[g][...] = jnp.zeros(l_refs[g].shape, F32)

    def rows(off, j, rb):
        start = off + j * rb
        return pl.ds(start if isinstance(start, int) else pl.multiple_of(start, LANES), rb)

    def bounded_scores():
        rb = ATTN_FAST_ROW_BLOCK if s_len % ATTN_FAST_ROW_BLOCK == 0 else LANES
        n_blocks = s_len // rb
        per_iter = max(d for d in range(1, ATTN_BLOCKS_PER_ITER + 1) if n_blocks % d == 0)

        def body(it, carry):
            sums = [None] * ATTN_GROUPS
            dens = [None] * ATTN_GROUPS
            pending = None

            def finish(unit):
                blk, g, p = unit
                part = _dot(vt_ref[0, 0, :, blk], p)
                sums[g] = part if sums[g] is None else sums[g] + part
                den = _sublane_partial_sums(p)
                dens[g] = den if dens[g] is None else dens[g] + den

            for j in range(per_iter):
                blk = rows(it * (per_iter * rb), j, rb)
                for g in range(ATTN_GROUPS):
                    s = _dot(k_ref[0, 0, blk, :], qs_ref[:, g * gw:(g + 1) * gw])
                    if pending is not None:
                        finish(pending)
                    pending = (blk, g, jnp.exp2(s).astype(BF16))
            finish(pending)
            for g in range(ATTN_GROUPS):
                acc_refs[g][...] += sums[g]
                l_refs[g][...] += jnp.sum(dens[g], axis=0, keepdims=True)
            return carry

        lax.fori_loop(0, n_blocks // per_iter, body, 0)

    def online_softmax():
        for g in range(ATTN_GROUPS):
            m_refs[g][...] = jnp.full(m_refs[g].shape, -jnp.inf, F32)

        def stage(score, apply):
            n_s = n_a = 0
            if score is not None:
                off_s, size_s, gs = score
                rb_s = min(ATTN_ROW_BLOCK, size_s)
                n_s = size_s // rb_s
            if apply is not None:
                off_a, size_a, ga = apply
                rb_a = min(ATTN_ROW_BLOCK, size_a)
                n_a = size_a // rb_a
                m_old = m_refs[ga][...]
                m_new = jnp.maximum(m_old, cmax_refs[ga][...])
                m_refs[ga][...] = m_new
            cmax = pv_sum = den = None
            for j in range(max(n_s, n_a)):
                if j < n_a:
                    p = jnp.exp2(s_refs[ga][j * rb_a:(j + 1) * rb_a, :] - m_new).astype(BF16)
                if j < n_s:
                    s = _dot(k_ref[0, 0, rows(off_s, j, rb_s), :], qs_ref[:, gs * gw:(gs + 1) * gw])
                    s_refs[gs][j * rb_s:(j + 1) * rb_s, :] = s
                    bmax = jnp.max(s, axis=0, keepdims=True)
                    cmax = bmax if cmax is None else jnp.maximum(cmax, bmax)
                if j < n_a:
                    part = _dot(vt_ref[0, 0, :, rows(off_a, j, rb_a)], p)
                    pv_sum = part if pv_sum is None else pv_sum + part
                    dj = _sublane_partial_sums(p)
                    den = dj if den is None else den + dj
            if score is not None:
                cmax_refs[gs][...] = cmax
            if apply is not None:
                alpha = jnp.exp2(m_old - m_new)
                acc_refs[ga][...] = alpha * acc_refs[ga][...] + pv_sum
                l_refs[ga][...] = alpha * l_refs[ga][...] + jnp.sum(den, axis=0, keepdims=True)

        last = ATTN_GROUPS - 1

        def chunk_stages(cur, prev):
            stage(cur + (0,), None if prev is None else prev + (last,))
            for g in range(1, ATTN_GROUPS):
                stage(cur + (g,), cur + (g - 1,))

        a, b = run
        trips = b - a - 1
        unroll = ATTN_UNROLL if trips >= ATTN_UNROLL else 1
        first_loop = a + 1 + trips % unroll
        for i in range(first_loop):
            chunk_stages(chunks[i], chunks[i - 1] if i else None)
        if b > first_loop:
            off0, kc = chunks[first_loop]

            def body(i, carry):
                for u in range(unroll):
                    off = off0 + (i * unroll + u) * kc
                    chunk_stages((off, kc), (off - kc, kc))
                return carry

            lax.fori_loop(0, (b - first_loop) // unroll, body, 0)
        for i in range(b, len(chunks)):
            chunk_stages(chunks[i], chunks[i - 1])
        stage(None, chunks[-1] + (last,))

    pl.when(par_ref[0] > 0.0)(bounded_scores)
    pl.when(par_ref[0] <= 0.0)(online_softmax)

    acc = jnp.concatenate([r[...] for r in acc_refs], axis=1)
    ot = acc / jnp.concatenate([r[...] for r in l_refs], axis=1)
    for j in range(Q_GROUP // 2):
        pair = jnp.concatenate([ot[:, (2 * j) * tq:(2 * j + 1) * tq], ot[:, (2 * j + 1) * tq:(2 * j + 2) * tq]],
                               axis=0)
        o_ref[0, :, j * LANES:(j + 1) * LANES] = pair.T.astype(o_ref.dtype)


def _attention(par, qt, k, vt, t_ctx, *, tq=ATTN_Q_TILE):
    nb, _, t = qt.shape
    s_len = k.shape[2]
    tq = min(tq, t)
    chunks, run = _attn_plan(t, t_ctx)
    kc = max(size for _, size in chunks)
    gw = Q_GROUP * tq // ATTN_GROUPS
    return pl.pallas_call(
        functools.partial(_attn_kernel, chunks=tuple(chunks), run=run),
        grid=(nb, ATTN_KV_HEADS, t // tq),
        in_specs=[
            pl.BlockSpec(memory_space=pltpu.SMEM),
            pl.BlockSpec((1, Q_GROUP * HEAD_DIM, tq), lambda b, g, i: (b, g, i)),
            pl.BlockSpec((1, 1, s_len, LANES), lambda b, g, i: (b, g, 0, 0)),
            pl.BlockSpec((1, 1, HEAD_DIM, s_len), lambda b, g, i: (b, g, 0, 0)),
        ],
        out_specs=pl.BlockSpec((1, tq, Q_GROUP * HEAD_DIM), lambda b, g, i: (b, i, g)),
        out_shape=jax.ShapeDtypeStruct((nb, t, ATTN_Q_W), BF16),
        scratch_shapes=[pltpu.VMEM((LANES, Q_GROUP * tq), BF16)] + ATTN_GROUPS * [
            pltpu.VMEM((1, gw), F32),
            pltpu.VMEM((1, gw), F32),
            pltpu.VMEM((1, gw), F32),
            pltpu.VMEM((HEAD_DIM, gw), F32),
            pltpu.VMEM((kc, gw), F32)],
        compiler_params=pltpu.CompilerParams(dimension_semantics=("arbitrary", "arbitrary", "arbitrary"),
                                             vmem_limit_bytes=VMEM_LIMIT),
        name="flash_attention",
    )(par, qt, k, vt)


def _gla_constants():
    c = GLA_CHUNK
    i = np.arange(c)[:, None]
    t = np.arange(c)[None, :]
    fwd = [t > i, t <= i]
    bwd = [t < i, t >= i]
    s = c // 2
    while s >= 2:
        mid = (i // (2 * s)) * (2 * s) + s
        second = (i % (2 * s)) >= s
        fwd.append(np.where(second, (t >= mid) & (t <= i), (t > i) & (t < mid)))
        bwd.append(np.where(second, (t >= mid) & (t < i), (t >= i) & (t < mid)))
        s //= 2
    to = lambda blocks: jnp.asarray(np.concatenate(blocks, axis=0).astype(np.float32), dtype=BF16)
    return to(fwd), to(bwd)


_GLA_LEVELS = int(np.log2(GLA_CHUNK))


def _cum(mat, la):
    hi, lo = _split2(la)
    r = _dot(mat, jnp.concatenate([hi, lo], axis=-1))
    return r[:, :LANES] + r[:, LANES:]


def _gla_kernel(q_ref, k_ref, v_ref, vt_ref, laf_ref, lab_ref, kc_ref, vtc_ref, lafc_ref, labc_ref,
                mf_ref, mb_ref, o_ref, st_ref, dec_ref, *, cpt):
    c = GLA_CHUNK
    n_lat = k_ref.shape[1] // c
    n_ctx = kc_ref.shape[1] // c
    n_all = n_ctx + n_lat
    tile = pl.program_id(2)
    lane = lax.broadcasted_iota(jnp.int32, (c, LANES), 1)
    head_masks = [lane < GLA_DK, lane >= GLA_DK]
    lane2 = lax.broadcasted_iota(jnp.int32, (c, 2 * LANES), 1) % LANES
    pair_masks = [lane2 < GLA_DK, lane2 >= GLA_DK]

    def increments(k_r, vt_r, laf_r, lab_r, n0, slot0, count):
        offs = [pl.multiple_of((n0 + i) * c, c) for i in range(count)]
        rfs = [_cum(mf_ref[0:2 * c, :], laf_r[0, pl.ds(off, c), :]) for off in offs]
        rbs = [_cum(mb_ref[0:2 * c, :], lab_r[0, pl.ds(off, c), :]) for off in offs]
        for i, (off, rf, rb) in enumerate(zip(offs, rfs, rbs)):
            k = k_r[0, pl.ds(off, c), :].astype(F32)
            kfb = jnp.concatenate([k * jnp.exp(rf[0:c]),
                                   k * jnp.exp(rb[0:c])], axis=-1)
            kk = jnp.concatenate([jnp.where(pair_masks[0], kfb, 0.0), jnp.where(pair_masks[1], kfb, 0.0)],
                                 axis=0).astype(BF16)
            vt2 = jnp.concatenate([vt_r[0, 0:GLA_DV, pl.ds(off, c)], vt_r[0, GLA_DV:2 * GLA_DV, pl.ds(off, c)]],
                                  axis=-1)
            st_ref[slot0 + i] = _dot(vt2, kk)
            dec_ref[slot0 + i, 0:1, :] = jnp.exp(rf[2 * c - 1:2 * c, :])
            dec_ref[slot0 + i, 1:2, :] = jnp.exp(rb[c:c + 1, :])

    def group_size(n):
        return max(g for g in (4, 2, 1) if n % g == 0)

    @pl.when(tile == 0)
    def _():
        gc, gl = group_size(n_ctx), group_size(n_lat)

        def ctx_body(i, carry):
            increments(kc_ref, vtc_ref, lafc_ref, labc_ref, i * gc, i * gc, gc)
            return carry
        lax.fori_loop(0, n_ctx // gc, ctx_body, 0)

        def lat_body(i, carry):
            increments(k_ref, vt_ref, laf_ref, lab_ref, i * gl, n_ctx + i * gl, gl)
            return carry
        lax.fori_loop(0, n_lat // gl, lat_body, 0)

        def fwd_body(s, st):
            inc = st_ref[s, :, 0:LANES]
            st_ref[s, :, 0:LANES] = st
            return st * dec_ref[s, 0:1, :] + inc
        lax.fori_loop(0, n_all, fwd_body, jnp.zeros((GLA_DV, LANES), F32))

        def bwd_body(j, st, base, count):
            s = base + count - 1 - j
            inc = st_ref[s, :, LANES:2 * LANES]
            st_ref[s, :, LANES:2 * LANES] = st
            return st * dec_ref[s, 1:2, :] + inc
        st = lax.fori_loop(0, n_ctx, functools.partial(bwd_body, base=0, count=n_ctx),
                           jnp.zeros((GLA_DV, LANES), F32))
        lax.fori_loop(0, n_lat, functools.partial(bwd_body, base=n_ctx, count=n_lat), st)

    xor2 = lax.broadcasted_iota(jnp.int32, (c, 2 * c), 0) ^ (lax.broadcasted_iota(jnp.int32, (c, 2 * c), 1) & (c - 1))
    row_l = lax.broadcasted_iota(jnp.int32, (c, LANES), 0)
    lane2v = lax.broadcasted_iota(jnp.int32, (c, 2 * GLA_DV), 1)
    lane2s = lax.broadcasted_iota(jnp.int32, (GLA_DV, 2 * LANES), 1) % LANES

    def stack_heads(x):
        return jnp.concatenate([jnp.where(head_masks[0], x, 0.0), jnp.where(head_masks[1], x, 0.0)], axis=0)

    go = group_size(cpt)

    def out_body(i, carry):
        idx = range(go)
        ns = [tile * cpt + i * go + u for u in idx]
        offs = [pl.multiple_of(n * c, c) for n in ns]
        locs = [pl.multiple_of((i * go + u) * c, c) for u in idx]
        lafs = [laf_ref[0, pl.ds(off, c), :] for off in offs]
        labs = [lab_ref[0, pl.ds(off, c), :] for off in offs]
        rfs = [_cum(mf_ref[c:, :], la) for la in lafs]
        rbs = [_cum(mb_ref[c:, :], la) for la in labs]
        qs = [q_ref[0, pl.ds(loc, c), :].astype(F32) for loc in locs]
        ks = [k_ref[0, pl.ds(off, c), :].astype(F32) for off in offs]
        a = [2.0 * _dot_nt(qs[u].astype(BF16), stack_heads(ks[u]).astype(BF16)) for u in idx]
        for lvl in range(_GLA_LEVELS):
            sh = _GLA_LEVELS - 1 - lvl
            second = ((row_l >> sh) & 1) == 1
            for u in idx:
                if sh > 0:
                    ef = jnp.exp(rfs[u][(1 + lvl) * c:(2 + lvl) * c])
                    eb = jnp.exp(rbs[u][(1 + lvl) * c:(2 + lvl) * c])
                    ql = (qs[u] * jnp.where(second, ef, eb)).astype(BF16)
                    kl = stack_heads(ks[u] * jnp.where(second, eb, ef)).astype(BF16)
                else:
                    ql = (qs[u] * jnp.exp(jnp.where(second, lafs[u], labs[u]))).astype(BF16)
                    kl = stack_heads(ks[u]).astype(BF16)
                a[u] = jnp.where((xor2 >> sh) == 1, _dot_nt(ql, kl), a[u])
        for u in idx:
            v = v_ref[0, pl.ds(locs[u], c), :]
            v_bd = jnp.concatenate([jnp.where(lane2v < GLA_DV, v, jnp.zeros_like(v)),
                                    jnp.where(lane2v >= GLA_DV, v, jnp.zeros_like(v))], axis=0)
            q_inter = jnp.concatenate([qs[u] * jnp.exp(rfs[u][0:c]), qs[u] * jnp.exp(rbs[u][0:c])],
                                      axis=-1).astype(BF16)
            states = st_ref[n_ctx + ns[u]]
            st2 = jnp.concatenate([jnp.where(lane2s < GLA_DK, states, 0.0),
                                   jnp.where(lane2s >= GLA_DK, states, 0.0)], axis=0).astype(BF16)
            o_ref[0, pl.ds(locs[u], c), :] = _dot(a[u].astype(BF16), v_bd) + _dot_nt(q_inter, st2)
        return carry

    lax.fori_loop(0, cpt // go, out_body, 0)


def _gla(gq, gk, gv, gvt, laf, lab, gk_c, gvt_c, laf_c, lab_c, mf, mb, *, tile=GLA_TILE):
    nb, t, _ = gq.shape
    tc = gk_c.shape[1] // nb
    tile = min(tile, t)
    c = GLA_CHUNK
    n_all = (t + tc) // c
    pair_k = 2 * GLA_DK
    pair_v = 2 * GLA_DV
    return pl.pallas_call(
        functools.partial(_gla_kernel, cpt=tile // c),
        grid=(nb, GLA_HEADS // 2, t // tile),
        in_specs=[
            pl.BlockSpec((1, tile, pair_k), lambda b, p, i: (b, i, p)),
            pl.BlockSpec((1, t, pair_k), lambda b, p, i: (b, 0, p)),
            pl.BlockSpec((1, tile, pair_v), lambda b, p, i: (b, i, p)),
            pl.BlockSpec((1, pair_v, t), lambda b, p, i: (b, p, 0)),
            pl.BlockSpec((1, t, pair_k), lambda b, p, i: (b, 0, p)),
            pl.BlockSpec((1, t, pair_k), lambda b, p, i: (b, 0, p)),
            pl.BlockSpec((1, tc, pair_k), lambda b, p, i: (0, b, p)),
            pl.BlockSpec((1, pair_v, tc), lambda b, p, i: (0, p, b)),
            pl.BlockSpec((1, tc, pair_k), lambda b, p, i: (0, b, p)),
            pl.BlockSpec((1, tc, pair_k), lambda b, p, i: (0, b, p)),
            _const_spec(tuple(mf.shape)), _const_spec(tuple(mb.shape)),
        ],
        out_specs=pl.BlockSpec((1, tile, pair_v), lambda b, p, i: (b, i, p)),
        out_shape=jax.ShapeDtypeStruct((nb, t, GLA_V_W), F32),
        scratch_shapes=[pltpu.VMEM((n_all, GLA_DV, 2 * LANES), F32),
                        pltpu.VMEM((n_all, SUBLANES, LANES), F32)],
        compiler_params=pltpu.CompilerParams(dimension_semantics=("arbitrary", "arbitrary", "arbitrary"),
                                             vmem_limit_bytes=VMEM_LIMIT),
        name="gla",
    )(gq, gk, gv, gvt, laf, lab, gk_c, gvt_c, laf_c, lab_c, mf, mb)


def _rope_tables(t):
    rows = t // GRID_W
    row = jnp.repeat(jnp.arange(rows, dtype=F32), GRID_W)
    col = jnp.tile(jnp.arange(GRID_W, dtype=F32), rows)
    freqs = ROPE_THETA ** (-jnp.arange(0, ROPE_AXIS_DIM, 2, dtype=F32) / ROPE_AXIS_DIM)
    ang = jnp.concatenate([row[:, None] * freqs, col[:, None] * freqs], axis=-1)
    cos, sin = jnp.cos(ang), jnp.sin(ang)
    reps = LANES // HEAD_DIM
    return (jnp.tile(jnp.concatenate([cos, cos], axis=-1), (1, reps)),
            jnp.tile(jnp.concatenate([-sin, sin], axis=-1), (1, reps)))


def kernel(x, c, ctx, c_ctx, w_mod, b_mod, g_norm, w_ffn_up, w_ffn_down, w_in, g_q, g_k,
           w_decay, b_decay, g_gla, w_branch, b_gate, w_out, g_final):
    nb, t, d = x.shape
    tc = ctx.shape[1]
    f = w_ffn_down.shape[2]
    assert w_mod.shape[0] == 1, "single layer"
    assert t % GLA_CHUNK == 0 and tc % GLA_CHUNK == 0 and t % GRID_W == 0

    rows = -(-(nb + 1) // SUBLANES) * SUBLANES
    c_rows = jnp.zeros((rows, d), F32).at[:nb].set(c).at[nb].set(c_ctx)
    m = _modulation(c_rows, w_mod[0], b_mod[0]).reshape(rows, N_MOD, d)
    m_lat, m_ctx = m[:nb], m[nb:nb + 1]

    wa = [w_ffn_up[0, i, :, :f].astype(BF16) for i in range(2)]
    wb = [w_ffn_up[0, i, :, f:].astype(BF16) for i in range(2)]
    wd = [w_ffn_down[0, i].astype(BF16) for i in range(2)]
    c_low = _C_GT
    w_low = w_in[0][:, c_low:c_low + 2 * GLA_RANK]
    w_r = jnp.concatenate([w_in[0][:, :c_low], w_in[0][:, c_low + 2 * GLA_RANK:], w_low,
                           jnp.zeros((d, LANES - 2 * GLA_RANK), F32)], axis=-1).astype(BF16)
    w_bd = jnp.zeros((2 * GLA_RANK, 2 * GLA_K_W), F32)
    w_bd = w_bd.at[:GLA_RANK, :GLA_K_W].set(w_decay[0, 0]).at[GLA_RANK:, GLA_K_W:].set(w_decay[0, 1])
    w_bd_hi = w_bd.astype(BF16)
    w_bd_lo = (w_bd - w_bd_hi.astype(F32)).astype(BF16)
    w_dec = jnp.concatenate([w_bd_hi, w_bd_hi, w_bd_lo, jnp.zeros_like(w_bd_hi)], axis=0)
    b_dec = b_decay[0].reshape(1, 2 * GLA_K_W)
    gqk = jnp.concatenate([jnp.tile(g_q[0], ATTN_HEADS), jnp.tile(g_k[0], ATTN_KV_HEADS)]).reshape(1, -1)
    lane = np.arange(LANES)
    seg = jnp.asarray(np.tile((lane[:, None] // HEAD_DIM == lane[None, :] // HEAD_DIM) / HEAD_DIM, (2, 1)), dtype=BF16)
    cos_t, sin_t = _rope_tables(t)
    ones_t, zeros_t = jnp.ones((nb * tc, LANES), F32), jnp.zeros((nb * tc, LANES), F32)
    mf, mb = _gla_constants()

    proj = functools.partial(_in_proj, g=g_norm[0, 1], w_r=w_r, gqk=gqk, seg=seg, w_dec=w_dec,
                             b_dec=b_dec, b_gate=b_gate[0].reshape(1, 2 * d))

    hc = _half_ffn(ctx.reshape(1, nb * tc, d), m_ctx[:, 0:3], g_norm[0, 0], wa[0], wb[0], wd[0])
    pc = proj(hc, m_ctx[:, 3:5], cos_t=ones_t, sin_t=zeros_t)
    _, k_c, vt_c, _, gk_c, _, gvt_c, laf_c, lab_c, _, _ = pc

    h1 = _half_ffn(x, m_lat[:, 0:3], g_norm[0, 0], wa[0], wb[0], wd[0])
    qt, k, vt, gq, gk, gv, gvt, laf, lab, gs, gt = proj(h1, m_lat[:, 3:5], cos_t=cos_t, sin_t=sin_t)
    k_all = jnp.concatenate(
        [k, k_c.reshape(ATTN_KV_HEADS, nb, tc, LANES).transpose(1, 0, 2, 3)], axis=2)
    vt_all = jnp.concatenate(
        [vt, vt_c.reshape(ATTN_KV_HEADS, HEAD_DIM, nb, tc).transpose(2, 0, 1, 3)], axis=3)
    bound = jnp.ceil(1.01 * HEAD_DIM ** 0.5 * float(np.log2(np.e)) * jnp.max(jnp.abs(g_q[0])) * jnp.max(jnp.abs(g_k[0])))
    bounded = bound <= ATTN_MAX_SHIFT
    par = jnp.stack([bounded.astype(F32), jnp.where(bounded, bound, 0.0)])
    attn_o = _attention(par, qt, k_all, vt_all, tc)
    gla_o = _gla(gq, gk, gv, gvt, laf, lab, gk_c, gvt_c, laf_c, lab_c, mf, mb)
    return _merge_ffn(h1, attn_o, gla_o, gs, gt, m_lat[:, 5:6], g_gla[0],
                      w_branch[0, 0].astype(BF16), w_branch[0, 1].astype(BF16), w_out[0].astype(BF16),
                      m_lat[:, 6:9], g_norm[0, 2], wa[1], wb[1], wd[1], g_final)
```

```python
import functools

import numpy as np
import jax
import jax.numpy as jnp
from jax import lax
from jax.experimental import pallas as pl
from jax.experimental.pallas import tpu as pltpu

F32 = jnp.float32
BF16 = jnp.bfloat16

EPS = 1e-6
GRID_W = 64
N_MOD = 9
ATTN_HEADS = 8
ATTN_KV_HEADS = 2
HEAD_DIM = 64
ROPE_AXIS_DIM = HEAD_DIM // 2
ROPE_THETA = 10000.0
GLA_HEADS = 4
GLA_DK = 64
GLA_DV = 128
GLA_RANK = 16
GLA_GATE_NORM = 16.0
ATTN_Q_W = ATTN_HEADS * HEAD_DIM
ATTN_KV_W = ATTN_KV_HEADS * HEAD_DIM
GLA_K_W = GLA_HEADS * GLA_DK
GLA_V_W = GLA_HEADS * GLA_DV
Q_GROUP = ATTN_HEADS // ATTN_KV_HEADS

LANES = 128
SUBLANES = 8
V7X_VMEM_BYTES = 64 * 1024 * 1024
VMEM_LIMIT = V7X_VMEM_BYTES * 7 // 8
TOKEN_TILE = 1024
MERGE_TOKEN_TILE = 512
PROJ_TOKEN_TILE = 1024
MOD_COL_TILE = 1024
ATTN_Q_TILE = 512
GLA_TILE = 4096
GLA_CHUNK = 128


def _dot(a, b):
    return jnp.dot(a, b, preferred_element_type=F32)


def _dot_nt(a, b):
    return lax.dot_general(a, b, (((1,), (1,)), ((), ())), preferred_element_type=F32)


def _sigmoid(x):
    return 0.5 + 0.5 * jnp.tanh(0.5 * x)


def _split2(x):
    hi = x.astype(BF16)
    lo = (x - hi.astype(F32)).astype(BF16)
    return hi, lo


def _rms(x):
    return x * lax.rsqrt(jnp.mean(x * x, axis=-1, keepdims=True) + EPS)


def _const_spec(shape):
    nd = len(shape)
    return pl.BlockSpec(shape, lambda *_: (0,) * nd, pipeline_mode=pl.Buffered(1))


def _mod_kernel(c_ref, w_ref, b_ref, o_ref):
    c = c_ref[...]
    s_hi, s_lo = _split2(c * _sigmoid(c))
    w_hi, w_lo = _split2(w_ref[...])
    o_ref[...] = _dot(s_hi, w_hi) + _dot(s_hi, w_lo) + _dot(s_lo, w_hi) + b_ref[...]


def _modulation(c_rows, w_mod, b_mod):
    rows, d = c_rows.shape
    n = w_mod.shape[1]
    tn = MOD_COL_TILE
    return pl.pallas_call(
        _mod_kernel,
        grid=(n // tn,),
        in_specs=[pl.BlockSpec((rows, d), lambda j: (0, 0)),
                  pl.BlockSpec((d, tn), lambda j: (0, j)),
                  pl.BlockSpec((1, tn), lambda j: (0, j))],
        out_specs=pl.BlockSpec((rows, tn), lambda j: (0, j)),
        out_shape=jax.ShapeDtypeStruct((rows, n), F32),
        compiler_params=pltpu.CompilerParams(dimension_semantics=("arbitrary",),
                                             vmem_limit_bytes=VMEM_LIMIT),
        name="modulation",
    )(c_rows, w_mod, b_mod.reshape(1, n))


def _ffn_math(x, mod_ref, g_ref, wa_ref, wb_ref, wd_ref):
    shift, scale, gate = mod_ref[0, 0:1, :], mod_ref[0, 1:2, :], mod_ref[0, 2:3, :]
    n = ((_rms(x) * g_ref[...]) * (1.0 + scale) + shift).astype(BF16)
    f = wa_ref.shape[1]
    cb = 2 * LANES
    acts = []
    for j in range(f // cb):
        a = _dot(n, wa_ref[:, j * cb:(j + 1) * cb])
        b = _dot(n, wb_ref[:, j * cb:(j + 1) * cb])
        acts.append(((a * _sigmoid(a)) * b).astype(BF16))
    act = jnp.concatenate(acts, axis=-1)
    return x + 0.5 * gate * _dot(act, wd_ref[...])


def _ffn_kernel(h_ref, mod_ref, g_ref, wa_ref, wb_ref, wd_ref, o_ref):
    o_ref[0] = _ffn_math(h_ref[0], mod_ref, g_ref, wa_ref, wb_ref, wd_ref)


def _merge_ffn_kernel(h_ref, ao_ref, go_ref, gs_ref, gt_ref, mg_ref, gg_ref, wb0_ref, wb1_ref, wo_ref,
                      mod_ref, g_ref, wa_ref, wb_ref, wd_ref, gf_ref, o_ref):
    d = h_ref.shape[2]
    go = go_ref[0]
    normed = [_rms(go[:, h * GLA_DV:(h + 1) * GLA_DV]) * gg_ref[...] for h in range(GLA_HEADS)]
    gn = (jnp.concatenate(normed, axis=-1) * gs_ref[0].astype(F32)).astype(BF16)
    y_attn = _dot(ao_ref[0], wb0_ref[...])
    y_gla = _dot(gn, wb1_ref[...])
    gt = gt_ref[0].astype(F32)
    z = (gt[:, :d] * y_attn + gt[:, d:] * y_gla).astype(BF16)
    h2 = h_ref[0] + mg_ref[0] * _dot(z, wo_ref[...])
    out = _ffn_math(h2, mod_ref, g_ref, wa_ref, wb_ref, wd_ref)
    o_ref[0] = _rms(out) * gf_ref[...]


def _ffn_weight_specs(d, f):
    return [_const_spec((1, d)), _const_spec((d, f)), _const_spec((d, f)), _const_spec((f, d))]


def _half_ffn(h, mod3, g, wa, wb, wd, *, tm=TOKEN_TILE):
    nb, t, d = h.shape
    f = wa.shape[1]
    tm = min(tm, t)
    return pl.pallas_call(
        _ffn_kernel,
        grid=(nb, t // tm),
        in_specs=[pl.BlockSpec((1, tm, d), lambda b, i: (b, i, 0)),
                  pl.BlockSpec((1, 3, d), lambda b, i: (b, 0, 0))] + _ffn_weight_specs(d, f),
        out_specs=pl.BlockSpec((1, tm, d), lambda b, i: (b, i, 0)),
        out_shape=jax.ShapeDtypeStruct((nb, t, d), F32),
        compiler_params=pltpu.CompilerParams(dimension_semantics=("arbitrary", "arbitrary"),
                                             vmem_limit_bytes=VMEM_LIMIT),
        name="half_ffn",
    )(h, mod3, g.reshape(1, d), wa, wb, wd)


def _merge_ffn(h, attn_o, gla_o, gs, gt, m_gate, g_gla, wb0, wb1, wo, mod3, g, wa, wb, wd, g_final,
               *, tm=MERGE_TOKEN_TILE):
    nb, t, d = h.shape
    f = wa.shape[1]
    tm = min(tm, t)
    tok = lambda w: pl.BlockSpec((1, tm, w), lambda b, i: (b, i, 0))
    return pl.pallas_call(
        _merge_ffn_kernel,
        grid=(nb, t // tm),
        in_specs=[tok(d), tok(ATTN_Q_W), tok(GLA_V_W), tok(GLA_V_W), tok(2 * d),
                  pl.BlockSpec((1, 1, d), lambda b, i: (b, 0, 0)),
                  _const_spec((1, GLA_DV)), _const_spec((ATTN_Q_W, d)), _const_spec((GLA_V_W, d)),
                  _const_spec((d, d)),
                  pl.BlockSpec((1, 3, d), lambda b, i: (b, 0, 0))] + _ffn_weight_specs(d, f)
                 + [_const_spec((1, d))],
        out_specs=tok(d),
        out_shape=jax.ShapeDtypeStruct((nb, t, d), F32),
        compiler_params=pltpu.CompilerParams(dimension_semantics=("arbitrary", "arbitrary"),
                                             vmem_limit_bytes=VMEM_LIMIT),
        name="merge_ffn_final",
    )(h, attn_o, gla_o, gs, gt, m_gate, g_gla.reshape(1, GLA_DV), wb0, wb1, wo,
      mod3, g.reshape(1, d), wa, wb, wd, g_final.reshape(1, d))


_C_Q = 0
_C_K = _C_Q + ATTN_Q_W
_C_V = _C_K + ATTN_KV_W
_C_GQ = _C_V + ATTN_KV_W
_C_GK = _C_GQ + GLA_K_W
_C_GV = _C_GK + GLA_K_W
_C_GS = _C_GV + GLA_V_W
_C_GT = _C_GS + GLA_V_W


def _proj_kernel(h_ref, mod_ref, g_ref, w_ref, gqk_ref, seg_ref, cos_ref, sin_ref, wdec_ref,
                 bdec_ref, bgate_ref,
                 qt_ref, k_ref, vt_ref, gq_ref, gk_ref, gv_ref, gvt_ref, laf_ref, lab_ref, gs_ref, gt_ref):
    d = h_ref.shape[2]
    c_low = _C_GT + 2 * d
    x = h_ref[0]
    shift, scale = mod_ref[0, 0:1, :], mod_ref[0, 1:2, :]
    n = ((_rms(x) * g_ref[...]) * (1.0 + scale) + shift).astype(BF16)
    tm = x.shape[0]

    qkv = _dot(n, w_ref[:, _C_Q:_C_GQ])
    qk = qkv[:, :_C_V]
    low = _dot(n, w_ref[:, c_low:c_low + LANES])
    gt = _dot(n, w_ref[:, _C_GT:c_low]) + bgate_ref[...]

    seg = seg_ref[...]
    ms = []
    for j in range((ATTN_Q_W + ATTN_KV_W) // LANES):
        sq = qk[:, j * LANES:(j + 1) * LANES]
        hi, lo = _split2(sq * sq)
        ms.append(_dot(jnp.concatenate([hi, lo], axis=-1), seg))
    ms = jnp.concatenate(ms, axis=-1)
    gs = _dot(n, w_ref[:, _C_GS:_C_GT])

    l_hi = low.astype(BF16).astype(F32)
    l_lo = low - l_hi
    packed = l_hi + pltpu.roll(l_lo, 2 * GLA_RANK, 1) + pltpu.roll(l_hi, 4 * GLA_RANK, 1)
    z = _dot(packed.astype(BF16), wdec_ref[...]) + bdec_ref[...]
    gv = _dot(n, w_ref[:, _C_GV:_C_GS])
    vt = qkv[:, _C_V:_C_GQ].T
    gqk = _dot(n, w_ref[:, _C_GQ:_C_GV])

    qk = qk * lax.rsqrt(ms + EPS) * gqk_ref[...]
    lane = lax.broadcasted_iota(jnp.int32, (tm, LANES), 1)
    first = (lane % HEAD_DIM) < (HEAD_DIM // 2)
    cos, sin = cos_ref[...], sin_ref[...]
    rot = []
    for j in range((ATTN_Q_W + ATTN_KV_W) // LANES):
        xs = qk[:, j * LANES:(j + 1) * LANES]
        other = jnp.where(first, pltpu.roll(xs, LANES - HEAD_DIM // 2, 1), pltpu.roll(xs, HEAD_DIM // 2, 1))
        rot.append(xs * cos + other * sin)
    q_scale = HEAD_DIM ** -0.5 * float(np.log2(np.e))
    for j in range(ATTN_Q_W // LANES):
        qt_ref[0, j * LANES:(j + 1) * LANES, :] = (rot[j] * q_scale).T.astype(BF16)
    k_rot = rot[ATTN_Q_W // LANES]
    ones_lane = jnp.where(lane == HEAD_DIM, 1.0, 0.0)
    for g in range(ATTN_KV_HEADS):
        k_head = k_rot if g == 0 else pltpu.roll(k_rot, LANES - g * HEAD_DIM, 1)
        k_ref[0, g] = jnp.where(lane < HEAD_DIM, k_head, ones_lane).astype(BF16)

    for g in range(ATTN_KV_HEADS):
        vt_ref[0, g] = vt[g * HEAD_DIM:(g + 1) * HEAD_DIM].astype(BF16)

    gq_ref[0] = (gqk[:, :GLA_K_W] * (GLA_DK ** -0.5)).astype(BF16)
    gk_ref[0] = gqk[:, GLA_K_W:].astype(BF16)
    gv_ref[0] = gv.astype(BF16)
    gvt_ref[0] = gv.T.astype(BF16)

    la = (jnp.minimum(z, 0.0) - jnp.log(1.0 + jnp.exp(-jnp.abs(z)))) * (1.0 / GLA_GATE_NORM)
    laf_ref[0] = la[:, :GLA_K_W]
    lab_ref[0] = la[:, GLA_K_W:]

    gs_ref[0] = (gs * _sigmoid(gs)).astype(BF16)
    gt_ref[0] = _sigmoid(gt).astype(BF16)


def _in_proj(h, mod2, g, w_r, gqk, seg, cos_t, sin_t, w_dec, b_dec, b_gate, *, tm=PROJ_TOKEN_TILE):
    nb, t, d = h.shape
    tm = min(tm, t)
    wp = w_r.shape[1]
    tok = lambda w: pl.BlockSpec((1, tm, w), lambda b, i: (b, i, 0))
    out_shape = [
        jax.ShapeDtypeStruct((nb, ATTN_Q_W, t), BF16),
        jax.ShapeDtypeStruct((nb, ATTN_KV_HEADS, t, LANES), BF16),
        jax.ShapeDtypeStruct((nb, ATTN_KV_HEADS, HEAD_DIM, t), BF16),
        jax.ShapeDtypeStruct((nb, t, GLA_K_W), BF16),
        jax.ShapeDtypeStruct((nb, t, GLA_K_W), BF16),
        jax.ShapeDtypeStruct((nb, t, GLA_V_W), BF16),
        jax.ShapeDtypeStruct((nb, GLA_V_W, t), BF16),
        jax.ShapeDtypeStruct((nb, t, GLA_K_W), F32),
        jax.ShapeDtypeStruct((nb, t, GLA_K_W), F32),
        jax.ShapeDtypeStruct((nb, t, GLA_V_W), BF16),
        jax.ShapeDtypeStruct((nb, t, 2 * d), BF16),
    ]
    out_specs = [
        pl.BlockSpec((1, ATTN_Q_W, tm), lambda b, i: (b, 0, i)),
        pl.BlockSpec((1, ATTN_KV_HEADS, tm, LANES), lambda b, i: (b, 0, i, 0)),
        pl.BlockSpec((1, ATTN_KV_HEADS, HEAD_DIM, tm), lambda b, i: (b, 0, 0, i)),
        tok(GLA_K_W), tok(GLA_K_W), tok(GLA_V_W),
        pl.BlockSpec((1, GLA_V_W, tm), lambda b, i: (b, 0, i)),
        tok(GLA_K_W), tok(GLA_K_W), tok(GLA_V_W), tok(2 * d),
    ]
    in_specs = [
        pl.BlockSpec((1, tm, d), lambda b, i: (b, i, 0)),
        pl.BlockSpec((1, 2, d), lambda b, i: (b, 0, 0)),
        _const_spec((1, d)), _const_spec((d, wp)), _const_spec((1, ATTN_Q_W + ATTN_KV_W)),
        _const_spec((2 * LANES, LANES)),
        pl.BlockSpec((tm, LANES), lambda b, i: (i, 0)),
        pl.BlockSpec((tm, LANES), lambda b, i: (i, 0)),
        _const_spec((LANES, 2 * GLA_K_W)),
        _const_spec((1, 2 * GLA_K_W)), _const_spec((1, 2 * d)),
    ]
    return pl.pallas_call(
        _proj_kernel,
        grid=(nb, t // tm),
        in_specs=in_specs,
        out_specs=out_specs,
        out_shape=out_shape,
        compiler_params=pltpu.CompilerParams(dimension_semantics=("arbitrary", "arbitrary"),
                                             vmem_limit_bytes=VMEM_LIMIT),
        name="in_proj",
    )(h, mod2, g.reshape(1, d), w_r, gqk, seg, cos_t, sin_t, w_dec, b_dec, b_gate)


ATTN_GROUPS = 2
ATTN_ROW_BLOCK = 256
ATTN_FAST_ROW_BLOCK = 256
ATTN_KC = 1024
ATTN_UNROLL = 3
ATTN_BLOCKS_PER_ITER = 33
ATTN_MAX_SHIFT = 50.0


def _attn_plan(t, tc):
    if tc % (2 * LANES) == 0:
        kc = max(c for c in range(LANES, ATTN_KC + 1, LANES) if t % c == 0)
        edge = tc // 2
        chunks = [(t, edge)] + [(i * kc, kc) for i in range(t // kc)] + [(t + edge, edge)]
        return chunks, (1, len(chunks) - 1)
    s_len = t + tc
    kc = max(c for c in range(LANES, ATTN_KC + 1, LANES) if s_len % c == 0)
    chunks = [(i * kc, kc) for i in range(s_len // kc)]
    return chunks, (0, len(chunks))


def _sublane_partial_sums(p):
    return p.astype(F32).reshape(p.shape[0] // SUBLANES, SUBLANES, p.shape[1]).sum(axis=0)


def _attn_kernel(par_ref, qt_ref, k_ref, vt_ref, o_ref, qs_ref, *group_refs, chunks, run):
    tq = qt_ref.shape[2]
    s_len = k_ref.shape[2]
    gw = Q_GROUP * tq // ATTN_GROUPS
    per = len(group_refs) // ATTN_GROUPS
    m_refs, cmax_refs, l_refs, acc_refs, s_refs = (
        [group_refs[g * per + j] for g in range(ATTN_GROUPS)] for j in range(per))
    for h in range(Q_GROUP):
        qs_ref[0:HEAD_DIM, h * tq:(h + 1) * tq] = qt_ref[0, h * HEAD_DIM:(h + 1) * HEAD_DIM, :]
    extra = lax.broadcasted_iota(jnp.int32, (LANES - HEAD_DIM, Q_GROUP * tq), 0)
    qs_ref[HEAD_DIM:LANES, :] = jnp.where(extra == 0, -par_ref[1], 0.0).astype(BF16)
    for g in range(ATTN_GROUPS):
        acc_refs[g][...] = jnp.zeros(acc_refs[g].shape, F32)
        l_refs[g][...] = jnp.zeros(l_refs[g].shape, F32)

    def rows(off, j, rb):
        start = off + j * rb
        return pl.ds(start if isinstance(start, int) else pl.multiple_of(start, LANES), rb)

    def bounded_scores():
        rb = ATTN_FAST_ROW_BLOCK if s_len % ATTN_FAST_ROW_BLOCK == 0 else LANES
        n_blocks = s_len // rb
        per_iter = max(d for d in range(1, ATTN_BLOCKS_PER_ITER + 1) if n_blocks % d == 0)

        def body(it, carry):
            sums = [None] * ATTN_GROUPS
            dens = [None] * ATTN_GROUPS
            pending = None

            def finish(unit):
                blk, g, p = unit
                part = _dot(vt_ref[0, 0, :, blk], p)
                sums[g] = part if sums[g] is None else sums[g] + part
                den = _sublane_partial_sums(p)
                dens[g] = den if dens[g] is None else dens[g] + den

            for j in range(per_iter):
                blk = rows(it * (per_iter * rb), j, rb)
                for g in range(ATTN_GROUPS):
                    s = _dot(k_ref[0, 0, blk, :], qs_ref[:, g * gw:(g + 1) * gw])
                    if pending is not None:
                        finish(pending)
                    pending = (blk, g, jnp.exp2(s).astype(BF16))
            finish(pending)
            for g in range(ATTN_GROUPS):
                acc_refs[g][...] += sums[g]
                l_refs[g][...] += jnp.sum(dens[g], axis=0, keepdims=True)
            return carry

        lax.fori_loop(0, n_blocks // per_iter, body, 0)

    def online_softmax():
        for g in range(ATTN_GROUPS):
            m_refs[g][...] = jnp.full(m_refs[g].shape, -jnp.inf, F32)

        def stage(score, apply):
            n_s = n_a = 0
            if score is not None:
                off_s, size_s, gs = score
                rb_s = min(ATTN_ROW_BLOCK, size_s)
                n_s = size_s // rb_s
            if apply is not None:
                off_a, size_a, ga = apply
                rb_a = min(ATTN_ROW_BLOCK, size_a)
                n_a = size_a // rb_a
                m_old = m_refs[ga][...]
                m_new = jnp.maximum(m_old, cmax_refs[ga][...])
                m_refs[ga][...] = m_new
            cmax = pv_sum = den = None
            for j in range(max(n_s, n_a)):
                if j < n_a:
                    p = jnp.exp2(s_refs[ga][j * rb_a:(j + 1) * rb_a, :] - m_new).astype(BF16)
                if j < n_s:
                    s = _dot(k_ref[0, 0, rows(off_s, j, rb_s), :], qs_ref[:, gs * gw:(gs + 1) * gw])
                    s_refs[gs][j * rb_s:(j + 1) * rb_s, :] = s
                    bmax = jnp.max(s, axis=0, keepdims=True)
                    cmax = bmax if cmax is None else jnp.maximum(cmax, bmax)
                if j < n_a:
                    part = _dot(vt_ref[0, 0, :, rows(off_a, j, rb_a)], p)
                    pv_sum = part if pv_sum is None else pv_sum + part
                    dj = _sublane_partial_sums(p)
                    den = dj if den is None else den + dj
            if score is not None:
                cmax_refs[gs][...] = cmax
            if apply is not None:
                alpha = jnp.exp2(m_old - m_new)
                acc_refs[ga][...] = alpha * acc_refs[ga][...] + pv_sum
                l_refs[ga][...] = alpha * l_refs[ga][...] + jnp.sum(den, axis=0, keepdims=True)

        last = ATTN_GROUPS - 1

        def chunk_stages(cur, prev):
            stage(cur + (0,), None if prev is None else prev + (last,))
            for g in range(1, ATTN_GROUPS):
                stage(cur + (g,), cur + (g - 1,))

        a, b = run
        trips = b - a - 1
        unroll = ATTN_UNROLL if trips >= ATTN_UNROLL else 1
        first_loop = a + 1 + trips % unroll
        for i in range(first_loop):
            chunk_stages(chunks[i], chunks[i - 1] if i else None)
        if b > first_loop:
            off0, kc = chunks[first_loop]

            def body(i, carry):
                for u in range(unroll):
                    off = off0 + (i * unroll + u) * kc
                    chunk_stages((off, kc), (off - kc, kc))
                return carry

            lax.fori_loop(0, (b - first_loop) // unroll, body, 0)
        for i in range(b, len(chunks)):
            chunk_stages(chunks[i], chunks[i - 1])
        stage(None, chunks[-1] + (last,))

    pl.when(par_ref[0] > 0.0)(bounded_scores)
    pl.when(par_ref[0] <= 0.0)(online_softmax)

    acc = jnp.concatenate([r[...] for r in acc_refs], axis=1)
    ot = acc / jnp.concatenate([r[...] for r in l_refs], axis=1)
    for j in range(Q_GROUP // 2):
        pair = jnp.concatenate([ot[:, (2 * j) * tq:(2 * j + 1) * tq], ot[:, (2 * j + 1) * tq:(2 * j + 2) * tq]],
                               axis=0)
        o_ref[0, :, j * LANES:(j + 1) * LANES] = pair.T.astype(o_ref.dtype)


def _attention(par, qt, k, vt, t_ctx, *, tq=ATTN_Q_TILE):
    nb, _, t = qt.shape
    s_len = k.shape[2]
    tq = min(tq, t)
    chunks, run = _attn_plan(t, t_ctx)
    kc = max(size for _, size in chunks)
    gw = Q_GROUP * tq // ATTN_GROUPS
    return pl.pallas_call(
        functools.partial(_attn_kernel, chunks=tuple(chunks), run=run),
        grid=(nb, ATTN_KV_HEADS, t // tq),
        in_specs=[
            pl.BlockSpec(memory_space=pltpu.SMEM),
            pl.BlockSpec((1, Q_GROUP * HEAD_DIM, tq), lambda b, g, i: (b, g, i)),
            pl.BlockSpec((1, 1, s_len, LANES), lambda b, g, i: (b, g, 0, 0)),
            pl.BlockSpec((1, 1, HEAD_DIM, s_len), lambda b, g, i: (b, g, 0, 0)),
        ],
        out_specs=pl.BlockSpec((1, tq, Q_GROUP * HEAD_DIM), lambda b, g, i: (b, i, g)),
        out_shape=jax.ShapeDtypeStruct((nb, t, ATTN_Q_W), BF16),
        scratch_shapes=[pltpu.VMEM((LANES, Q_GROUP * tq), BF16)] + ATTN_GROUPS * [
            pltpu.VMEM((1, gw), F32),
            pltpu.VMEM((1, gw), F32),
            pltpu.VMEM((1, gw), F32),
            pltpu.VMEM((HEAD_DIM, gw), F32),
            pltpu.VMEM((kc, gw), F32)],
        compiler_params=pltpu.CompilerParams(dimension_semantics=("arbitrary", "arbitrary", "arbitrary"),
                                             vmem_limit_bytes=VMEM_LIMIT),
        name="flash_attention",
    )(par, qt, k, vt)


def _gla_constants():
    c = GLA_CHUNK
    i = np.arange(c)[:, None]
    t = np.arange(c)[None, :]
    fwd = [t > i, t <= i]
    bwd = [t < i, t >= i]
    s = c // 2
    while s >= 2:
        mid = (i // (2 * s)) * (2 * s) + s
        second = (i % (2 * s)) >= s
        fwd.append(np.where(second, (t >= mid) & (t <= i), (t > i) & (t < mid)))
        bwd.append(np.where(second, (t >= mid) & (t < i), (t >= i) & (t < mid)))
        s //= 2
    to = lambda blocks: jnp.asarray(np.concatenate(blocks, axis=0).astype(np.float32), dtype=BF16)
    return to(fwd), to(bwd)


_GLA_LEVELS = int(np.log2(GLA_CHUNK))


def _cum(mat, la):
    hi, lo = _split2(la)
    r = _dot(mat, jnp.concatenate([hi, lo], axis=-1))
    return r[:, :LANES] + r[:, LANES:]


def _gla_kernel(q_ref, k_ref, v_ref, vt_ref, laf_ref, lab_ref, kc_ref, vtc_ref, lafc_ref, labc_ref,
                mf_ref, mb_ref, o_ref, st_ref, dec_ref, *, cpt):
    c = GLA_CHUNK
    n_lat = k_ref.shape[1] // c
    n_ctx = kc_ref.shape[1] // c
    n_all = n_ctx + n_lat
    tile = pl.program_id(2)
    lane = lax.broadcasted_iota(jnp.int32, (c, LANES), 1)
    head_masks = [lane < GLA_DK, lane >= GLA_DK]
    lane2 = lax.broadcasted_iota(jnp.int32, (c, 2 * LANES), 1) % LANES
    pair_masks = [lane2 < GLA_DK, lane2 >= GLA_DK]

    def increments(k_r, vt_r, laf_r, lab_r, n0, slot0, count):
        offs = [pl.multiple_of((n0 + i) * c, c) for i in range(count)]
        rfs = [_cum(mf_ref[0:2 * c, :], laf_r[0, pl.ds(off, c), :]) for off in offs]
        rbs = [_cum(mb_ref[0:2 * c, :], lab_r[0, pl.ds(off, c), :]) for off in offs]
        for i, (off, rf, rb) in enumerate(zip(offs, rfs, rbs)):
            k = k_r[0, pl.ds(off, c), :].astype(F32)
            kfb = jnp.concatenate([k * jnp.exp(rf[0:c]),
                                   k * jnp.exp(rb[0:c])], axis=-1)
            kk = jnp.concatenate([jnp.where(pair_masks[0], kfb, 0.0), jnp.where(pair_masks[1], kfb, 0.0)],
                                 axis=0).astype(BF16)
            vt2 = jnp.concatenate([vt_r[0, 0:GLA_DV, pl.ds(off, c)], vt_r[0, GLA_DV:2 * GLA_DV, pl.ds(off, c)]],
                                  axis=-1)
            st_ref[slot0 + i] = _dot(vt2, kk)
            dec_ref[slot0 + i, 0:1, :] = jnp.exp(rf[2 * c - 1:2 * c, :])
            dec_ref[slot0 + i, 1:2, :] = jnp.exp(rb[c:c + 1, :])

    def group_size(n):
        return max(g for g in (4, 2, 1) if n % g == 0)

    @pl.when(tile == 0)
    def _():
        gc, gl = group_size(n_ctx), group_size(n_lat)

        def ctx_body(i, carry):
            increments(kc_ref, vtc_ref, lafc_ref, labc_ref, i * gc, i * gc, gc)
            return carry
        lax.fori_loop(0, n_ctx // gc, ctx_body, 0)

        def lat_body(i, carry):
            increments(k_ref, vt_ref, laf_ref, lab_ref, i * gl, n_ctx + i * gl, gl)
            return carry
        lax.fori_loop(0, n_lat // gl, lat_body, 0)

        def fwd_body(s, st):
            inc = st_ref[s, :, 0:LANES]
            st_ref[s, :, 0:LANES] = st
            return st * dec_ref[s, 0:1, :] + inc
        lax.fori_loop(0, n_all, fwd_body, jnp.zeros((GLA_DV, LANES), F32))

        def bwd_body(j, st, base, count):
            s = base + count - 1 - j
            inc = st_ref[s, :, LANES:2 * LANES]
            st_ref[s, :, LANES:2 * LANES] = st
            return st * dec_ref[s, 1:2, :] + inc
        st = lax.fori_loop(0, n_ctx, functools.partial(bwd_body, base=0, count=n_ctx),
                           jnp.zeros((GLA_DV, LANES), F32))
        lax.fori_loop(0, n_lat, functools.partial(bwd_body, base=n_ctx, count=n_lat), st)

    xor2 = lax.broadcasted_iota(jnp.int32, (c, 2 * c), 0) ^ (lax.broadcasted_iota(jnp.int32, (c, 2 * c), 1) & (c - 1))
    row_l = lax.broadcasted_iota(jnp.int32, (c, LANES), 0)
    lane2v = lax.broadcasted_iota(jnp.int32, (c, 2 * GLA_DV), 1)
    lane2s = lax.broadcasted_iota(jnp.int32, (GLA_DV, 2 * LANES), 1) % LANES

    def stack_heads(x):
        return jnp.concatenate([jnp.where(head_masks[0], x, 0.0), jnp.where(head_masks[1], x, 0.0)], axis=0)

    go = group_size(cpt)

    def out_body(i, carry):
        idx = range(go)
        ns = [tile * cpt + i * go + u for u in idx]
        offs = [pl.multiple_of(n * c, c) for n in ns]
        locs = [pl.multiple_of((i * go + u) * c, c) for u in idx]
        lafs = [laf_ref[0, pl.ds(off, c), :] for off in offs]
        labs = [lab_ref[0, pl.ds(off, c), :] for off in offs]
        rfs = [_cum(mf_ref[c:, :], la) for la in lafs]
        rbs = [_cum(mb_ref[c:, :], la) for la in labs]
        qs = [q_ref[0, pl.ds(loc, c), :].astype(F32) for loc in locs]
        ks = [k_ref[0, pl.ds(off, c), :].astype(F32) for off in offs]
        a = [2.0 * _dot_nt(qs[u].astype(BF16), stack_heads(ks[u]).astype(BF16)) for u in idx]
        for lvl in range(_GLA_LEVELS):
            sh = _GLA_LEVELS - 1 - lvl
            second = ((row_l >> sh) & 1) == 1
            for u in idx:
                if sh > 0:
                    ef = jnp.exp(rfs[u][(1 + lvl) * c:(2 + lvl) * c])
                    eb = jnp.exp(rbs[u][(1 + lvl) * c:(2 + lvl) * c])
                    ql = (qs[u] * jnp.where(second, ef, eb)).astype(BF16)
                    kl = stack_heads(ks[u] * jnp.where(second, eb, ef)).astype(BF16)
                else:
                    ql = (qs[u] * jnp.exp(jnp.where(second, lafs[u], labs[u]))).astype(BF16)
                    kl = stack_heads(ks[u]).astype(BF16)
                a[u] = jnp.where((xor2 >> sh) == 1, _dot_nt(ql, kl), a[u])
        for u in idx:
            v = v_ref[0, pl.ds(locs[u], c), :]
            v_bd = jnp.concatenate([jnp.where(lane2v < GLA_DV, v, jnp.zeros_like(v)),
                                    jnp.where(lane2v >= GLA_DV, v, jnp.zeros_like(v))], axis=0)
            q_inter = jnp.concatenate([qs[u] * jnp.exp(rfs[u][0:c]), qs[u] * jnp.exp(rbs[u][0:c])],
                                      axis=-1).astype(BF16)
            states = st_ref[n_ctx + ns[u]]
            st2 = jnp.concatenate([jnp.where(lane2s < GLA_DK, states, 0.0),
                                   jnp.where(lane2s >= GLA_DK, states, 0.0)], axis=0).astype(BF16)
            o_ref[0, pl.ds(locs[u], c), :] = _dot(a[u].astype(BF16), v_bd) + _dot_nt(q_inter, st2)
        return carry

    lax.fori_loop(0, cpt // go, out_body, 0)


def _gla(gq, gk, gv, gvt, laf, lab, gk_c, gvt_c, laf_c, lab_c, mf, mb, *, tile=GLA_TILE):
    nb, t, _ = gq.shape
    tc = gk_c.shape[1] // nb
    tile = min(tile, t)
    c = GLA_CHUNK
    n_all = (t + tc) // c
    pair_k = 2 * GLA_DK
    pair_v = 2 * GLA_DV
    return pl.pallas_call(
        functools.partial(_gla_kernel, cpt=tile // c),
        grid=(nb, GLA_HEADS // 2, t // tile),
        in_specs=[
            pl.BlockSpec((1, tile, pair_k), lambda b, p, i: (b, i, p)),
            pl.BlockSpec((1, t, pair_k), lambda b, p, i: (b, 0, p)),
            pl.BlockSpec((1, tile, pair_v), lambda b, p, i: (b, i, p)),
            pl.BlockSpec((1, pair_v, t), lambda b, p, i: (b, p, 0)),
            pl.BlockSpec((1, t, pair_k), lambda b, p, i: (b, 0, p)),
            pl.BlockSpec((1, t, pair_k), lambda b, p, i: (b, 0, p)),
            pl.BlockSpec((1, tc, pair_k), lambda b, p, i: (0, b, p)),
            pl.BlockSpec((1, pair_v, tc), lambda b, p, i: (0, p, b)),
            pl.BlockSpec((1, tc, pair_k), lambda b, p, i: (0, b, p)),
            pl.BlockSpec((1, tc, pair_k), lambda b, p, i: (0, b, p)),
            _const_spec(tuple(mf.shape)), _const_spec(tuple(mb.shape)),
        ],
        out_specs=pl.BlockSpec((1, tile, pair_v), lambda b, p, i: (b, i, p)),
        out_shape=jax.ShapeDtypeStruct((nb, t, GLA_V_W), F32),
        scratch_shapes=[pltpu.VMEM((n_all, GLA_DV, 2 * LANES), F32),
                        pltpu.VMEM((n_all, SUBLANES, LANES), F32)],
        compiler_params=pltpu.CompilerParams(dimension_semantics=("arbitrary", "arbitrary", "arbitrary"),
                                             vmem_limit_bytes=VMEM_LIMIT),
        name="gla",
    )(gq, gk, gv, gvt, laf, lab, gk_c, gvt_c, laf_c, lab_c, mf, mb)


def _rope_tables(t):
    rows = t // GRID_W
    row = jnp.repeat(jnp.arange(rows, dtype=F32), GRID_W)
    col = jnp.tile(jnp.arange(GRID_W, dtype=F32), rows)
    freqs = ROPE_THETA ** (-jnp.arange(0, ROPE_AXIS_DIM, 2, dtype=F32) / ROPE_AXIS_DIM)
    ang = jnp.concatenate([row[:, None] * freqs, col[:, None] * freqs], axis=-1)
    cos, sin = jnp.cos(ang), jnp.sin(ang)
    reps = LANES // HEAD_DIM
    return (jnp.tile(jnp.concatenate([cos, cos], axis=-1), (1, reps)),
            jnp.tile(jnp.concatenate([-sin, sin], axis=-1), (1, reps)))


def kernel(x, c, ctx, c_ctx, w_mod, b_mod, g_norm, w_ffn_up, w_ffn_down, w_in, g_q, g_k,
           w_decay, b_decay, g_gla, w_branch, b_gate, w_out, g_final):
    nb, t, d = x.shape
    tc = ctx.shape[1]
    f = w_ffn_down.shape[2]
    assert w_mod.shape[0] == 1, "single layer"
    assert t % GLA_CHUNK == 0 and tc % GLA_CHUNK == 0 and t % GRID_W == 0

    rows = -(-(nb + 1) // SUBLANES) * SUBLANES
    c_rows = jnp.zeros((rows, d), F32).at[:nb].set(c).at[nb].set(c_ctx)
    m = _modulation(c_rows, w_mod[0], b_mod[0]).reshape(rows, N_MOD, d)
    m_lat, m_ctx = m[:nb], m[nb:nb + 1]

    wa = [w_ffn_up[0, i, :, :f].astype(BF16) for i in range(2)]
    wb = [w_ffn_up[0, i, :, f:].astype(BF16) for i in range(2)]
    wd = [w_ffn_down[0, i].astype(BF16) for i in range(2)]
    c_low = _C_GT
    w_low = w_in[0][:, c_low:c_low + 2 * GLA_RANK]
    w_r = jnp.concatenate([w_in[0][:, :c_low], w_in[0][:, c_low + 2 * GLA_RANK:], w_low,
                           jnp.zeros((d, LANES - 2 * GLA_RANK), F32)], axis=-1).astype(BF16)
    w_bd = jnp.zeros((2 * GLA_RANK, 2 * GLA_K_W), F32)
    w_bd = w_bd.at[:GLA_RANK, :GLA_K_W].set(w_decay[0, 0]).at[GLA_RANK:, GLA_K_W:].set(w_decay[0, 1])
    w_bd_hi = w_bd.astype(BF16)
    w_bd_lo = (w_bd - w_bd_hi.astype(F32)).astype(BF16)
    w_dec = jnp.concatenate([w_bd_hi, w_bd_hi, w_bd_lo, jnp.zeros_like(w_bd_hi)], axis=0)
    b_dec = b_decay[0].reshape(1, 2 * GLA_K_W)
    gqk = jnp.concatenate([jnp.tile(g_q[0], ATTN_HEADS), jnp.tile(g_k[0], ATTN_KV_HEADS)]).reshape(1, -1)
    lane = np.arange(LANES)
    seg = jnp.asarray(np.tile((lane[:, None] // HEAD_DIM == lane[None, :] // HEAD_DIM) / HEAD_DIM, (2, 1)), dtype=BF16)
    cos_t, sin_t = _rope_tables(t)
    ones_t, zeros_t = jnp.ones((nb * tc, LANES), F32), jnp.zeros((nb * tc, LANES), F32)
    mf, mb = _gla_constants()

    proj = functools.partial(_in_proj, g=g_norm[0, 1], w_r=w_r, gqk=gqk, seg=seg, w_dec=w_dec,
                             b_dec=b_dec, b_gate=b_gate[0].reshape(1, 2 * d))

    hc = _half_ffn(ctx.reshape(1, nb * tc, d), m_ctx[:, 0:3], g_norm[0, 0], wa[0], wb[0], wd[0])
    pc = proj(hc, m_ctx[:, 3:5], cos_t=ones_t, sin_t=zeros_t)
    _, k_c, vt_c, _, gk_c, _, gvt_c, laf_c, lab_c, _, _ = pc

    h1 = _half_ffn(x, m_lat[:, 0:3], g_norm[0, 0], wa[0], wb[0], wd[0])
    qt, k, vt, gq, gk, gv, gvt, laf, lab, gs, gt = proj(h1, m_lat[:, 3:5], cos_t=cos_t, sin_t=sin_t)
    k_all = jnp.concatenate(
        [k, k_c.reshape(ATTN_KV_HEADS, nb, tc, LANES).transpose(1, 0, 2, 3)], axis=2)
    vt_all = jnp.concatenate(
        [vt, vt_c.reshape(ATTN_KV_HEADS, HEAD_DIM, nb, tc).transpose(2, 0, 1, 3)], axis=3)
    bound = jnp.ceil(1.01 * HEAD_DIM ** 0.5 * float(np.log2(np.e)) * jnp.max(jnp.abs(g_q[0])) * jnp.max(jnp.abs(g_k[0])))
    bounded = bound <= ATTN_MAX_SHIFT
    par = jnp.stack([bounded.astype(F32), jnp.where(bounded, bound, 0.0)])
    attn_o = _attention(par, qt, k_all, vt_all, tc)
    gla_o = _gla(gq, gk, gv, gvt, laf, lab, gk_c, gvt_c, laf_c, lab_c, mf, mb)
    return _merge_ffn(h1, attn_o, gla_o, gs, gt, m_lat[:, 5:6], g_gla[0],
                      w_branch[0, 0].astype(BF16), w_branch[0, 1].astype(BF16), w_out[0].astype(BF16),
                      m_lat[:, 6:9], g_norm[0, 2], wa[1], wb[1], wd[1], g_final)
```
